```python
import math
import jax, jax.numpy as jnp
from jax import lax
import numpy as np

D_MODEL = 1024
BATCH = 8
SEQ = 8192
DEPTH = 1

PLE_DIM = 256
ATTN_HEADS = 8
HEAD_DIM = 64
ATTN_WIDTH = ATTN_HEADS * HEAD_DIM
MOBA_BLOCK = 256
MOBA_TOPK = 3
Q_CHUNK = 128
REL_BUCKETS = 32
REL_MAX_DIST = 128
SSM_WIDTH = 512
SSM_GROUP = 16
SSM_GROUPS = SSM_WIDTH // SSM_GROUP
SSM_STATE = 64
DT_MIN = 1e-3
DT_MAX = 1e-1
IN_SIZES = (ATTN_WIDTH, ATTN_WIDTH, ATTN_WIDTH, ATTN_WIDTH,
            SSM_WIDTH, SSM_WIDTH,
            D_MODEL, D_MODEL)
IN_WIDTH = sum(IN_SIZES)
IN_SPLITS = tuple(int(s) for s in np.cumsum(IN_SIZES)[:-1])
DEEPNORM_ALPHA = (2.0 * DEPTH) ** 0.25
DEEPNORM_BETA = (8.0 * DEPTH) ** -0.25
LN_EPS = 1e-5

kernel_name = "moba_s5_gated_hybrid_deepnorm"


def t5_bucket(dist):
    max_exact = REL_BUCKETS // 2
    is_small = dist < max_exact
    d = jnp.maximum(dist, 1).astype(jnp.float32)
    large = max_exact + (jnp.log(d / max_exact) / math.log(REL_MAX_DIST / max_exact)
                         * (REL_BUCKETS - max_exact)).astype(jnp.int32)
    large = jnp.minimum(large, REL_BUCKETS - 1)
    return jnp.where(is_small, dist, large)


def moba_attention(q, k, v, rel_bias):
    B, H, S, Dh = q.shape
    f32 = jnp.float32
    nb = -(-S // MOBA_BLOCK)
    s_pad = nb * MOBA_BLOCK
    pad = ((0, 0), (0, 0), (0, s_pad - S), (0, 0))
    q = jnp.pad(q, pad)
    k = jnp.pad(k, pad)
    v = jnp.pad(v, pad)
    kb = k.reshape(B, H, nb, MOBA_BLOCK, Dh)
    vb = v.reshape(B, H, nb, MOBA_BLOCK, Dh)
    k_mean = kb.astype(f32).mean(axis=3)
    topk = min(MOBA_TOPK, nb)
    n_chunks = s_pad // Q_CHUNK
    q_chunks = q.reshape(B, H, n_chunks, Q_CHUNK, Dh).transpose(2, 0, 1, 3, 4)
    bias_table = rel_bias.T
    scale = Dh ** -0.5
    offs = jnp.arange(MOBA_BLOCK)
    b_ix = jnp.arange(B)[:, None, None, None]
    h_ix = jnp.arange(H)[None, :, None, None]
    h_ix5 = jnp.arange(H)[None, :, None, None, None]

    def chunk_fn(args):
        qc, ci = args
        q_pos = ci * Q_CHUNK + jnp.arange(Q_CHUNK)
        blk = (ci * Q_CHUNK) // MOBA_BLOCK
        gate = jnp.einsum('bhqd,bhnd->bhqn', qc.astype(f32), k_mean)
        past = jnp.arange(nb) < blk
        gate = jnp.where(past, gate, -jnp.inf)
        _, idx = lax.top_k(gate, topk)
        valid = idx < blk
        k_sel = kb[b_ix, h_ix, idx]
        v_sel = vb[b_ix, h_ix, idx]
        k_pos = idx[..., None] * MOBA_BLOCK + offs
        logit_sel = jnp.einsum('bhqd,bhqnkd->bhqnk', qc, k_sel).astype(f32) * scale
        dist_sel = jnp.maximum(q_pos[None, None, :, None, None] - k_pos, 0)
        bias_sel = bias_table[h_ix5, t5_bucket(dist_sel)].astype(f32)
        logit_sel = jnp.where(valid[..., None], logit_sel + bias_sel, -jnp.inf)
        k_own = lax.dynamic_index_in_dim(kb, blk, axis=2, keepdims=False)
        v_own = lax.dynamic_index_in_dim(vb, blk, axis=2, keepdims=False)
        own_pos = blk * MOBA_BLOCK + offs
        dist_own = q_pos[:, None] - own_pos[None, :]
        logit_own = jnp.einsum('bhqd,bhkd->bhqk', qc, k_own).astype(f32) * scale
        bias_own = bias_table[:, t5_bucket(jnp.maximum(dist_own, 0))].astype(f32)
        logit_own = jnp.where(dist_own >= 0, logit_own + bias_own, -jnp.inf)
        logits = jnp.concatenate(
            [logit_sel.reshape(B, H, Q_CHUNK, topk * MOBA_BLOCK), logit_own], axis=-1)
        probs = jax.nn.softmax(logits, axis=-1)
        p_sel = probs[..., :topk * MOBA_BLOCK].reshape(B, H, Q_CHUNK, topk, MOBA_BLOCK)
        p_own = probs[..., topk * MOBA_BLOCK:]
        out = (jnp.einsum('bhqnk,bhqnkd->bhqd', p_sel, v_sel.astype(f32))
               + jnp.einsum('bhqk,bhkd->bhqd', p_own, v_own.astype(f32)))
        return out.astype(q.dtype)

    out = lax.map(chunk_fn, (q_chunks, jnp.arange(n_chunks)))
    out = out.transpose(1, 0, 3, 2, 4).reshape(B, s_pad, H * Dh)
    return out[:, :S]


def s5_ssm(u, a_re, a_im, log_dt, b_re, b_im, c_re, c_im, d_skip):
    B, S, W = u.shape
    f32 = jnp.float32
    ug = u.reshape(B, S, SSM_GROUPS, SSM_GROUP).astype(f32)
    ar = a_re.astype(f32)
    ai = a_im.astype(f32)
    dt = jnp.exp(log_dt.astype(f32))[:, None]
    mag = jnp.exp(dt * ar)
    ang = dt * ai
    abar_re = mag * jnp.cos(ang)
    abar_im = mag * jnp.sin(ang)
    den = ar * ar + ai * ai
    nr = abar_re - 1.0
    ni = abar_im
    fr = (nr * ar + ni * ai) / den
    fi = (ni * ar - nr * ai) / den
    br = b_re.astype(f32)
    bi = b_im.astype(f32)
    bbar_re = fr[..., None] * br - fi[..., None] * bi
    bbar_im = fr[..., None] * bi + fi[..., None] * br
    bu_re = jnp.einsum('bsgc,gpc->bsgp', ug, bbar_re)
    bu_im = jnp.einsum('bsgc,gpc->bsgp', ug, bbar_im)
    a_re_t = jnp.broadcast_to(abar_re, bu_re.shape)
    a_im_t = jnp.broadcast_to(abar_im, bu_re.shape)

    def combine(lhs, rhs):
        a1r, a1i, b1r, b1i = lhs
        a2r, a2i, b2r, b2i = rhs
        return (a2r * a1r - a2i * a1i,
                a2r * a1i + a2i * a1r,
                a2r * b1r - a2i * b1i + b2r,
                a2r * b1i + a2i * b1r + b2i)

    _, _, h_re, h_im = lax.associative_scan(combine, (a_re_t, a_im_t, bu_re, bu_im), axis=1)
    y = (jnp.einsum('bsgp,gcp->bsgc', h_re, c_re.astype(f32))
         - jnp.einsum('bsgp,gcp->bsgc', h_im, c_im.astype(f32)))
    y = y.reshape(B, S, W) + d_skip.astype(f32) * u.astype(f32)
    return y.astype(u.dtype)


def layer_norm(x, g, b):
    xf = x.astype(jnp.float32)
    mu = jnp.mean(xf, axis=-1, keepdims=True)
    var = jnp.mean(jnp.square(xf - mu), axis=-1, keepdims=True)
    y = (xf - mu) * lax.rsqrt(var + LN_EPS) * g.astype(jnp.float32) + b.astype(jnp.float32)
    return y.astype(x.dtype)


def setup_inputs(seed: int = 0) -> dict:
    key = jax.random.key(seed)
    ks = jax.random.split(key, 20)
    f32 = jnp.float32

    def nrm(k, shape, scale):
        return scale * jax.random.normal(k, shape, f32)

    n = jnp.arange(SSM_STATE, dtype=f32)
    return {
        'x': nrm(ks[0], (BATCH, SEQ, D_MODEL), 1.0),
        'p': nrm(ks[1], (DEPTH, BATCH, SEQ, PLE_DIM), 1.0),
        'w_in': nrm(ks[2], (DEPTH, D_MODEL, IN_WIDTH), D_MODEL ** -0.5),
        'w_attn_proj': nrm(ks[3], (DEPTH, ATTN_WIDTH, D_MODEL), ATTN_WIDTH ** -0.5),
        'w_ssm_proj': nrm(ks[4], (DEPTH, SSM_WIDTH, D_MODEL), SSM_WIDTH ** -0.5),
        'w_out': nrm(ks[5], (DEPTH, D_MODEL, D_MODEL), DEEPNORM_BETA * D_MODEL ** -0.5),
        'ssm_a_re': -0.5 * jnp.exp(nrm(ks[6], (DEPTH, SSM_GROUPS, SSM_STATE), 0.05)),
        'ssm_a_im': math.pi * n + nrm(ks[7], (DEPTH, SSM_GROUPS, SSM_STATE), 0.01),
        'ssm_log_dt': jax.random.uniform(ks[8], (DEPTH, SSM_GROUPS), f32,
                                         math.log(DT_MIN), math.log(DT_MAX)),
        'ssm_b_re': nrm(ks[9], (DEPTH, SSM_GROUPS, SSM_STATE, SSM_GROUP), (2 * SSM_GROUP) ** -0.5),
        'ssm_b_im': nrm(ks[10], (DEPTH, SSM_GROUPS, SSM_STATE, SSM_GROUP), (2 * SSM_GROUP) ** -0.5),
        'ssm_c_re': nrm(ks[11], (DEPTH, SSM_GROUPS, SSM_GROUP, SSM_STATE), SSM_STATE ** -0.5),
        'ssm_c_im': nrm(ks[12], (DEPTH, SSM_GROUPS, SSM_GROUP, SSM_STATE), SSM_STATE ** -0.5),
        'ssm_d': nrm(ks[13], (DEPTH, SSM_WIDTH), 1.0),
        'w_glu': nrm(ks[14], (DEPTH, SSM_WIDTH, 2 * SSM_WIDTH), SSM_WIDTH ** -0.5),
        'w_ple_gate': nrm(ks[15], (DEPTH, D_MODEL, D_MODEL), D_MODEL ** -0.5),
        'w_ple_proj': nrm(ks[16], (DEPTH, PLE_DIM, D_MODEL), PLE_DIM ** -0.5),
        'ln_g': 1.0 + nrm(ks[17], (DEPTH, D_MODEL), 0.02),
        'ln_b': nrm(ks[18], (DEPTH, D_MODEL), 0.02),
        'rel_bias': nrm(ks[19], (REL_BUCKETS, ATTN_HEADS), 0.5),
    }


def reference(x, p, w_in, w_attn_proj, w_ssm_proj, w_out, ssm_a_re, ssm_a_im, ssm_log_dt,
              ssm_b_re, ssm_b_im, ssm_c_re, ssm_c_im, ssm_d, w_glu, w_ple_gate, w_ple_proj,
              ln_g, ln_b, rel_bias):
    B, S, _ = x.shape

    def heads(t):
        return t.reshape(B, S, ATTN_HEADS, HEAD_DIM).transpose(0, 2, 1, 3)

    for i in range(DEPTH):
        proj = x @ w_in[i]
        q, k, v, z_a, u, z_s, g_a, g_s = jnp.split(proj, IN_SPLITS, axis=-1)
        o_a = moba_attention(heads(q), heads(k), heads(v), rel_bias)
        y_a = (o_a * jax.nn.silu(z_a)) @ w_attn_proj[i]
        y_s = s5_ssm(u, ssm_a_re[i], ssm_a_im[i], ssm_log_dt[i], ssm_b_re[i], ssm_b_im[i],
                     ssm_c_re[i], ssm_c_im[i], ssm_d[i])
        glu_a, glu_b = jnp.split(jax.nn.gelu(y_s, approximate=False) @ w_glu[i], 2, axis=-1)
        y_s = glu_a * jax.nn.sigmoid(glu_b)
        y_s = (y_s * jax.nn.silu(z_s)) @ w_ssm_proj[i]
        mix = (jax.nn.sigmoid(g_a) * y_a + jax.nn.sigmoid(g_s) * y_s) @ w_out[i]
        ple = jax.nn.sigmoid(x @ w_ple_gate[i]) * (p[i] @ w_ple_proj[i])
        x = layer_norm(DEEPNORM_ALPHA * x + mix + ple, ln_g[i], ln_b[i])
    return x
```

```python
import functools
import math

import numpy as np
import jax
import jax.numpy as jnp
from jax import lax
from jax.experimental import pallas as pl
from jax.experimental.pallas import tpu as pltpu

F32 = jnp.float32
BF16 = jnp.bfloat16

LANES = 128
SUBLANES = 8
V7X_VMEM_BYTES = 64 * 1024 * 1024

D_MODEL = 1024
PLE_DIM = 256
HEADS = 8
HEAD_DIM = 64
ATTN_WIDTH = HEADS * HEAD_DIM
HEAD_PAIRS = ATTN_WIDTH // LANES
MOBA_BLOCK = 256
MOBA_TOPK = 3
REL_BUCKETS = 32
REL_MAX_DIST = 128
SSM_WIDTH = 512
SSM_GROUP = 16
SSM_GROUPS = SSM_WIDTH // SSM_GROUP
SSM_STATE = 64
SSM_CHUNK = 16
IN_WIDTH = 4 * ATTN_WIDTH + 2 * SSM_WIDTH + 2 * D_MODEL
DEPTH = 1
DEEPNORM_ALPHA = (2.0 * DEPTH) ** 0.25
LN_EPS = 1e-5
MASK_VALUE = -1e30

COL_Q, COL_K, COL_V, COL_ZA, COL_U, COL_ZS = 0, 1, 2, 3, 4, 5
COL_GA, COL_GS = 3, 4


def _dot(a, b):
    return jnp.dot(a, b, preferred_element_type=F32)


def _dot_nt(a, b):
    return lax.dot_general(a, b, (((1,), (1,)), ((), ())), preferred_element_type=F32)


def _sigmoid(v):
    return 1.0 / (1.0 + jnp.exp(-v))


def _vmem_limit(nbytes):
    return int(min(V7X_VMEM_BYTES - (4 << 20), max(nbytes + (8 << 20), 32 << 20)))


def _proj_kernel(x_ref, w_ref, o_ref, *, col_tile):
    xb = x_ref[...].astype(BF16)
    for n in range(o_ref.shape[1] // col_tile):
        cols = slice(n * col_tile, (n + 1) * col_tile)
        o_ref[:, cols] = _dot(xb, w_ref[:, cols]).astype(BF16)


def _proj(x2d, w_bf16, *, row_tile=512, col_tile=1024):
    m, k = x2d.shape
    n = w_bf16.shape[1]
    vmem = 2 * row_tile * k * 4 + 2 * k * n * 2 + 2 * row_tile * n * 2 + row_tile * k * 2
    return pl.pallas_call(
        functools.partial(_proj_kernel, col_tile=col_tile),
        grid=(m // row_tile,),
        in_specs=[pl.BlockSpec((row_tile, k), lambda i: (i, 0)),
                  pl.BlockSpec((k, n), lambda i: (0, 0))],
        out_specs=pl.BlockSpec((row_tile, n), lambda i: (i, 0)),
        out_shape=jax.ShapeDtypeStruct((m, n), BF16),
        compiler_params=pltpu.CompilerParams(
            dimension_semantics=("parallel",), vmem_limit_bytes=_vmem_limit(vmem)),
        name="proj",
    )(x2d, w_bf16)


def _t5_bucket_thresholds():
    max_exact = REL_BUCKETS // 2
    dist = np.arange(0, 2 * MOBA_BLOCK, dtype=np.int32)
    d = np.maximum(dist, 1).astype(np.float32)
    large = max_exact + (np.log(d / np.float32(max_exact)) / np.float32(math.log(REL_MAX_DIST / max_exact))
                         * np.float32(REL_BUCKETS - max_exact)).astype(np.int32)
    large = np.minimum(large, REL_BUCKETS - 1)
    bucket = np.where(dist < max_exact, dist, large)
    assert np.all(np.diff(bucket) >= 0) and bucket[-1] == REL_BUCKETS - 1
    return [int(np.argmax(bucket >= k)) for k in range(1, REL_BUCKETS)]


_BUCKET_THRESHOLDS = _t5_bucket_thresholds()


def _attn_kernel(relb_ref, q_ref, k_ref, v_ref, o_ref,
                 kaug, vaug, kmean, kmean_hi, kmean_lo, bias, qaug, acc, mrow):
    hp = pl.program_id(1)
    i = pl.program_id(2)
    blk = MOBA_BLOCK
    n_blocks = k_ref.shape[0] // blk
    lane = lax.broadcasted_iota(jnp.int32, (blk, LANES), 1)

    def head_lanes(h):
        return (lane >= HEAD_DIM * h) & (lane < HEAD_DIM * (h + 1))

    def other_off(h):
        return HEAD_DIM * (1 - h)

    @pl.when(i == 0)
    def _prepare():
        kmean[...] = jnp.zeros(kmean.shape, F32)
        for h in range(2):
            head = head_lanes(h)
            off = other_off(h)

            def build(j, carry, h=h, head=head, off=off):
                rows = pl.ds(pl.multiple_of(j * blk, blk), blk)
                kb = k_ref[rows, :]
                vb = v_ref[rows, :]
                onehot = jnp.where(lane - off == j, 1.0, 0.0).astype(BF16)
                kaug[h, rows, :] = jnp.where(head, kb, onehot)
                vaug[h, rows, :] = jnp.where(head, vb, jnp.ones_like(vb))
                mean = jnp.sum(kb.astype(F32), axis=0, keepdims=True) * (1.0 / blk)
                kmean[h, pl.ds(off + j, 1), :] = jnp.where(head[:1], mean, 0.0)
                return carry

            lax.fori_loop(0, n_blocks, build, 0)
            km = kmean[h]
            hi = km.astype(BF16)
            kmean_hi[h] = hi
            kmean_lo[h] = (km - hi.astype(F32)).astype(BF16)

            head_id = 2 * hp + h
            far = relb_ref[REL_BUCKETS - 1, head_id]
            row = lax.broadcasted_iota(jnp.int32, (blk, blk), 0)
            col = lax.broadcasted_iota(jnp.int32, (blk, blk), 1)
            for t, base in enumerate((0, blk)):
                dist = base + row - col
                val = jnp.full((blk, blk), relb_ref[0, head_id] - far, F32)
                for kk, thr in enumerate(_BUCKET_THRESHOLDS, start=1):
                    val = jnp.where(dist >= thr, relb_ref[kk, head_id] - far, val)
                bias[h, t] = jnp.where(dist >= 0, val, MASK_VALUE)

    q = q_ref[...]
    for h in range(2):
        head = head_lanes(h)
        off = other_off(h)
        qs = jnp.where(head, q * 0.125, jnp.zeros_like(q))
        gate = _dot_nt(qs, kmean_hi[h]) + _dot_nt(qs, kmean_lo[h])
        block_id = lane - off
        past = (block_id >= 0) & (block_id < i)
        gate = jnp.where(past, gate, -jnp.inf)
        chosen = jnp.zeros(gate.shape, jnp.bool_)
        for _ in range(MOBA_TOPK):
            top = jnp.max(gate, axis=1, keepdims=True)
            first = jnp.min(jnp.where(gate == top, lane, LANES), axis=1, keepdims=True)
            hit = lane == first
            chosen = chosen | (hit & (top > -jnp.inf))
            gate = jnp.where(hit, -jnp.inf, gate)
        mask = jnp.where(past & jnp.logical_not(chosen), MASK_VALUE, 0.0).astype(BF16)
        qaug[h] = jnp.where(head, qs, mask)

    def attend(h, j, bias_tile, first):
        rows = pl.ds(pl.multiple_of(j * blk, blk), blk)
        s = _dot_nt(qaug[h], kaug[h, rows, :])
        if bias_tile is not None:
            s = s + bias[h, bias_tile]
        smax = jnp.max(s, axis=1, keepdims=True)
        if first:
            m_new = smax
            p = jnp.exp(s - m_new)
            acc[h] = _dot(p.astype(BF16), vaug[h, rows, :])
        else:
            m_old = mrow[h]
            m_new = jnp.maximum(m_old, smax)
            p = jnp.exp(s - m_new)
            acc[h] = acc[h] * jnp.exp(m_old - m_new) + _dot(p.astype(BF16), vaug[h, rows, :])
        mrow[h] = m_new

    for h in range(2):
        attend(h, i, 0, True)

    @pl.when(i >= 1)
    def _adjacent():
        for h in range(2):
            attend(h, i - 1, 1, False)

    def far_block(j, carry):
        for h in range(2):
            attend(h, j, None, False)
        return carry

    lax.fori_loop(0, jnp.maximum(i - 1, 0), far_block, 0)

    a0 = acc[0]
    a1 = acc[1]
    l0 = pltpu.roll(a0, HEAD_DIM, axis=1)
    l1 = pltpu.roll(a1, HEAD_DIM, axis=1)
    o_ref[...] = jnp.where(head_lanes(0), a0 / l0, a1 / l1).astype(o_ref.dtype)


def _attention(proj3d, rel_bias):
    b, s, _ = proj3d.shape
    n_blocks = s // MOBA_BLOCK
    assert s % MOBA_BLOCK == 0 and n_blocks <= HEAD_DIM // 2
    blk = MOBA_BLOCK
    scratch = [
        pltpu.VMEM((2, s, LANES), BF16),
        pltpu.VMEM((2, s, LANES), BF16),
        pltpu.VMEM((2, LANES, LANES), F32),
        pltpu.VMEM((2, LANES, LANES), BF16),
        pltpu.VMEM((2, LANES, LANES), BF16),
        pltpu.VMEM((2, 2, blk, blk), F32),
        pltpu.VMEM((2, blk, LANES), BF16),
        pltpu.VMEM((2, blk, LANES), F32),
        pltpu.VMEM((2, blk, 1), F32),
    ]
    vmem = 2 * 2 * s * LANES * 2 + 2 * 2 * s * LANES * 2 + 4 * blk * blk * 4 + (4 << 20)
    return pl.pallas_call(
        _attn_kernel,
        grid=(b, HEAD_PAIRS, n_blocks),
        in_specs=[
            pl.BlockSpec(memory_space=pltpu.SMEM),
            pl.BlockSpec((None, blk, LANES), lambda bi, hp, i: (bi, i, hp)),
            pl.BlockSpec((None, s, LANES), lambda bi, hp, i: (bi, 0, HEAD_PAIRS + hp)),
            pl.BlockSpec((None, s, LANES), lambda bi, hp, i: (bi, 0, 2 * HEAD_PAIRS + hp)),
        ],
        out_specs=pl.BlockSpec((None, blk, LANES), lambda bi, hp, i: (bi, i, hp)),
        out_shape=jax.ShapeDtypeStruct((b, s, ATTN_WIDTH), BF16),
        scratch_shapes=scratch,
        compiler_params=pltpu.CompilerParams(
            dimension_semantics=("parallel", "parallel", "arbitrary"),
            vmem_limit_bytes=_vmem_limit(vmem)),
        name="moba_attn",
    )(rel_bias, proj3d, proj3d, proj3d)


def _ssm_weights(a_re, a_im, log_dt, b_re, b_im, c_re, c_im, d_skip):
    g, p_states, c = b_re.shape
    big_l = SSM_CHUNK
    dt = jnp.exp(log_dt.astype(F32))[:, None]
    ar = a_re.astype(F32)
    ai = a_im.astype(F32)
    mag = jnp.exp(dt * ar)
    ang = dt * ai
    abar_re = mag * jnp.cos(ang)
    abar_im = mag * jnp.sin(ang)
    den = ar * ar + ai * ai
    nr = abar_re - 1.0
    ni = abar_im
    fr = (nr * ar + ni * ai) / den
    fi = (ni * ar - nr * ai) / den
    br = b_re.astype(F32)
    bi = b_im.astype(F32)
    bbar_re = fr[..., None] * br - fi[..., None] * bi
    bbar_im = fr[..., None] * bi + fi[..., None] * br
    n = jnp.arange(big_l + 1, dtype=F32)[:, None, None]
    pmag = jnp.exp(n * (dt * ar)[None])
    pw_re = pmag * jnp.cos(n * ang[None])
    pw_im = pmag * jnp.sin(n * ang[None])
    cr = c_re.astype(F32)
    ci = c_im.astype(F32)
    ab_re = pw_re[..., None] * bbar_re[None] - pw_im[..., None] * bbar_im[None]
    ab_im = pw_re[..., None] * bbar_im[None] + pw_im[..., None] * bbar_re[None]
    hi = lax.Precision.HIGHEST
    kern = (jnp.einsum('gdp,ngpc->ngcd', cr, ab_re[:big_l], precision=hi)
            - jnp.einsum('gdp,ngpc->ngcd', ci, ab_im[:big_l], precision=hi))
    lag = jnp.arange(big_l)[None, :] - jnp.arange(big_l)[:, None]
    toep = kern[jnp.clip(lag, 0, big_l - 1)]
    toep = jnp.where((lag >= 0)[:, :, None, None, None], toep, 0.0)
    toep = toep.transpose(2, 0, 3, 1, 4).reshape(g, big_l * c, big_l * c)
    skip = jnp.tile(d_skip.astype(F32).reshape(g, 1, c), (1, big_l, 1)).reshape(g, big_l * c)
    toep = toep + skip[:, :, None] * jnp.eye(big_l * c, dtype=F32)[None]
    e_re = ab_re[:big_l][::-1].transpose(1, 0, 3, 2).reshape(g, big_l * c, p_states)
    e_im = ab_im[:big_l][::-1].transpose(1, 0, 3, 2).reshape(g, big_l * c, p_states)
    w_in = jnp.concatenate([e_re, e_im, e_im, e_re], axis=-1)
    ca_re = cr[None] * pw_re[1:, :, None, :] - ci[None] * pw_im[1:, :, None, :]
    ca_im = cr[None] * pw_im[1:, :, None, :] + ci[None] * pw_re[1:, :, None, :]
    o_re = ca_re.transpose(1, 3, 0, 2).reshape(g, p_states, big_l * c)
    o_im = -ca_im.transpose(1, 3, 0, 2).reshape(g, p_states, big_l * c)
    w_out = jnp.concatenate([o_re, o_im], axis=1)
    are, aim = pw_re[big_l], pw_im[big_l]
    carry = jnp.stack([jnp.concatenate([are, are], -1),
                       jnp.concatenate([-aim, aim], -1),
                       jnp.concatenate([aim, -aim], -1)], axis=1)
    return toep.astype(BF16), w_in.astype(BF16), w_out.astype(BF16), carry


def _ssm_kernel(u_ref, toep_ref, win_ref, wout_ref, carry_ref, y_ref, e_sc, prev_sc, *, batch):
    n_chunks = u_ref.shape[0] // batch
    half = LANES
    u = u_ref[...]
    e_sc[...] = _dot(u, win_ref[...])
    a1 = jnp.broadcast_to(carry_ref[0:1, :], (batch, half))
    a2 = jnp.broadcast_to(carry_ref[1:2, :], (batch, half))
    a3 = jnp.broadcast_to(carry_ref[2:3, :], (batch, half))

    def step(kk, state):
        st, st_swapped = state
        rows = pl.ds(pl.multiple_of(kk * batch, batch), batch)
        prev_sc[rows, :] = st
        e = e_sc[rows, :]
        new = a1 * st + a2 * st_swapped + e[:, :half]
        new_swapped = a1 * st_swapped + a3 * st + e[:, half:]
        return new, new_swapped

    zero = jnp.zeros((batch, half), F32)
    lax.fori_loop(0, n_chunks, step, (zero, zero), unroll=8)
    y = _dot(u, toep_ref[...]) + _dot(prev_sc[...].astype(BF16), wout_ref[...])
    y_ref[...] = y.astype(y_ref.dtype)


def _ssm(u_t, toep, w_in, w_out, carry, *, batch):
    g, rows, width = u_t.shape
    vmem = 2 * 2 * rows * width * 2 + rows * width * 4 + rows * LANES * 4 + 3 * rows * width * 4
    return pl.pallas_call(
        functools.partial(_ssm_kernel, batch=batch),
        grid=(g,),
        in_specs=[
            pl.BlockSpec((None, rows, width), lambda gi: (gi, 0, 0)),
            pl.BlockSpec((None, width, width), lambda gi: (gi, 0, 0)),
            pl.BlockSpec((None, width, width), lambda gi: (gi, 0, 0)),
            pl.BlockSpec((None, LANES, width), lambda gi: (gi, 0, 0)),
            pl.BlockSpec((None, 3, LANES), lambda gi: (gi, 0, 0)),
        ],
        out_specs=pl.BlockSpec((None, rows, width), lambda gi: (gi, 0, 0)),
        out_shape=jax.ShapeDtypeStruct((g, rows, width), BF16),
        scratch_shapes=[pltpu.VMEM((rows, width), F32), pltpu.VMEM((rows, LANES), F32)],
        compiler_params=pltpu.CompilerParams(
            dimension_semantics=("parallel",), vmem_limit_bytes=_vmem_limit(vmem)),
        name="s5_ssm",
    )(u_t, toep, w_in, w_out, carry)


def _final_kernel(x_ref, p_ref, oa_ref, ys_ref, za_ref, zs_ref, ga_ref, gs_ref,
                  wap_ref, wglu_ref, wsp_ref, wout_ref, wpg_ref, wpp_ref, lng_ref, lnb_ref, o_ref):
    x = x_ref[...]
    za = za_ref[...].astype(F32)
    a_in = oa_ref[...].astype(F32) * (za * _sigmoid(za))
    y_a = _dot(a_in.astype(BF16), wap_ref[...])
    ys = ys_ref[...].astype(F32)
    gelu = 0.5 * ys * (1.0 + lax.erf(ys * (2.0 ** -0.5)))
    glu = _dot(gelu.astype(BF16), wglu_ref[...])
    zs = zs_ref[...].astype(F32)
    s_in = glu[:, :SSM_WIDTH] * _sigmoid(glu[:, SSM_WIDTH:]) * (zs * _sigmoid(zs))
    y_s = _dot(s_in.astype(BF16), wsp_ref[...])
    merge = _sigmoid(ga_ref[...].astype(F32)) * y_a + _sigmoid(gs_ref[...].astype(F32)) * y_s
    mix = _dot(merge.astype(BF16), wout_ref[...])
    ple = _sigmoid(_dot(x.astype(BF16), wpg_ref[...])) * _dot(p_ref[...].astype(BF16), wpp_ref[...])
    hsum = DEEPNORM_ALPHA * x + mix + ple
    mu = jnp.mean(hsum, axis=-1, keepdims=True)
    cen = hsum - mu
    var = jnp.mean(cen * cen, axis=-1, keepdims=True)
    o_ref[...] = cen * lax.rsqrt(var + LN_EPS) * lng_ref[...] + lnb_ref[...]


def _final(x2d, p2d, proj, o_a, y_s, w_ap, w_glu, w_sp, w_out, w_pg, w_pp, ln_g, ln_b, *, row_tile=256):
    m = x2d.shape[0]
    half, full = SSM_WIDTH, D_MODEL

    def rows(width, col):
        return pl.BlockSpec((row_tile, width), lambda i: (i, col))

    def whole(arr):
        return pl.BlockSpec(arr.shape, lambda i: (0, 0))

    weights = (w_ap, w_glu, w_sp, w_out, w_pg, w_pp, ln_g, ln_b)
    vmem = (2 * sum(int(np.prod(w.shape)) * w.dtype.itemsize for w in weights)
            + 2 * row_tile * (2 * full * 4 + PLE_DIM * 4 + (4 * half + 2 * full) * 2)
            + 12 * row_tile * full * 4)
    return pl.pallas_call(
        _final_kernel,
        grid=(m // row_tile,),
        in_specs=[rows(full, 0), rows(PLE_DIM, 0), rows(half, 0), rows(half, 0),
                  rows(half, COL_ZA), rows(half, COL_ZS), rows(full, COL_GA), rows(full, COL_GS)]
                 + [whole(w) for w in weights],
        out_specs=rows(full, 0),
        out_shape=jax.ShapeDtypeStruct((m, full), F32),
        compiler_params=pltpu.CompilerParams(
            dimension_semantics=("parallel",), vmem_limit_bytes=_vmem_limit(vmem)),
        name="final",
    )(x2d, p2d, o_a, y_s, proj, proj, proj, proj, *weights)


def kernel(x, p, w_in, w_attn_proj, w_ssm_proj, w_out, ssm_a_re, ssm_a_im, ssm_log_dt, ssm_b_re, ssm_b_im, ssm_c_re, ssm_c_im, ssm_d, w_glu, w_ple_gate, w_ple_proj, ln_g, ln_b, rel_bias):
    b, s, d = x.shape
    m = b * s
    n_chunks = s // SSM_CHUNK
    for i in range(w_in.shape[0]):
        x2d = x.reshape(m, d)
        proj = _proj(x2d, w_in[i].astype(BF16))
        o_a = _attention(proj.reshape(b, s, IN_WIDTH), rel_bias.astype(F32))
        u = proj[:, COL_U * SSM_WIDTH:(COL_U + 1) * SSM_WIDTH]
        u_t = (u.reshape(b, n_chunks, SSM_CHUNK, SSM_GROUPS, SSM_GROUP)
               .transpose(3, 1, 0, 2, 4).reshape(SSM_GROUPS, n_chunks * b, SSM_CHUNK * SSM_GROUP))
        toep, s_in, s_out, carry = _ssm_weights(
            ssm_a_re[i], ssm_a_im[i], ssm_log_dt[i], ssm_b_re[i], ssm_b_im[i],
            ssm_c_re[i], ssm_c_im[i], ssm_d[i].reshape(SSM_GROUPS, SSM_GROUP))
        y_t = _ssm(u_t, toep, s_in, s_out, carry, batch=b)
        y_s = (y_t.reshape(SSM_GROUPS, n_chunks, b, SSM_CHUNK, SSM_GROUP)
               .transpose(2, 1, 3, 0, 4).reshape(m, SSM_WIDTH))
        x2d = _final(x2d, p[i].reshape(m, PLE_DIM), proj, o_a.reshape(m, ATTN_WIDTH), y_s,
                     w_attn_proj[i].astype(BF16), w_glu[i].astype(BF16), w_ssm_proj[i].astype(BF16),
                     w_out[i].astype(BF16), w_ple_gate[i].astype(BF16), w_ple_proj[i].astype(BF16),
                     ln_g[i].astype(F32).reshape(1, d), ln_b[i].astype(F32).reshape(1, d))
        x = x2d.reshape(b, s, d)
    return x
```

```python
import functools
import math

import numpy as np
import jax
import jax.numpy as jnp
from jax import lax
from jax.experimental import pallas as pl
from jax.experimental.pallas import tpu as pltpu

F32 = jnp.float32
BF16 = jnp.bfloat16

LANES = 128
SUBLANES = 8
V7X_VMEM_BYTES = 64 * 1024 * 1024

D_MODEL = 1024
PLE_DIM = 256
HEADS = 8
HEAD_DIM = 64
ATTN_WIDTH = HEADS * HEAD_DIM
HEAD_PAIRS = ATTN_WIDTH // LANES
MOBA_BLOCK = 256
MOBA_TOPK = 3
REL_BUCKETS = 32
REL_MAX_DIST = 128
SSM_WIDTH = 512
SSM_GROUP = 16
SSM_GROUPS = SSM_WIDTH // SSM_GROUP
SSM_STATE = 64
SSM_CHUNK = 16
IN_WIDTH = 4 * ATTN_WIDTH + 2 * SSM_WIDTH + 2 * D_MODEL
DEPTH = 1
DEEPNORM_ALPHA = (2.0 * DEPTH) ** 0.25
LN_EPS = 1e-5
MASK_VALUE = -1e30

PROJ_WIDTH = IN_WIDTH - ATTN_WIDTH
_IN_SPLITS = tuple(int(v) for v in np.cumsum(
    (ATTN_WIDTH,) * 4 + (SSM_WIDTH,) * 2 + (D_MODEL,) * 2)[:-1])
COL_Q, COL_K, COL_ZA, COL_ZS, COL_U = 0, 1, 2, 3, 8
COL_GA, COL_GS = 2, 3


def _dot(a, b):
    return jnp.dot(a, b, preferred_element_type=F32)


def _dot_nt(a, b):
    return lax.dot_general(a, b, (((1,), (1,)), ((), ())), preferred_element_type=F32)


def _sigmoid(v):
    return 1.0 / (1.0 + jnp.exp(-v))


def _vmem_limit(nbytes):
    return int(min(V7X_VMEM_BYTES - (4 << 20), max(nbytes + (8 << 20), 32 << 20)))


def _proj_kernel(x_ref, w_ref, wvt_ref, o_ref, vt_ref, *, col_tile):
    xb = x_ref[...].astype(BF16)
    n_cols = o_ref.shape[1]
    for start in range(0, n_cols, col_tile):
        cols = slice(start, min(start + col_tile, n_cols))
        o_ref[:, cols] = _dot(xb, w_ref[:, cols]).astype(BF16)
    vt_ref[...] = _dot_nt(wvt_ref[...], xb).astype(BF16)


def _proj(x2d, w_bf16, wvt_bf16, *, batch, row_tile=512, col_tile=1024):
    m, k = x2d.shape
    n = w_bf16.shape[1]
    nv = wvt_bf16.shape[0]
    tiles_per_batch = m // batch // row_tile
    vmem = (2 * row_tile * k * 4 + 2 * k * (n + nv) * 2 + 2 * row_tile * (n + nv) * 2
            + row_tile * k * 2 + 2 * row_tile * col_tile * 4)
    return pl.pallas_call(
        functools.partial(_proj_kernel, col_tile=col_tile),
        grid=(m // row_tile,),
        in_specs=[pl.BlockSpec((row_tile, k), lambda i: (i, 0)),
                  pl.BlockSpec((k, n), lambda i: (0, 0)),
                  pl.BlockSpec((nv, k), lambda i: (0, 0))],
        out_specs=[pl.BlockSpec((row_tile, n), lambda i: (i, 0)),
                   pl.BlockSpec((None, nv, row_tile),
                                lambda i: (i // tiles_per_batch, 0, i % tiles_per_batch))],
        out_shape=[jax.ShapeDtypeStruct((m, n), BF16),
                   jax.ShapeDtypeStruct((batch, nv, m // batch), BF16)],
        compiler_params=pltpu.CompilerParams(
            dimension_semantics=("parallel",), vmem_limit_bytes=_vmem_limit(vmem)),
        name="proj",
    )(x2d, w_bf16, wvt_bf16)


def _t5_bucket_thresholds():
    max_exact = REL_BUCKETS // 2
    dist = np.arange(0, 2 * MOBA_BLOCK, dtype=np.int32)
    d = np.maximum(dist, 1).astype(np.float32)
    large = max_exact + (np.log(d / np.float32(max_exact)) / np.float32(math.log(REL_MAX_DIST / max_exact))
                         * np.float32(REL_BUCKETS - max_exact)).astype(np.int32)
    large = np.minimum(large, REL_BUCKETS - 1)
    bucket = np.where(dist < max_exact, dist, large)
    assert np.all(np.diff(bucket) >= 0) and bucket[-1] == REL_BUCKETS - 1
    return [int(np.argmax(bucket >= k)) for k in range(1, REL_BUCKETS)]


_BUCKET_THRESHOLDS = _t5_bucket_thresholds()


def _attn_kernel(relb_ref, q_ref, k_ref, vt_ref, o_ref,
                 kaug, vaug_t, kmean, kmean_hi, kmean_lo, bias_t, qaug, acc_t, mcol, logits_a, logits_b):
    hp = pl.program_id(1)
    i = pl.program_id(2)
    blk = MOBA_BLOCK
    n_blocks = k_ref.shape[0] // blk
    lane = lax.broadcasted_iota(jnp.int32, (blk, LANES), 1)

    def head_lanes(h):
        return (lane >= HEAD_DIM * h) & (lane < HEAD_DIM * (h + 1))

    def other_off(h):
        return HEAD_DIM * (1 - h)

    @pl.when(i == 0)
    def _prepare():
        kmean[...] = jnp.zeros(kmean.shape, F32)
        for h in range(2):
            head = head_lanes(h)
            off = other_off(h)

            def build(j, carry, h=h, head=head, off=off):
                rows = pl.ds(pl.multiple_of(j * blk, blk), blk)
                kb = k_ref[rows, :]
                onehot = jnp.where(lane - off == j, 1.0, 0.0).astype(BF16)
                kaug[h, rows, :] = jnp.where(head, kb, onehot)
                vtb = vt_ref[:, rows]
                dim = lax.broadcasted_iota(jnp.int32, vtb.shape, 0)
                head_dims = (dim >= HEAD_DIM * h) & (dim < HEAD_DIM * (h + 1))
                vaug_t[h, :, rows] = jnp.where(head_dims, vtb, jnp.ones_like(vtb))
                mean = jnp.sum(kb.astype(F32), axis=0, keepdims=True) * (1.0 / blk)
                kmean[h, pl.ds(off + j, 1), :] = jnp.where(head[:1], mean, 0.0)
                return carry

            lax.fori_loop(0, n_blocks, build, 0)
            km = kmean[h]
            hi = km.astype(BF16)
            kmean_hi[h] = hi
            kmean_lo[h] = (km - hi.astype(F32)).astype(BF16)

            head_id = 2 * hp + h
            far = relb_ref[REL_BUCKETS - 1, head_id]
            key = lax.broadcasted_iota(jnp.int32, (blk, blk), 0)
            qry = lax.broadcasted_iota(jnp.int32, (blk, blk), 1)
            for t, base in enumerate((0, blk)):
                dist = base + qry - key
                val = jnp.full((blk, blk), relb_ref[0, head_id] - far, F32)
                for kk, thr in enumerate(_BUCKET_THRESHOLDS, start=1):
                    val = jnp.where(dist >= thr, relb_ref[kk, head_id] - far, val)
                bias_t[h, t] = jnp.where(dist >= 0, val, MASK_VALUE)

    q = q_ref[...]
    for h in range(2):
        head = head_lanes(h)
        off = other_off(h)
        qs = jnp.where(head, q * 0.125, jnp.zeros_like(q))
        gate = _dot_nt(qs, kmean_hi[h]) + _dot_nt(qs, kmean_lo[h])
        block_id = lane - off
        past = (block_id >= 0) & (block_id < i)
        gate = jnp.where(past, gate, -jnp.inf)
        chosen = jnp.zeros(gate.shape, jnp.bool_)
        for _ in range(MOBA_TOPK):
            top = jnp.max(gate, axis=1, keepdims=True)
            first = jnp.min(jnp.where(gate == top, lane, LANES), axis=1, keepdims=True)
            hit = lane == first
            chosen = chosen | (hit & (top > -jnp.inf))
            gate = jnp.where(hit, -jnp.inf, gate)
        mask = jnp.where(past & jnp.logical_not(chosen), MASK_VALUE, 0.0).astype(BF16)
        qaug[h] = jnp.where(head, qs, mask)

    n_steps = i + 1
    slots = (logits_a, logits_b)

    def block_rows(t):
        j = jnp.maximum(i - t, 0)
        return pl.ds(pl.multiple_of(j * blk, blk), blk)

    def compute_logits(t, slot, bias_tile):
        for h in range(2):
            s = _dot_nt(kaug[h, block_rows(t), :], qaug[h])
            if bias_tile is not None:
                s = s + bias_t[h, bias_tile]
            slots[slot][h] = s

    def softmax_update(t, slot):
        for h in range(2):
            s = slots[slot][h]
            m_old = mcol[h]
            m_new = jnp.maximum(m_old, jnp.max(s, axis=0, keepdims=True))
            p = jnp.exp(s - m_new)
            acc_t[h] = (acc_t[h] * jnp.exp(m_old - m_new)
                        + _dot(vaug_t[h, :, block_rows(t)], p.astype(BF16)))
            mcol[h] = m_new

    def step_pair(t, adjacent_bias):
        compute_logits(t + 1, 1, adjacent_bias)
        softmax_update(t, 0)
        compute_logits(t + 2, 0, None)
        softmax_update(t + 1, 1)

    mcol[...] = jnp.full(mcol.shape, -jnp.inf, F32)
    acc_t[...] = jnp.zeros(acc_t.shape, F32)
    compute_logits(0, 0, 0)

    @pl.when(i >= 1)
    def _own_and_adjacent():
        step_pair(0, 1)

    def far_pair(pair, carry):
        step_pair(2 * pair, None)
        return carry

    lax.fori_loop(1, n_steps // 2, far_pair, 0)

    @pl.when(n_steps % 2 == 1)
    def _last_step():
        softmax_update(n_steps - 1, 0)

    a0 = acc_t[0]
    a1 = acc_t[1]
    o_t = jnp.concatenate([a0[:HEAD_DIM] / a0[HEAD_DIM:HEAD_DIM + 1],
                           a1[HEAD_DIM:] / a1[0:1]], axis=0)
    o_ref[...] = o_t.T.astype(o_ref.dtype)


def _attention(proj3d, v_t, rel_bias):
    b, s, _ = proj3d.shape
    n_blocks = s // MOBA_BLOCK
    assert s % MOBA_BLOCK == 0 and n_blocks <= HEAD_DIM // 2
    blk = MOBA_BLOCK
    scratch = [
        pltpu.VMEM((2, s, LANES), BF16),
        pltpu.VMEM((2, LANES, s), BF16),
        pltpu.VMEM((2, LANES, LANES), F32),
        pltpu.VMEM((2, LANES, LANES), BF16),
        pltpu.VMEM((2, LANES, LANES), BF16),
        pltpu.VMEM((2, 2, blk, blk), F32),
        pltpu.VMEM((2, blk, LANES), BF16),
        pltpu.VMEM((2, LANES, blk), F32),
        pltpu.VMEM((2, 1, blk), F32),
        pltpu.VMEM((2, blk, blk), F32),
        pltpu.VMEM((2, blk, blk), F32),
    ]
    vmem = 2 * 2 * s * LANES * 2 + 2 * 2 * s * LANES * 2 + 4 * blk * blk * 4 + (4 << 20)
    return pl.pallas_call(
        _attn_kernel,
        grid=(b, HEAD_PAIRS, n_blocks),
        in_specs=[
            pl.BlockSpec(memory_space=pltpu.SMEM),
            pl.BlockSpec((None, blk, LANES), lambda bi, hp, i: (bi, i, hp)),
            pl.BlockSpec((None, s, LANES), lambda bi, hp, i: (bi, 0, HEAD_PAIRS + hp)),
            pl.BlockSpec((None, LANES, s), lambda bi, hp, i: (bi, hp, 0)),
        ],
        out_specs=pl.BlockSpec((None, blk, LANES), lambda bi, hp, i: (bi, i, hp)),
        out_shape=jax.ShapeDtypeStruct((b, s, ATTN_WIDTH), BF16),
        scratch_shapes=scratch,
        compiler_params=pltpu.CompilerParams(
            dimension_semantics=("parallel", "parallel", "arbitrary"),
            vmem_limit_bytes=_vmem_limit(vmem)),
        name="moba_attn",
    )(rel_bias, proj3d, proj3d, v_t)


def _ssm_weights(a_re, a_im, log_dt, b_re, b_im, c_re, c_im, d_skip):
    g, p_states, c = b_re.shape
    big_l = SSM_CHUNK
    dt = jnp.exp(log_dt.astype(F32))[:, None]
    ar = a_re.astype(F32)
    ai = a_im.astype(F32)
    mag = jnp.exp(dt * ar)
    ang = dt * ai
    abar_re = mag * jnp.cos(ang)
    abar_im = mag * jnp.sin(ang)
    den = ar * ar + ai * ai
    nr = abar_re - 1.0
    ni = abar_im
    fr = (nr * ar + ni * ai) / den
    fi = (ni * ar - nr * ai) / den
    br = b_re.astype(F32)
    bi = b_im.astype(F32)
    bbar_re = fr[..., None] * br - fi[..., None] * bi
    bbar_im = fr[..., None] * bi + fi[..., None] * br
    n = jnp.arange(big_l + 1, dtype=F32)[:, None, None]
    pmag = jnp.exp(n * (dt * ar)[None])
    pw_re = pmag * jnp.cos(n * ang[None])
    pw_im = pmag * jnp.sin(n * ang[None])
    cr = c_re.astype(F32)
    ci = c_im.astype(F32)
    ab_re = pw_re[..., None] * bbar_re[None] - pw_im[..., None] * bbar_im[None]
    ab_im = pw_re[..., None] * bbar_im[None] + pw_im[..., None] * bbar_re[None]
    hi = lax.Precision.HIGHEST
    kern = (jnp.einsum('gdp,ngpc->ngcd', cr, ab_re[:big_l], precision=hi)
            - jnp.einsum('gdp,ngpc->ngcd', ci, ab_im[:big_l], precision=hi))
    lag = jnp.arange(big_l)[None, :] - jnp.arange(big_l)[:, None]
    toep = kern[jnp.clip(lag, 0, big_l - 1)]
    toep = jnp.where((lag >= 0)[:, :, None, None, None], toep, 0.0)
    toep = toep.transpose(2, 0, 3, 1, 4).reshape(g, big_l * c, big_l * c)
    skip = jnp.tile(d_skip.astype(F32).reshape(g, 1, c), (1, big_l, 1)).reshape(g, big_l * c)
    toep = toep + skip[:, :, None] * jnp.eye(big_l * c, dtype=F32)[None]
    e_re = ab_re[:big_l][::-1].transpose(1, 0, 3, 2).reshape(g, big_l * c, p_states)
    e_im = ab_im[:big_l][::-1].transpose(1, 0, 3, 2).reshape(g, big_l * c, p_states)
    w_in = jnp.concatenate([e_re, e_im, e_im, e_re], axis=-1)
    ca_re = cr[None] * pw_re[1:, :, None, :] - ci[None] * pw_im[1:, :, None, :]
    ca_im = cr[None] * pw_im[1:, :, None, :] + ci[None] * pw_re[1:, :, None, :]
    o_re = ca_re.transpose(1, 3, 0, 2).reshape(g, p_states, big_l * c)
    o_im = -ca_im.transpose(1, 3, 0, 2).reshape(g, p_states, big_l * c)
    w_out = jnp.concatenate([o_re, o_im], axis=1)
    are, aim = pw_re[big_l], pw_im[big_l]
    carry = jnp.stack([jnp.concatenate([are, are], -1),
                       jnp.concatenate([-aim, aim], -1),
                       jnp.concatenate([aim, -aim], -1)], axis=1)
    return toep.astype(BF16), w_in.astype(BF16), w_out.astype(BF16), carry


def _ssm_kernel(u_ref, toep_ref, win_ref, wout_ref, carry_ref, y_ref, e_sc, prev_sc, *, batch):
    n_chunks = u_ref.shape[0] // batch
    half = LANES
    u = u_ref[...]
    e_sc[...] = _dot(u, win_ref[...])
    a1 = jnp.broadcast_to(carry_ref[0:1, :], (batch, half))
    a2 = jnp.broadcast_to(carry_ref[1:2, :], (batch, half))
    a3 = jnp.broadcast_to(carry_ref[2:3, :], (batch, half))

    def step(kk, state):
        st, st_swapped = state
        rows = pl.ds(pl.multiple_of(kk * batch, batch), batch)
        prev_sc[rows, :] = st
        e = e_sc[rows, :]
        new = a1 * st + a2 * st_swapped + e[:, :half]
        new_swapped = a1 * st_swapped + a3 * st + e[:, half:]
        return new, new_swapped

    zero = jnp.zeros((batch, half), F32)
    lax.fori_loop(0, n_chunks, step, (zero, zero), unroll=8)
    y = _dot(u, toep_ref[...]) + _dot(prev_sc[...].astype(BF16), wout_ref[...])
    y_ref[...] = y.astype(y_ref.dtype)


def _ssm(u_t, toep, w_in, w_out, carry, *, batch):
    g, rows, width = u_t.shape
    vmem = 2 * 2 * rows * width * 2 + rows * width * 4 + rows * LANES * 4 + 3 * rows * width * 4
    return pl.pallas_call(
        functools.partial(_ssm_kernel, batch=batch),
        grid=(g,),
        in_specs=[
            pl.BlockSpec((None, rows, width), lambda gi: (gi, 0, 0)),
            pl.BlockSpec((None, width, width), lambda gi: (gi, 0, 0)),
            pl.BlockSpec((None, width, width), lambda gi: (gi, 0, 0)),
            pl.BlockSpec((None, LANES, width), lambda gi: (gi, 0, 0)),
            pl.BlockSpec((None, 3, LANES), lambda gi: (gi, 0, 0)),
        ],
        out_specs=pl.BlockSpec((None, rows, width), lambda gi: (gi, 0, 0)),
        out_shape=jax.ShapeDtypeStruct((g, rows, width), BF16),
        scratch_shapes=[pltpu.VMEM((rows, width), F32), pltpu.VMEM((rows, LANES), F32)],
        compiler_params=pltpu.CompilerParams(
            dimension_semantics=("parallel",), vmem_limit_bytes=_vmem_limit(vmem)),
        name="s5_ssm",
    )(u_t, toep, w_in, w_out, carry)


def _final_kernel(x_ref, p_ref, oa_ref, ys_ref, za_ref, zs_ref, ga_ref, gs_ref,
                  wap_ref, wglu_ref, wsp_ref, wout_ref, wpg_ref, wpp_ref, lng_ref, lnb_ref, o_ref):
    x = x_ref[...]
    za = za_ref[...].astype(F32)
    a_in = oa_ref[...].astype(F32) * (za * _sigmoid(za))
    y_a = _dot(a_in.astype(BF16), wap_ref[...])
    ys = ys_ref[...].astype(F32)
    gelu = 0.5 * ys * (1.0 + lax.erf(ys * (2.0 ** -0.5)))
    glu = _dot(gelu.astype(BF16), wglu_ref[...])
    zs = zs_ref[...].astype(F32)
    s_in = glu[:, :SSM_WIDTH] * _sigmoid(glu[:, SSM_WIDTH:]) * (zs * _sigmoid(zs))
    y_s = _dot(s_in.astype(BF16), wsp_ref[...])
    merge = _sigmoid(ga_ref[...].astype(F32)) * y_a + _sigmoid(gs_ref[...].astype(F32)) * y_s
    mix = _dot(merge.astype(BF16), wout_ref[...])
    ple = _sigmoid(_dot(x.astype(BF16), wpg_ref[...])) * _dot(p_ref[...].astype(BF16), wpp_ref[...])
    hsum = DEEPNORM_ALPHA * x + mix + ple
    mu = jnp.mean(hsum, axis=-1, keepdims=True)
    cen = hsum - mu
    var = jnp.mean(cen * cen, axis=-1, keepdims=True)
    o_ref[...] = cen * lax.rsqrt(var + LN_EPS) * lng_ref[...] + lnb_ref[...]


def _final(x2d, p2d, proj, o_a, y_s, w_ap, w_glu, w_sp, w_out, w_pg, w_pp, ln_g, ln_b, *, row_tile=256):
    m = x2d.shape[0]
    half, full = SSM_WIDTH, D_MODEL

    def rows(width, col):
        return pl.BlockSpec((row_tile, width), lambda i: (i, col))

    def whole(arr):
        return pl.BlockSpec(arr.shape, lambda i: (0, 0))

    weights = (w_ap, w_glu, w_sp, w_out, w_pg, w_pp, ln_g, ln_b)
    vmem = (2 * sum(int(np.prod(w.shape)) * w.dtype.itemsize for w in weights)
            + 2 * row_tile * (2 * full * 4 + PLE_DIM * 4 + (4 * half + 2 * full) * 2)
            + 12 * row_tile * full * 4)
    return pl.pallas_call(
        _final_kernel,
        grid=(m // row_tile,),
        in_specs=[rows(full, 0), rows(PLE_DIM, 0), rows(half, 0), rows(half, 0),
                  rows(half, COL_ZA), rows(half, COL_ZS), rows(full, COL_GA), rows(full, COL_GS)]
                 + [whole(w) for w in weights],
        out_specs=rows(full, 0),
        out_shape=jax.ShapeDtypeStruct((m, full), F32),
        compiler_params=pltpu.CompilerParams(
            dimension_semantics=("parallel",), vmem_limit_bytes=_vmem_limit(vmem)),
        name="final",
    )(x2d, p2d, o_a, y_s, proj, proj, proj, proj, *weights)


def kernel(x, p, w_in, w_attn_proj, w_ssm_proj, w_out, ssm_a_re, ssm_a_im, ssm_log_dt, ssm_b_re, ssm_b_im, ssm_c_re, ssm_c_im, ssm_d, w_glu, w_ple_gate, w_ple_proj, ln_g, ln_b, rel_bias):
    b, s, d = x.shape
    m = b * s
    n_chunks = s // SSM_CHUNK
    for i in range(w_in.shape[0]):
        x2d = x.reshape(m, d)
        wq, wk, wv, wza, wu, wzs, wga, wgs = jnp.split(w_in[i].astype(BF16), _IN_SPLITS, axis=1)
        w_main = jnp.concatenate([wq, wk, wza, wzs, wga, wgs, wu], axis=1)
        proj, v_t = _proj(x2d, w_main, wv.T, batch=b)
        o_a = _attention(proj.reshape(b, s, PROJ_WIDTH), v_t, rel_bias.astype(F32))
        u = proj[:, COL_U * SSM_WIDTH:(COL_U + 1) * SSM_WIDTH]
        u_t = (u.reshape(b, n_chunks, SSM_CHUNK, SSM_GROUPS, SSM_GROUP)
               .transpose(3, 1, 0, 2, 4).reshape(SSM_GROUPS, n_chunks * b, SSM_CHUNK * SSM_GROUP))
        toep, s_in, s_out, carry = _ssm_weights(
            ssm_a_re[i], ssm_a_im[i], ssm_log_dt[i], ssm_b_re[i], ssm_b_im[i],
            ssm_c_re[i], ssm_c_im[i], ssm_d[i].reshape(SSM_GROUPS, SSM_GROUP))
        y_t = _ssm(u_t, toep, s_in, s_out, carry, batch=b)
        y_s = (y_t.reshape(SSM_GROUPS, n_chunks, b, SSM_CHUNK, SSM_GROUP)
               .transpose(2, 1, 3, 0, 4).reshape(m, SSM_WIDTH))
        x2d = _final(x2d, p[i].reshape(m, PLE_DIM), proj, o_a.reshape(m, ATTN_WIDTH), y_s,
                     w_attn_proj[i].astype(BF16), w_glu[i].astype(BF16), w_ssm_proj[i].astype(BF16),
                     w_out[i].astype(BF16), w_ple_gate[i].astype(BF16), w_ple_proj[i].astype(BF16),
                     ln_g[i].astype(F32).reshape(1, d), ln_b[i].astype(F32).reshape(1, d))
        x = x2d.reshape(b, s, d)
    return x
```

```python
import functools
import math

import numpy as np
import jax
import jax.numpy as jnp
from jax import lax
from jax.experimental import pallas as pl
from jax.experimental.pallas import tpu as pltpu

F32 = jnp.float32
BF16 = jnp.bfloat16

LANES = 128
SUBLANES = 8
V7X_VMEM_BYTES = 64 * 1024 * 1024

D_MODEL = 1024
PLE_DIM = 256
HEADS = 8
HEAD_DIM = 64
ATTN_WIDTH = HEADS * HEAD_DIM
HEAD_PAIRS = ATTN_WIDTH // LANES
MOBA_BLOCK = 256
MOBA_TOPK = 3
REL_BUCKETS = 32
REL_MAX_DIST = 128
SSM_WIDTH = 512
SSM_GROUP = 16
SSM_GROUPS = SSM_WIDTH // SSM_GROUP
SSM_STATE = 64
SSM_CHUNK = 16
IN_WIDTH = 4 * ATTN_WIDTH + 2 * SSM_WIDTH + 2 * D_MODEL
DEPTH = 1
DEEPNORM_ALPHA = (2.0 * DEPTH) ** 0.25
LN_EPS = 1e-5
MASK_VALUE = -1e30

PROJ_WIDTH = IN_WIDTH - 2 * ATTN_WIDTH
_IN_SPLITS = tuple(int(v) for v in np.cumsum(
    (ATTN_WIDTH,) * 4 + (SSM_WIDTH,) * 2 + (D_MODEL,) * 2)[:-1])
COL_GA, COL_GS = 0, 1
COL_K, COL_ZA, COL_ZS, COL_U = 4, 5, 6, 7


def _dot(a, b):
    return jnp.dot(a, b, preferred_element_type=F32)


def _dot_nt(a, b):
    return lax.dot_general(a, b, (((1,), (1,)), ((), ())), preferred_element_type=F32)


def _sigmoid(v):
    return 1.0 / (1.0 + jnp.exp(-v))


def _vmem_limit(nbytes):
    return int(min(V7X_VMEM_BYTES - (4 << 20), max(nbytes + (8 << 20), 32 << 20)))


def _proj_kernel(x_ref, w_ref, wvt_ref, o_ref, vt_ref, *, col_tile):
    xb = x_ref[...].astype(BF16)
    n_cols = o_ref.shape[1]
    for start in range(0, n_cols, col_tile):
        cols = slice(start, min(start + col_tile, n_cols))
        o_ref[:, cols] = _dot(xb, w_ref[:, cols]).astype(BF16)
    vt_ref[...] = _dot_nt(wvt_ref[...], xb).astype(BF16)


def _proj(x2d, w_bf16, wvt_bf16, *, batch, row_tile=512, col_tile=1024):
    m, k = x2d.shape
    n = w_bf16.shape[1]
    nv = wvt_bf16.shape[0]
    tiles_per_batch = m // batch // row_tile
    vmem = (2 * row_tile * k * 4 + 2 * k * (n + nv) * 2 + 2 * row_tile * (n + nv) * 2
            + row_tile * k * 2 + 2 * row_tile * col_tile * 4)
    return pl.pallas_call(
        functools.partial(_proj_kernel, col_tile=col_tile),
        grid=(m // row_tile,),
        in_specs=[pl.BlockSpec((row_tile, k), lambda i: (i, 0)),
                  pl.BlockSpec((k, n), lambda i: (0, 0)),
                  pl.BlockSpec((nv, k), lambda i: (0, 0))],
        out_specs=[pl.BlockSpec((row_tile, n), lambda i: (i, 0)),
                   pl.BlockSpec((None, nv, row_tile),
                                lambda i: (i // tiles_per_batch, 0, i % tiles_per_batch))],
        out_shape=[jax.ShapeDtypeStruct((m, n), BF16),
                   jax.ShapeDtypeStruct((batch, nv, m // batch), BF16)],
        compiler_params=pltpu.CompilerParams(
            dimension_semantics=("parallel",), vmem_limit_bytes=_vmem_limit(vmem)),
        name="proj",
    )(x2d, w_bf16, wvt_bf16)


def _t5_bucket_thresholds():
    max_exact = REL_BUCKETS // 2
    dist = np.arange(0, 2 * MOBA_BLOCK, dtype=np.int32)
    d = np.maximum(dist, 1).astype(np.float32)
    large = max_exact + (np.log(d / np.float32(max_exact)) / np.float32(math.log(REL_MAX_DIST / max_exact))
                         * np.float32(REL_BUCKETS - max_exact)).astype(np.int32)
    large = np.minimum(large, REL_BUCKETS - 1)
    bucket = np.where(dist < max_exact, dist, large)
    assert np.all(np.diff(bucket) >= 0) and bucket[-1] == REL_BUCKETS - 1
    return [int(np.argmax(bucket >= k)) for k in range(1, REL_BUCKETS)]


_BUCKET_THRESHOLDS = _t5_bucket_thresholds()


def _attn_kernel(relb_ref, qt_ref, k_ref, vt_ref, o_ref,
                 kaug, vaug_t, kmean, kmean_hi, kmean_lo, bias_t, qaug_t, acc_t, mcol, logits_a, logits_b):
    hp = pl.program_id(1)
    i = pl.program_id(2)
    blk = MOBA_BLOCK
    n_blocks = k_ref.shape[0] // blk
    n_slots = kmean.shape[1]
    lane = lax.broadcasted_iota(jnp.int32, (blk, LANES), 1)
    dim = lax.broadcasted_iota(jnp.int32, (LANES, blk), 0)

    def in_head(index, h):
        return (index >= HEAD_DIM * h) & (index < HEAD_DIM * (h + 1))

    def other_off(h):
        return HEAD_DIM * (1 - h)

    @pl.when(i == 0)
    def _prepare():
        kmean[...] = jnp.zeros(kmean.shape, F32)
        for h in range(2):
            head = in_head(lane, h)
            off = other_off(h)

            def build(j, carry, h=h, head=head, off=off):
                rows = pl.ds(pl.multiple_of(j * blk, blk), blk)
                kb = k_ref[rows, :]
                onehot = jnp.where(lane - off == j, 1.0, 0.0).astype(BF16)
                kaug[h, rows, :] = jnp.where(head, kb, onehot)
                vtb = vt_ref[:, rows]
                vaug_t[h, :, rows] = jnp.where(in_head(dim, h), vtb, jnp.ones_like(vtb))
                mean = jnp.sum(kb.astype(F32), axis=0, keepdims=True) * (1.0 / blk)
                kmean[h, pl.ds(j, 1), :] = jnp.where(head[:1], mean, 0.0)
                return carry

            lax.fori_loop(0, n_blocks, build, 0)
            km = kmean[h]
            hi = km.astype(BF16)
            kmean_hi[h] = hi
            kmean_lo[h] = (km - hi.astype(F32)).astype(BF16)

            head_id = 2 * hp + h
            far = relb_ref[REL_BUCKETS - 1, head_id]
            key = lax.broadcasted_iota(jnp.int32, (blk, blk), 0)
            qry = lax.broadcasted_iota(jnp.int32, (blk, blk), 1)
            for t, base in enumerate((0, blk)):
                dist = base + qry - key
                val = jnp.full((blk, blk), relb_ref[0, head_id] - far, F32)
                for kk, thr in enumerate(_BUCKET_THRESHOLDS, start=1):
                    val = jnp.where(dist >= thr, relb_ref[kk, head_id] - far, val)
                bias_t[h, t] = jnp.where(dist >= 0, val, MASK_VALUE)

    qt = qt_ref[...]
    block_id = lax.broadcasted_iota(jnp.int32, (n_slots, blk), 0)
    past = block_id < i
    for h in range(2):
        off = other_off(h)
        qs = jnp.where(in_head(dim, h), qt * 0.125, jnp.zeros_like(qt))
        gate = _dot(kmean_hi[h], qs) + _dot(kmean_lo[h], qs)
        gate = jnp.where(past, gate, -jnp.inf)
        chosen = jnp.zeros(gate.shape, jnp.bool_)
        for _ in range(MOBA_TOPK):
            top = jnp.max(gate, axis=0, keepdims=True)
            first = jnp.min(jnp.where(gate == top, block_id, n_slots), axis=0, keepdims=True)
            hit = block_id == first
            chosen = chosen | (hit & (top > -jnp.inf))
            gate = jnp.where(hit, -jnp.inf, gate)
        qaug_t[h] = qs
        qaug_t[h, off:off + n_slots, :] = jnp.where(
            past & jnp.logical_not(chosen), MASK_VALUE, 0.0).astype(BF16)

    n_steps = i + 1
    slots = (logits_a, logits_b)

    def block_rows(t):
        j = jnp.maximum(i - t, 0)
        return pl.ds(pl.multiple_of(j * blk, blk), blk)

    def compute_logits(t, slot, bias_tile):
        for h in range(2):
            s = _dot(kaug[h, block_rows(t), :], qaug_t[h])
            if bias_tile is not None:
                s = s + bias_t[h, bias_tile]
            slots[slot][h] = s

    def softmax_update(t, slot):
        for h in range(2):
            s = slots[slot][h]
            m_old = mcol[h]
            m_new = jnp.maximum(m_old, jnp.max(s, axis=0, keepdims=True))
            p = jnp.exp(s - m_new)
            acc_t[h] = (acc_t[h] * jnp.exp(m_old - m_new)
                        + _dot(vaug_t[h, :, block_rows(t)], p.astype(BF16)))
            mcol[h] = m_new

    def step_pair(t, adjacent_bias):
        compute_logits(t + 1, 1, adjacent_bias)
        softmax_update(t, 0)
        compute_logits(t + 2, 0, None)
        softmax_update(t + 1, 1)

    mcol[...] = jnp.full(mcol.shape, -jnp.inf, F32)
    acc_t[...] = jnp.zeros(acc_t.shape, F32)
    compute_logits(0, 0, 0)

    @pl.when(i >= 1)
    def _own_and_adjacent():
        step_pair(0, 1)

    def far_pair(pair, carry):
        step_pair(2 * pair, None)
        return carry

    lax.fori_loop(1, n_steps // 2, far_pair, 0)

    @pl.when(n_steps % 2 == 1)
    def _last_step():
        softmax_update(n_steps - 1, 0)

    a0 = acc_t[0]
    a1 = acc_t[1]
    o_t = jnp.concatenate([a0[:HEAD_DIM] / a0[HEAD_DIM:HEAD_DIM + 1],
                           a1[HEAD_DIM:] / a1[0:1]], axis=0)
    o_ref[...] = o_t.T.astype(o_ref.dtype)


def _attention(proj3d, qv_t, rel_bias):
    b, s, _ = proj3d.shape
    n_blocks = s // MOBA_BLOCK
    n_slots = HEAD_DIM // 2
    assert s % MOBA_BLOCK == 0 and n_blocks <= n_slots
    blk = MOBA_BLOCK
    scratch = [
        pltpu.VMEM((2, s, LANES), BF16),
        pltpu.VMEM((2, LANES, s), BF16),
        pltpu.VMEM((2, n_slots, LANES), F32),
        pltpu.VMEM((2, n_slots, LANES), BF16),
        pltpu.VMEM((2, n_slots, LANES), BF16),
        pltpu.VMEM((2, 2, blk, blk), F32),
        pltpu.VMEM((2, LANES, blk), BF16),
        pltpu.VMEM((2, LANES, blk), F32),
        pltpu.VMEM((2, 1, blk), F32),
        pltpu.VMEM((2, blk, blk), F32),
        pltpu.VMEM((2, blk, blk), F32),
    ]
    vmem = 2 * 2 * s * LANES * 2 + 2 * 2 * s * LANES * 2 + 4 * blk * blk * 4 + (4 << 20)
    return pl.pallas_call(
        _attn_kernel,
        grid=(b, HEAD_PAIRS, n_blocks),
        in_specs=[
            pl.BlockSpec(memory_space=pltpu.SMEM),
            pl.BlockSpec((None, LANES, blk), lambda bi, hp, i: (bi, hp, i)),
            pl.BlockSpec((None, s, LANES), lambda bi, hp, i: (bi, 0, COL_K * HEAD_PAIRS + hp)),
            pl.BlockSpec((None, LANES, s), lambda bi, hp, i: (bi, HEAD_PAIRS + hp, 0)),
        ],
        out_specs=pl.BlockSpec((None, blk, LANES), lambda bi, hp, i: (bi, i, hp)),
        out_shape=jax.ShapeDtypeStruct((b, s, ATTN_WIDTH), BF16),
        scratch_shapes=scratch,
        compiler_params=pltpu.CompilerParams(
            dimension_semantics=("parallel", "parallel", "arbitrary"),
            vmem_limit_bytes=_vmem_limit(vmem)),
        name="moba_attn",
    )(rel_bias, qv_t, proj3d, qv_t)


def _ssm_weights(a_re, a_im, log_dt, b_re, b_im, c_re, c_im, d_skip):
    g, p_states, c = b_re.shape
    big_l = SSM_CHUNK
    dt = jnp.exp(log_dt.astype(F32))[:, None]
    ar = a_re.astype(F32)
    ai = a_im.astype(F32)
    mag = jnp.exp(dt * ar)
    ang = dt * ai
    abar_re = mag * jnp.cos(ang)
    abar_im = mag * jnp.sin(ang)
    den = ar * ar + ai * ai
    nr = abar_re - 1.0
    ni = abar_im
    fr = (nr * ar + ni * ai) / den
    fi = (ni * ar - nr * ai) / den
    br = b_re.astype(F32)
    bi = b_im.astype(F32)
    bbar_re = fr[..., None] * br - fi[..., None] * bi
    bbar_im = fr[..., None] * bi + fi[..., None] * br
    n = jnp.arange(big_l + 1, dtype=F32)[:, None, None]
    pmag = jnp.exp(n * (dt * ar)[None])
    pw_re = pmag * jnp.cos(n * ang[None])
    pw_im = pmag * jnp.sin(n * ang[None])
    cr = c_re.astype(F32)
    ci = c_im.astype(F32)
    ab_re = pw_re[..., None] * bbar_re[None] - pw_im[..., None] * bbar_im[None]
    ab_im = pw_re[..., None] * bbar_im[None] + pw_im[..., None] * bbar_re[None]
    hi = lax.Precision.HIGHEST
    kern = (jnp.einsum('gdp,ngpc->ngcd', cr, ab_re[:big_l], precision=hi)
            - jnp.einsum('gdp,ngpc->ngcd', ci, ab_im[:big_l], precision=hi))
    lag = jnp.arange(big_l)[None, :] - jnp.arange(big_l)[:, None]
    toep = kern[jnp.clip(lag, 0, big_l - 1)]
    toep = jnp.where((lag >= 0)[:, :, None, None, None], toep, 0.0)
    toep = toep.transpose(2, 0, 3, 1, 4).reshape(g, big_l * c, big_l * c)
    skip = jnp.tile(d_skip.astype(F32).reshape(g, 1, c), (1, big_l, 1)).reshape(g, big_l * c)
    toep = toep + skip[:, :, None] * jnp.eye(big_l * c, dtype=F32)[None]
    e_re = ab_re[:big_l][::-1].transpose(1, 0, 3, 2).reshape(g, big_l * c, p_states)
    e_im = ab_im[:big_l][::-1].transpose(1, 0, 3, 2).reshape(g, big_l * c, p_states)
    w_in = jnp.concatenate([e_re, e_im, e_im, e_re], axis=-1)
    ca_re = cr[None] * pw_re[1:, :, None, :] - ci[None] * pw_im[1:, :, None, :]
    ca_im = cr[None] * pw_im[1:, :, None, :] + ci[None] * pw_re[1:, :, None, :]
    o_re = ca_re.transpose(1, 3, 0, 2).reshape(g, p_states, big_l * c)
    o_im = -ca_im.transpose(1, 3, 0, 2).reshape(g, p_states, big_l * c)
    w_out = jnp.concatenate([o_re, o_im], axis=1)
    are, aim = pw_re[big_l], pw_im[big_l]
    carry = jnp.stack([jnp.concatenate([are, are], -1),
                       jnp.concatenate([-aim, aim], -1),
                       jnp.concatenate([aim, -aim], -1)], axis=1)
    return toep.astype(BF16), w_in.astype(BF16), w_out.astype(BF16), carry


def _ssm_kernel(u_ref, toep_ref, win_ref, wout_ref, carry_ref, y_ref, e_sc, prev_sc, *, batch):
    n_chunks = u_ref.shape[0] // batch
    half = LANES
    u = u_ref[...]
    e_sc[...] = _dot(u, win_ref[...])
    a1 = jnp.broadcast_to(carry_ref[0:1, :], (batch, half))
    a2 = jnp.broadcast_to(carry_ref[1:2, :], (batch, half))
    a3 = jnp.broadcast_to(carry_ref[2:3, :], (batch, half))

    def step(kk, state):
        st, st_swapped = state
        rows = pl.ds(pl.multiple_of(kk * batch, batch), batch)
        prev_sc[rows, :] = st
        e = e_sc[rows, :]
        new = a1 * st + a2 * st_swapped + e[:, :half]
        new_swapped = a1 * st_swapped + a3 * st + e[:, half:]
        return new, new_swapped

    zero = jnp.zeros((batch, half), F32)
    lax.fori_loop(0, n_chunks, step, (zero, zero), unroll=8)
    y = _dot(u, toep_ref[...]) + _dot(prev_sc[...].astype(BF16), wout_ref[...])
    y_ref[...] = y.astype(y_ref.dtype)


def _ssm(u_t, toep, w_in, w_out, carry, *, batch):
    g, rows, width = u_t.shape
    vmem = 2 * 2 * rows * width * 2 + rows * width * 4 + rows * LANES * 4 + 3 * rows * width * 4
    return pl.pallas_call(
        functools.partial(_ssm_kernel, batch=batch),
        grid=(g,),
        in_specs=[
            pl.BlockSpec((None, rows, width), lambda gi: (gi, 0, 0)),
            pl.BlockSpec((None, width, width), lambda gi: (gi, 0, 0)),
            pl.BlockSpec((None, width, width), lambda gi: (gi, 0, 0)),
            pl.BlockSpec((None, LANES, width), lambda gi: (gi, 0, 0)),
            pl.BlockSpec((None, 3, LANES), lambda gi: (gi, 0, 0)),
        ],
        out_specs=pl.BlockSpec((None, rows, width), lambda gi: (gi, 0, 0)),
        out_shape=jax.ShapeDtypeStruct((g, rows, width), BF16),
        scratch_shapes=[pltpu.VMEM((rows, width), F32), pltpu.VMEM((rows, LANES), F32)],
        compiler_params=pltpu.CompilerParams(
            dimension_semantics=("parallel",), vmem_limit_bytes=_vmem_limit(vmem)),
        name="s5_ssm",
    )(u_t, toep, w_in, w_out, carry)


def _final_kernel(x_ref, p_ref, oa_ref, ys_ref, za_ref, zs_ref, ga_ref, gs_ref,
                  wap_ref, wglu_ref, wsp_ref, wout_ref, wpg_ref, wpp_ref, lng_ref, lnb_ref, o_ref):
    x = x_ref[...]
    za = za_ref[...].astype(F32)
    a_in = oa_ref[...].astype(F32) * (za * _sigmoid(za))
    y_a = _dot(a_in.astype(BF16), wap_ref[...])
    ys = ys_ref[...].astype(F32)
    gelu = 0.5 * ys * (1.0 + lax.erf(ys * (2.0 ** -0.5)))
    glu = _dot(gelu.astype(BF16), wglu_ref[...])
    zs = zs_ref[...].astype(F32)
    s_in = glu[:, :SSM_WIDTH] * _sigmoid(glu[:, SSM_WIDTH:]) * (zs * _sigmoid(zs))
    y_s = _dot(s_in.astype(BF16), wsp_ref[...])
    merge = _sigmoid(ga_ref[...].astype(F32)) * y_a + _sigmoid(gs_ref[...].astype(F32)) * y_s
    mix = _dot(merge.astype(BF16), wout_ref[...])
    ple = _sigmoid(_dot(x.astype(BF16), wpg_ref[...])) * _dot(p_ref[...].astype(BF16), wpp_ref[...])
    hsum = DEEPNORM_ALPHA * x + mix + ple
    mu = jnp.mean(hsum, axis=-1, keepdims=True)
    cen = hsum - mu
    var = jnp.mean(cen * cen, axis=-1, keepdims=True)
    o_ref[...] = cen * lax.rsqrt(var + LN_EPS) * lng_ref[...] + lnb_ref[...]


def _final(x2d, p2d, proj, o_a, y_s, w_ap, w_glu, w_sp, w_out, w_pg, w_pp, ln_g, ln_b, *, row_tile=256):
    m = x2d.shape[0]
    half, full = SSM_WIDTH, D_MODEL

    def rows(width, col):
        return pl.BlockSpec((row_tile, width), lambda i: (i, col))

    def whole(arr):
        return pl.BlockSpec(arr.shape, lambda i: (0, 0))

    weights = (w_ap, w_glu, w_sp, w_out, w_pg, w_pp, ln_g, ln_b)
    vmem = (2 * sum(int(np.prod(w.shape)) * w.dtype.itemsize for w in weights)
            + 2 * row_tile * (2 * full * 4 + PLE_DIM * 4 + (4 * half + 2 * full) * 2)
            + 12 * row_tile * full * 4)
    return pl.pallas_call(
        _final_kernel,
        grid=(m // row_tile,),
        in_specs=[rows(full, 0), rows(PLE_DIM, 0), rows(half, 0), rows(half, 0),
                  rows(half, COL_ZA), rows(half, COL_ZS), rows(full, COL_GA), rows(full, COL_GS)]
                 + [whole(w) for w in weights],
        out_specs=rows(full, 0),
        out_shape=jax.ShapeDtypeStruct((m, full), F32),
        compiler_params=pltpu.CompilerParams(
            dimension_semantics=("parallel",), vmem_limit_bytes=_vmem_limit(vmem)),
        name="final",
    )(x2d, p2d, o_a, y_s, proj, proj, proj, proj, *weights)


def kernel(x, p, w_in, w_attn_proj, w_ssm_proj, w_out, ssm_a_re, ssm_a_im, ssm_log_dt, ssm_b_re, ssm_b_im, ssm_c_re, ssm_c_im, ssm_d, w_glu, w_ple_gate, w_ple_proj, ln_g, ln_b, rel_bias):
    b, s, d = x.shape
    m = b * s
    n_chunks = s // SSM_CHUNK
    for i in range(w_in.shape[0]):
        x2d = x.reshape(m, d)
        wq, wk, wv, wza, wu, wzs, wga, wgs = jnp.split(w_in[i].astype(BF16), _IN_SPLITS, axis=1)
        w_main = jnp.concatenate([wga, wgs, wk, wza, wzs, wu], axis=1)
        proj, qv_t = _proj(x2d, w_main, jnp.concatenate([wq, wv], axis=1).T, batch=b)
        o_a = _attention(proj.reshape(b, s, PROJ_WIDTH), qv_t, rel_bias.astype(F32))
        u = proj[:, COL_U * SSM_WIDTH:(COL_U + 1) * SSM_WIDTH]
        u_t = (u.reshape(b, n_chunks, SSM_CHUNK, SSM_GROUPS, SSM_GROUP)
               .transpose(3, 1, 0, 2, 4).reshape(SSM_GROUPS, n_chunks * b, SSM_CHUNK * SSM_GROUP))
        toep, s_in, s_out, carry = _ssm_weights(
            ssm_a_re[i], ssm_a_im[i], ssm_log_dt[i], ssm_b_re[i], ssm_b_im[i],
            ssm_c_re[i], ssm_c_im[i], ssm_d[i].reshape(SSM_GROUPS, SSM_GROUP))
        y_t = _ssm(u_t, toep, s_in, s_out, carry, batch=b)
        y_s = (y_t.reshape(SSM_GROUPS, n_chunks, b, SSM_CHUNK, SSM_GROUP)
               .transpose(2, 1, 3, 0, 4).reshape(m, SSM_WIDTH))
        x2d = _final(x2d, p[i].reshape(m, PLE_DIM), proj, o_a.reshape(m, ATTN_WIDTH), y_s,
                     w_attn_proj[i].astype(BF16), w_glu[i].astype(BF16), w_ssm_proj[i].astype(BF16),
                     w_out[i].astype(BF16), w_ple_gate[i].astype(BF16), w_ple_proj[i].astype(BF16),
                     ln_g[i].astype(F32).reshape(1, d), ln_b[i].astype(F32).reshape(1, d))
        x = x2d.reshape(b, s, d)
    return x
```

```python
import functools
import math

import numpy as np
import jax
import jax.numpy as jnp
from jax import lax
from jax.experimental import pallas as pl
from jax.experimental.pallas import tpu as pltpu

F32 = jnp.float32
BF16 = jnp.bfloat16

LANES = 128
SUBLANES = 8
V7X_VMEM_BYTES = 64 * 1024 * 1024

D_MODEL = 1024
PLE_DIM = 256
HEADS = 8
HEAD_DIM = 64
ATTN_WIDTH = HEADS * HEAD_DIM
HEAD_PAIRS = ATTN_WIDTH // LANES
MOBA_BLOCK = 256
MOBA_TOPK = 3
REL_BUCKETS = 32
REL_MAX_DIST = 128
SSM_WIDTH = 512
SSM_GROUP = 16
SSM_GROUPS = SSM_WIDTH // SSM_GROUP
SSM_STATE = 64
SSM_CHUNK = 16
IN_WIDTH = 4 * ATTN_WIDTH + 2 * SSM_WIDTH + 2 * D_MODEL
DEPTH = 1
DEEPNORM_ALPHA = (2.0 * DEPTH) ** 0.25
LN_EPS = 1e-5
MASK_VALUE = -1e30

PROJ_WIDTH = IN_WIDTH - 2 * ATTN_WIDTH - SSM_WIDTH
_IN_SPLITS = tuple(int(v) for v in np.cumsum(
    (ATTN_WIDTH,) * 4 + (SSM_WIDTH,) * 2 + (D_MODEL,) * 2)[:-1])
COL_GA, COL_GS = 0, 1
COL_K, COL_ZA, COL_ZS = 4, 5, 6


def _dot(a, b):
    return jnp.dot(a, b, preferred_element_type=F32)


def _dot_nt(a, b):
    return lax.dot_general(a, b, (((1,), (1,)), ((), ())), preferred_element_type=F32)


def _sigmoid(v):
    return 1.0 / (1.0 + jnp.exp(-v))


def _vmem_limit(nbytes):
    return int(min(V7X_VMEM_BYTES - (4 << 20), max(nbytes + (8 << 20), 32 << 20)))


def _proj_kernel(x_ref, w_ref, wt_ref, wu_ref, o_ref, t_ref, uf_ref, u_sc, *, col_tile):
    xb = x_ref[...].astype(BF16)
    u = _dot(xb, wu_ref[...])
    for gb in range(u_sc.shape[0]):
        u_sc[gb] = u[:, gb * LANES:(gb + 1) * LANES]
    n_cols = o_ref.shape[1]
    for start in range(0, n_cols, col_tile):
        cols = slice(start, min(start + col_tile, n_cols))
        o_ref[:, cols] = _dot(xb, w_ref[:, cols]).astype(BF16)
    t_ref[...] = _dot_nt(wt_ref[...], xb).astype(BF16)
    n_chunks = u_sc.shape[1] // SSM_CHUNK
    groups_per_tile = LANES // SSM_GROUP
    steps = [[u_sc[gb, pl.ds(t, n_chunks, stride=SSM_CHUNK), :]
              for gb in range(u_sc.shape[0])] for t in range(SSM_CHUNK)]
    for g in range(SSM_GROUPS):
        gb, lo = g // groups_per_tile, (g % groups_per_tile) * SSM_GROUP
        folded = jnp.concatenate([steps[t][gb][:, lo:lo + SSM_GROUP] for t in range(SSM_CHUNK)], axis=1)
        uf_ref[g] = folded.astype(BF16)


def _proj(x2d, w_bf16, wt_bf16, wu_bf16, *, batch, row_tile=512, col_tile=1024):
    m, k = x2d.shape
    n = w_bf16.shape[1]
    nt = wt_bf16.shape[0]
    nu = wu_bf16.shape[1]
    tiles_per_batch = m // batch // row_tile
    fold_rows = row_tile // SSM_CHUNK
    vmem = (2 * row_tile * k * 4 + 2 * k * (n + nt + nu) * 2 + 2 * row_tile * (n + nt + nu) * 2
            + row_tile * k * 2 + 2 * row_tile * col_tile * 4 + row_tile * nu * 4)
    return pl.pallas_call(
        functools.partial(_proj_kernel, col_tile=col_tile),
        grid=(m // row_tile,),
        in_specs=[pl.BlockSpec((row_tile, k), lambda i: (i, 0)),
                  pl.BlockSpec((k, n), lambda i: (0, 0)),
                  pl.BlockSpec((nt, k), lambda i: (0, 0)),
                  pl.BlockSpec((k, nu), lambda i: (0, 0))],
        out_specs=[pl.BlockSpec((row_tile, n), lambda i: (i, 0)),
                   pl.BlockSpec((None, nt, row_tile),
                                lambda i: (i // tiles_per_batch, 0, i % tiles_per_batch)),
                   pl.BlockSpec((SSM_GROUPS, fold_rows, SSM_CHUNK * SSM_GROUP), lambda i: (0, i, 0))],
        out_shape=[jax.ShapeDtypeStruct((m, n), BF16),
                   jax.ShapeDtypeStruct((batch, nt, m // batch), BF16),
                   jax.ShapeDtypeStruct((SSM_GROUPS, m // SSM_CHUNK, SSM_CHUNK * SSM_GROUP), BF16)],
        scratch_shapes=[pltpu.VMEM((nu // LANES, row_tile, LANES), F32)],
        compiler_params=pltpu.CompilerParams(
            dimension_semantics=("parallel",), vmem_limit_bytes=_vmem_limit(vmem)),
        name="proj",
    )(x2d, w_bf16, wt_bf16, wu_bf16)


def _t5_bucket_thresholds():
    max_exact = REL_BUCKETS // 2
    dist = np.arange(0, 2 * MOBA_BLOCK, dtype=np.int32)
    d = np.maximum(dist, 1).astype(np.float32)
    large = max_exact + (np.log(d / np.float32(max_exact)) / np.float32(math.log(REL_MAX_DIST / max_exact))
                         * np.float32(REL_BUCKETS - max_exact)).astype(np.int32)
    large = np.minimum(large, REL_BUCKETS - 1)
    bucket = np.where(dist < max_exact, dist, large)
    assert np.all(np.diff(bucket) >= 0) and bucket[-1] == REL_BUCKETS - 1
    return [int(np.argmax(bucket >= k)) for k in range(1, REL_BUCKETS)]


_BUCKET_THRESHOLDS = _t5_bucket_thresholds()


def _attn_kernel(relb_ref, qt_ref, k_ref, vt_ref, o_ref,
                 kaug, vaug_t, kmean, kmean_hi, kmean_lo, bias_t, qaug_t, acc_t, mcol, logits_a, logits_b):
    hp = pl.program_id(1)
    i = pl.program_id(2)
    blk = MOBA_BLOCK
    n_blocks = k_ref.shape[0] // blk
    n_slots = kmean.shape[1]
    lane = lax.broadcasted_iota(jnp.int32, (blk, LANES), 1)
    dim = lax.broadcasted_iota(jnp.int32, (LANES, blk), 0)

    def in_head(index, h):
        return (index >= HEAD_DIM * h) & (index < HEAD_DIM * (h + 1))

    def other_off(h):
        return HEAD_DIM * (1 - h)

    @pl.when(i == 0)
    def _prepare():
        kmean[...] = jnp.zeros(kmean.shape, F32)
        for h in range(2):
            head = in_head(lane, h)
            off = other_off(h)

            def build(j, carry, h=h, head=head, off=off):
                rows = pl.ds(pl.multiple_of(j * blk, blk), blk)
                kb = k_ref[rows, :]
                onehot = jnp.where(lane - off == j, 1.0, 0.0).astype(BF16)
                kaug[h, rows, :] = jnp.where(head, kb, onehot)
                vtb = vt_ref[:, rows]
                vaug_t[h, :, rows] = jnp.where(in_head(dim, h), vtb, jnp.ones_like(vtb))
                mean = jnp.sum(kb.astype(F32), axis=0, keepdims=True) * (1.0 / blk)
                kmean[h, pl.ds(j, 1), :] = jnp.where(head[:1], mean, 0.0)
                return carry

            lax.fori_loop(0, n_blocks, build, 0)
            km = kmean[h]
            hi = km.astype(BF16)
            kmean_hi[h] = hi
            kmean_lo[h] = (km - hi.astype(F32)).astype(BF16)

            head_id = 2 * hp + h
            far = relb_ref[REL_BUCKETS - 1, head_id]
            key = lax.broadcasted_iota(jnp.int32, (blk, blk), 0)
            qry = lax.broadcasted_iota(jnp.int32, (blk, blk), 1)
            for t, base in enumerate((0, blk)):
                dist = base + qry - key
                val = jnp.full((blk, blk), relb_ref[0, head_id] - far, F32)
                for kk, thr in enumerate(_BUCKET_THRESHOLDS, start=1):
                    val = jnp.where(dist >= thr, relb_ref[kk, head_id] - far, val)
                bias_t[h, t] = jnp.where(dist >= 0, val, MASK_VALUE)

    qt = qt_ref[...]
    block_id = lax.broadcasted_iota(jnp.int32, (n_slots, blk), 0)
    past = block_id < i
    for h in range(2):
        off = other_off(h)
        qs = jnp.where(in_head(dim, h), qt * 0.125, jnp.zeros_like(qt))
        gate = _dot(kmean_hi[h], qs) + _dot(kmean_lo[h], qs)
        gate = jnp.where(past, gate, -jnp.inf)
        chosen = jnp.zeros(gate.shape, jnp.bool_)
        for _ in range(MOBA_TOPK):
            top = jnp.max(gate, axis=0, keepdims=True)
            first = jnp.min(jnp.where(gate == top, block_id, n_slots), axis=0, keepdims=True)
            hit = block_id == first
            chosen = chosen | (hit & (top > -jnp.inf))
            gate = jnp.where(hit, -jnp.inf, gate)
        qaug_t[h] = qs
        qaug_t[h, off:off + n_slots, :] = jnp.where(
            past & jnp.logical_not(chosen), MASK_VALUE, 0.0).astype(BF16)

    n_steps = i + 1
    slots = (logits_a, logits_b)

    def block_rows(t):
        j = jnp.maximum(i - t, 0)
        return pl.ds(pl.multiple_of(j * blk, blk), blk)

    def compute_logits(t, slot, bias_tile):
        for h in range(2):
            s = _dot(kaug[h, block_rows(t), :], qaug_t[h])
            if bias_tile is not None:
                s = s + bias_t[h, bias_tile]
            slots[slot][h] = s

    def softmax_update(t, slot):
        for h in range(2):
            s = slots[slot][h]
            m_old = mcol[h]
            m_new = jnp.maximum(m_old, jnp.max(s, axis=0, keepdims=True))
            p = jnp.exp(s - m_new)
            acc_t[h] = (acc_t[h] * jnp.exp(m_old - m_new)
                        + _dot(vaug_t[h, :, block_rows(t)], p.astype(BF16)))
            mcol[h] = m_new

    def step_pair(t, adjacent_bias):
        compute_logits(t + 1, 1, adjacent_bias)
        softmax_update(t, 0)
        compute_logits(t + 2, 0, None)
        softmax_update(t + 1, 1)

    mcol[...] = jnp.full(mcol.shape, -jnp.inf, F32)
    acc_t[...] = jnp.zeros(acc_t.shape, F32)
    compute_logits(0, 0, 0)

    @pl.when(i >= 1)
    def _own_and_adjacent():
        step_pair(0, 1)

    def far_pair(pair, carry):
        step_pair(2 * pair, None)
        return carry

    lax.fori_loop(1, n_steps // 2, far_pair, 0)

    @pl.when(n_steps % 2 == 1)
    def _last_step():
        softmax_update(n_steps - 1, 0)

    a0 = acc_t[0]
    a1 = acc_t[1]
    o_t = jnp.concatenate([a0[:HEAD_DIM] / a0[HEAD_DIM:HEAD_DIM + 1],
                           a1[HEAD_DIM:] / a1[0:1]], axis=0)
    o_ref[...] = o_t.T.astype(o_ref.dtype)


def _attention(proj3d, qv_t, rel_bias):
    b, s, _ = proj3d.shape
    n_blocks = s // MOBA_BLOCK
    n_slots = HEAD_DIM // 2
    assert s % MOBA_BLOCK == 0 and n_blocks <= n_slots
    blk = MOBA_BLOCK
    scratch = [
        pltpu.VMEM((2, s, LANES), BF16),
        pltpu.VMEM((2, LANES, s), BF16),
        pltpu.VMEM((2, n_slots, LANES), F32),
        pltpu.VMEM((2, n_slots, LANES), BF16),
        pltpu.VMEM((2, n_slots, LANES), BF16),
        pltpu.VMEM((2, 2, blk, blk), F32),
        pltpu.VMEM((2, LANES, blk), BF16),
        pltpu.VMEM((2, LANES, blk), F32),
        pltpu.VMEM((2, 1, blk), F32),
        pltpu.VMEM((2, blk, blk), F32),
        pltpu.VMEM((2, blk, blk), F32),
    ]
    vmem = 2 * 2 * s * LANES * 2 + 2 * 2 * s * LANES * 2 + 4 * blk * blk * 4 + (4 << 20)
    return pl.pallas_call(
        _attn_kernel,
        grid=(b, HEAD_PAIRS, n_blocks),
        in_specs=[
            pl.BlockSpec(memory_space=pltpu.SMEM),
            pl.BlockSpec((None, LANES, blk), lambda bi, hp, i: (bi, hp, i)),
            pl.BlockSpec((None, s, LANES), lambda bi, hp, i: (bi, 0, COL_K * HEAD_PAIRS + hp)),
            pl.BlockSpec((None, LANES, s), lambda bi, hp, i: (bi, HEAD_PAIRS + hp, 0)),
        ],
        out_specs=pl.BlockSpec((None, blk, LANES), lambda bi, hp, i: (bi, i, hp)),
        out_shape=jax.ShapeDtypeStruct((b, s, ATTN_WIDTH), BF16),
        scratch_shapes=scratch,
        compiler_params=pltpu.CompilerParams(
            dimension_semantics=("parallel", "parallel", "arbitrary"),
            vmem_limit_bytes=_vmem_limit(vmem)),
        name="moba_attn",
    )(rel_bias, qv_t, proj3d, qv_t)


def _ssm_weights(a_re, a_im, log_dt, b_re, b_im, c_re, c_im, d_skip):
    g, p_states, c = b_re.shape
    big_l = SSM_CHUNK
    dt = jnp.exp(log_dt.astype(F32))[:, None]
    ar = a_re.astype(F32)
    ai = a_im.astype(F32)
    mag = jnp.exp(dt * ar)
    ang = dt * ai
    abar_re = mag * jnp.cos(ang)
    abar_im = mag * jnp.sin(ang)
    den = ar * ar + ai * ai
    nr = abar_re - 1.0
    ni = abar_im
    fr = (nr * ar + ni * ai) / den
    fi = (ni * ar - nr * ai) / den
    br = b_re.astype(F32)
    bi = b_im.astype(F32)
    bbar_re = fr[..., None] * br - fi[..., None] * bi
    bbar_im = fr[..., None] * bi + fi[..., None] * br
    n = jnp.arange(big_l + 1, dtype=F32)[:, None, None]
    pmag = jnp.exp(n * (dt * ar)[None])
    pw_re = pmag * jnp.cos(n * ang[None])
    pw_im = pmag * jnp.sin(n * ang[None])
    cr = c_re.astype(F32)
    ci = c_im.astype(F32)
    ab_re = pw_re[..., None] * bbar_re[None] - pw_im[..., None] * bbar_im[None]
    ab_im = pw_re[..., None] * bbar_im[None] + pw_im[..., None] * bbar_re[None]
    hi = lax.Precision.HIGHEST
    kern = (jnp.einsum('gdp,ngpc->ngcd', cr, ab_re[:big_l], precision=hi)
            - jnp.einsum('gdp,ngpc->ngcd', ci, ab_im[:big_l], precision=hi))
    lag = jnp.arange(big_l)[None, :] - jnp.arange(big_l)[:, None]
    toep = kern[jnp.clip(lag, 0, big_l - 1)]
    toep = jnp.where((lag >= 0)[:, :, None, None, None], toep, 0.0)
    toep = toep.transpose(2, 0, 3, 1, 4).reshape(g, big_l * c, big_l * c)
    skip = jnp.tile(d_skip.astype(F32).reshape(g, 1, c), (1, big_l, 1)).reshape(g, big_l * c)
    toep = toep + skip[:, :, None] * jnp.eye(big_l * c, dtype=F32)[None]
    e_re = ab_re[:big_l][::-1].transpose(1, 0, 3, 2).reshape(g, big_l * c, p_states)
    e_im = ab_im[:big_l][::-1].transpose(1, 0, 3, 2).reshape(g, big_l * c, p_states)
    w_in = jnp.concatenate([e_re, e_im, e_im, e_re], axis=-1)
    ca_re = cr[None] * pw_re[1:, :, None, :] - ci[None] * pw_im[1:, :, None, :]
    ca_im = cr[None] * pw_im[1:, :, None, :] + ci[None] * pw_re[1:, :, None, :]
    o_re = ca_re.transpose(1, 3, 0, 2).reshape(g, p_states, big_l * c)
    o_im = -ca_im.transpose(1, 3, 0, 2).reshape(g, p_states, big_l * c)
    w_out = jnp.concatenate([o_re, o_im], axis=1)
    are, aim = pw_re[big_l], pw_im[big_l]
    carry = jnp.stack([jnp.concatenate([are, are], -1),
                       jnp.concatenate([-aim, aim], -1),
                       jnp.concatenate([aim, -aim], -1)], axis=1)
    return toep.astype(BF16), w_in.astype(BF16), w_out.astype(BF16), carry


def _ssm_kernel(u_ref, toep_ref, win_ref, wout_ref, carry_ref, y_ref, e_sc, prev_sc, *, batch):
    n_chunks = u_ref.shape[0] // batch
    half = LANES
    u = u_ref[...]
    e = _dot(u, win_ref[...])
    e_sc[0] = e[:, :half]
    e_sc[1] = e[:, half:]
    a1 = jnp.broadcast_to(carry_ref[0:1, :], (batch, half))
    a2 = jnp.broadcast_to(carry_ref[1:2, :], (batch, half))
    a3 = jnp.broadcast_to(carry_ref[2:3, :], (batch, half))

    def step(kk, state):
        st, st_swapped = state
        rows = pl.ds(kk, batch, stride=n_chunks)
        prev_sc[rows, :] = st
        new = a1 * st + a2 * st_swapped + e_sc[0, rows, :]
        new_swapped = a1 * st_swapped + a3 * st + e_sc[1, rows, :]
        return new, new_swapped

    zero = jnp.zeros((batch, half), F32)
    lax.fori_loop(0, n_chunks, step, (zero, zero), unroll=8)
    y = _dot(u, toep_ref[...]) + _dot(prev_sc[...].astype(BF16), wout_ref[...])
    y_ref[...] = y.astype(y_ref.dtype)


def _ssm(u_t, toep, w_in, w_out, carry, *, batch):
    g, rows, width = u_t.shape
    vmem = 2 * 2 * rows * width * 2 + rows * width * 4 + rows * LANES * 4 + 3 * rows * width * 4
    return pl.pallas_call(
        functools.partial(_ssm_kernel, batch=batch),
        grid=(g,),
        in_specs=[
            pl.BlockSpec((None, rows, width), lambda gi: (gi, 0, 0)),
            pl.BlockSpec((None, width, width), lambda gi: (gi, 0, 0)),
            pl.BlockSpec((None, width, width), lambda gi: (gi, 0, 0)),
            pl.BlockSpec((None, LANES, width), lambda gi: (gi, 0, 0)),
            pl.BlockSpec((None, 3, LANES), lambda gi: (gi, 0, 0)),
        ],
        out_specs=pl.BlockSpec((None, rows, width), lambda gi: (gi, 0, 0)),
        out_shape=jax.ShapeDtypeStruct((g, rows, width), BF16),
        scratch_shapes=[pltpu.VMEM((2, rows, LANES), F32), pltpu.VMEM((rows, LANES), F32)],
        compiler_params=pltpu.CompilerParams(
            dimension_semantics=("parallel",), vmem_limit_bytes=_vmem_limit(vmem)),
        name="s5_ssm",
    )(u_t, toep, w_in, w_out, carry)


def _final_kernel(x_ref, p_ref, oa_ref, yf_ref, za_ref, zs_ref, ga_ref, gs_ref,
                  wap_ref, wglu_ref, wsp_ref, wout_ref, wpg_ref, wpp_ref, lng_ref, lnb_ref, o_ref, ys_sc):
    n_chunks = ys_sc.shape[1] // SSM_CHUNK
    groups_per_tile = LANES // SSM_GROUP
    folded = [yf_ref[g].astype(F32) for g in range(SSM_GROUPS)]
    for t in range(SSM_CHUNK):
        for gb in range(ys_sc.shape[0]):
            tile = jnp.concatenate(
                [folded[gb * groups_per_tile + gl][:, t * SSM_GROUP:(t + 1) * SSM_GROUP]
                 for gl in range(groups_per_tile)], axis=1)
            ys_sc[gb, pl.ds(t, n_chunks, stride=SSM_CHUNK), :] = tile
    x = x_ref[...]
    za = za_ref[...].astype(F32)
    a_in = oa_ref[...].astype(F32) * (za * _sigmoid(za))
    y_a = _dot(a_in.astype(BF16), wap_ref[...])
    ys = jnp.concatenate([ys_sc[gb] for gb in range(ys_sc.shape[0])], axis=1)
    gelu = 0.5 * ys * (1.0 + lax.erf(ys * (2.0 ** -0.5)))
    glu = _dot(gelu.astype(BF16), wglu_ref[...])
    zs = zs_ref[...].astype(F32)
    s_in = glu[:, :SSM_WIDTH] * _sigmoid(glu[:, SSM_WIDTH:]) * (zs * _sigmoid(zs))
    y_s = _dot(s_in.astype(BF16), wsp_ref[...])
    merge = _sigmoid(ga_ref[...].astype(F32)) * y_a + _sigmoid(gs_ref[...].astype(F32)) * y_s
    mix = _dot(merge.astype(BF16), wout_ref[...])
    ple = _sigmoid(_dot(x.astype(BF16), wpg_ref[...])) * _dot(p_ref[...].astype(BF16), wpp_ref[...])
    hsum = DEEPNORM_ALPHA * x + mix + ple
    mu = jnp.mean(hsum, axis=-1, keepdims=True)
    cen = hsum - mu
    var = jnp.mean(cen * cen, axis=-1, keepdims=True)
    o_ref[...] = cen * lax.rsqrt(var + LN_EPS) * lng_ref[...] + lnb_ref[...]


def _final(x2d, p2d, proj, o_a, y_fold, w_ap, w_glu, w_sp, w_out, w_pg, w_pp, ln_g, ln_b, *, row_tile=256):
    m = x2d.shape[0]
    half, full = SSM_WIDTH, D_MODEL

    def rows(width, col):
        return pl.BlockSpec((row_tile, width), lambda i: (i, col))

    def whole(arr):
        return pl.BlockSpec(arr.shape, lambda i: (0, 0))

    weights = (w_ap, w_glu, w_sp, w_out, w_pg, w_pp, ln_g, ln_b)
    vmem = (2 * sum(int(np.prod(w.shape)) * w.dtype.itemsize for w in weights)
            + 2 * row_tile * (2 * full * 4 + PLE_DIM * 4 + (4 * half + 2 * full) * 2)
            + 12 * row_tile * full * 4)
    return pl.pallas_call(
        _final_kernel,
        grid=(m // row_tile,),
        in_specs=[rows(full, 0), rows(PLE_DIM, 0), rows(half, 0),
                  pl.BlockSpec((SSM_GROUPS, row_tile // SSM_CHUNK, SSM_CHUNK * SSM_GROUP), lambda i: (0, i, 0)),
                  rows(half, COL_ZA), rows(half, COL_ZS), rows(full, COL_GA), rows(full, COL_GS)]
                 + [whole(w) for w in weights],
        out_specs=rows(full, 0),
        out_shape=jax.ShapeDtypeStruct((m, full), F32),
        scratch_shapes=[pltpu.VMEM((half // LANES, row_tile, LANES), F32)],
        compiler_params=pltpu.CompilerParams(
            dimension_semantics=("parallel",), vmem_limit_bytes=_vmem_limit(vmem)),
        name="final",
    )(x2d, p2d, o_a, y_fold, proj, proj, proj, proj, *weights)


def kernel(x, p, w_in, w_attn_proj, w_ssm_proj, w_out, ssm_a_re, ssm_a_im, ssm_log_dt, ssm_b_re, ssm_b_im, ssm_c_re, ssm_c_im, ssm_d, w_glu, w_ple_gate, w_ple_proj, ln_g, ln_b, rel_bias):
    b, s, d = x.shape
    m = b * s
    n_chunks = s // SSM_CHUNK
    for i in range(w_in.shape[0]):
        x2d = x.reshape(m, d)
        wq, wk, wv, wza, wu, wzs, wga, wgs = jnp.split(w_in[i].astype(BF16), _IN_SPLITS, axis=1)
        w_main = jnp.concatenate([wga, wgs, wk, wza, wzs], axis=1)
        proj, qv_t, u_fold = _proj(x2d, w_main, jnp.concatenate([wq, wv], axis=1).T, wu, batch=b)
        o_a = _attention(proj.reshape(b, s, PROJ_WIDTH), qv_t, rel_bias.astype(F32))
        toep, s_in, s_out, carry = _ssm_weights(
            ssm_a_re[i], ssm_a_im[i], ssm_log_dt[i], ssm_b_re[i], ssm_b_im[i],
            ssm_c_re[i], ssm_c_im[i], ssm_d[i].reshape(SSM_GROUPS, SSM_GROUP))
        y_fold = _ssm(u_fold, toep, s_in, s_out, carry, batch=b)
        x2d = _final(x2d, p[i].reshape(m, PLE_DIM), proj, o_a.reshape(m, ATTN_WIDTH), y_fold,
                     w_attn_proj[i].astype(BF16), w_glu[i].astype(BF16), w_ssm_proj[i].astype(BF16),
                     w_out[i].astype(BF16), w_ple_gate[i].astype(BF16), w_ple_proj[i].astype(BF16),
                     ln_g[i].astype(F32).reshape(1, d), ln_b[i].astype(F32).reshape(1, d))
        x = x2d.reshape(b, s, d)
    return x
```

```python
import functools
import math

import numpy as np
import jax
import jax.numpy as jnp
from jax import lax
from jax.experimental import pallas as pl
from jax.experimental.pallas import tpu as pltpu

F32 = jnp.float32
BF16 = jnp.bfloat16

LANES = 128
SUBLANES = 8
V7X_VMEM_BYTES = 64 * 1024 * 1024

D_MODEL = 1024
PLE_DIM = 256
HEADS = 8
HEAD_DIM = 64
ATTN_WIDTH = HEADS * HEAD_DIM
HEAD_PAIRS = ATTN_WIDTH // LANES
MOBA_BLOCK = 256
MOBA_TOPK = 3
REL_BUCKETS = 32
REL_MAX_DIST = 128
SSM_WIDTH = 512
SSM_GROUP = 16
SSM_GROUPS = SSM_WIDTH // SSM_GROUP
SSM_STATE = 64
SSM_CHUNK = 16
IN_WIDTH = 4 * ATTN_WIDTH + 2 * SSM_WIDTH + 2 * D_MODEL
DEPTH = 1
DEEPNORM_ALPHA = (2.0 * DEPTH) ** 0.25
LN_EPS = 1e-5
MASK_VALUE = -1e30

PROJ_WIDTH = IN_WIDTH - 2 * ATTN_WIDTH - SSM_WIDTH
_IN_SPLITS = tuple(int(v) for v in np.cumsum(
    (ATTN_WIDTH,) * 4 + (SSM_WIDTH,) * 2 + (D_MODEL,) * 2)[:-1])
COL_GA, COL_GS = 0, 1
COL_K, COL_ZA, COL_ZS = 4, 5, 6


def _dot(a, b):
    return jnp.dot(a, b, preferred_element_type=F32)


def _dot_nt(a, b):
    return lax.dot_general(a, b, (((1,), (1,)), ((), ())), preferred_element_type=F32)


def _sigmoid(v):
    return 1.0 / (1.0 + jnp.exp(-v))


def _vmem_limit(nbytes):
    return int(min(V7X_VMEM_BYTES - (4 << 20), max(nbytes + (8 << 20), 32 << 20)))


def _proj_kernel(x_ref, w_ref, wt_ref, wu_ref, o_ref, t_ref, uf_ref, u_sc, *, col_tile):
    xb = x_ref[...].astype(BF16)
    u = _dot(xb, wu_ref[...])
    for gb in range(u_sc.shape[0]):
        u_sc[gb] = u[:, gb * LANES:(gb + 1) * LANES]
    n_cols = o_ref.shape[1]
    for start in range(0, n_cols, col_tile):
        cols = slice(start, min(start + col_tile, n_cols))
        o_ref[:, cols] = _dot(xb, w_ref[:, cols]).astype(BF16)
    t_ref[...] = _dot_nt(wt_ref[...], xb).astype(BF16)
    n_chunks = u_sc.shape[1] // SSM_CHUNK
    groups_per_tile = LANES // SSM_GROUP
    steps = [[u_sc[gb, pl.ds(t, n_chunks, stride=SSM_CHUNK), :]
              for gb in range(u_sc.shape[0])] for t in range(SSM_CHUNK)]
    for g in range(SSM_GROUPS):
        gb, lo = g // groups_per_tile, (g % groups_per_tile) * SSM_GROUP
        folded = jnp.concatenate([steps[t][gb][:, lo:lo + SSM_GROUP] for t in range(SSM_CHUNK)], axis=1)
        uf_ref[g] = folded.astype(BF16)


def _proj(x2d, w_bf16, wt_bf16, wu_bf16, *, batch, row_tile=512, col_tile=1024):
    m, k = x2d.shape
    n = w_bf16.shape[1]
    nt = wt_bf16.shape[0]
    nu = wu_bf16.shape[1]
    tiles_per_batch = m // batch // row_tile
    fold_rows = row_tile // SSM_CHUNK
    vmem = (2 * row_tile * k * 4 + 2 * k * (n + nt + nu) * 2 + 2 * row_tile * (n + nt + nu) * 2
            + row_tile * k * 2 + 2 * row_tile * col_tile * 4 + row_tile * nu * 4)
    return pl.pallas_call(
        functools.partial(_proj_kernel, col_tile=col_tile),
        grid=(m // row_tile,),
        in_specs=[pl.BlockSpec((row_tile, k), lambda i: (i, 0)),
                  pl.BlockSpec((k, n), lambda i: (0, 0)),
                  pl.BlockSpec((nt, k), lambda i: (0, 0)),
                  pl.BlockSpec((k, nu), lambda i: (0, 0))],
        out_specs=[pl.BlockSpec((row_tile, n), lambda i: (i, 0)),
                   pl.BlockSpec((None, nt, row_tile),
                                lambda i: (i // tiles_per_batch, 0, i % tiles_per_batch)),
                   pl.BlockSpec((SSM_GROUPS, fold_rows, SSM_CHUNK * SSM_GROUP), lambda i: (0, i, 0))],
        out_shape=[jax.ShapeDtypeStruct((m, n), BF16),
                   jax.ShapeDtypeStruct((batch, nt, m // batch), BF16),
                   jax.ShapeDtypeStruct((SSM_GROUPS, m // SSM_CHUNK, SSM_CHUNK * SSM_GROUP), BF16)],
        scratch_shapes=[pltpu.VMEM((nu // LANES, row_tile, LANES), F32)],
        compiler_params=pltpu.CompilerParams(
            dimension_semantics=("parallel",), vmem_limit_bytes=_vmem_limit(vmem)),
        name="proj",
    )(x2d, w_bf16, wt_bf16, wu_bf16)


def _t5_bucket_thresholds():
    max_exact = REL_BUCKETS // 2
    dist = np.arange(0, 2 * MOBA_BLOCK, dtype=np.int32)
    d = np.maximum(dist, 1).astype(np.float32)
    large = max_exact + (np.log(d / np.float32(max_exact)) / np.float32(math.log(REL_MAX_DIST / max_exact))
                         * np.float32(REL_BUCKETS - max_exact)).astype(np.int32)
    large = np.minimum(large, REL_BUCKETS - 1)
    bucket = np.where(dist < max_exact, dist, large)
    assert np.all(np.diff(bucket) >= 0) and bucket[-1] == REL_BUCKETS - 1
    return [int(np.argmax(bucket >= k)) for k in range(1, REL_BUCKETS)]


_BUCKET_THRESHOLDS = _t5_bucket_thresholds()


def _attn_kernel(relb_ref, qt_ref, k_ref, vt_ref, o_ref,
                 kaug, vaug_t, kmean, kmean_hi, kmean_lo, bias_t, qaug_t, acc_t, mcol, logits_a, logits_b, smax):
    hp = pl.program_id(1)
    i = pl.program_id(2)
    blk = MOBA_BLOCK
    n_blocks = k_ref.shape[0] // blk
    n_slots = kmean.shape[1]
    lane = lax.broadcasted_iota(jnp.int32, (blk, LANES), 1)
    dim = lax.broadcasted_iota(jnp.int32, (LANES, blk), 0)

    def in_head(index, h):
        return (index >= HEAD_DIM * h) & (index < HEAD_DIM * (h + 1))

    def other_off(h):
        return HEAD_DIM * (1 - h)

    @pl.when(i == 0)
    def _prepare():
        kmean[...] = jnp.zeros(kmean.shape, F32)
        for h in range(2):
            head = in_head(lane, h)
            off = other_off(h)

            def build(j, carry, h=h, head=head, off=off):
                rows = pl.ds(pl.multiple_of(j * blk, blk), blk)
                kb = k_ref[rows, :]
                onehot = jnp.where(lane - off == j, 1.0, 0.0).astype(BF16)
                kaug[h, rows, :] = jnp.where(head, kb, onehot)
                vtb = vt_ref[:, rows]
                vaug_t[h, :, rows] = jnp.where(in_head(dim, h), vtb, jnp.ones_like(vtb))
                mean = jnp.sum(kb.astype(F32), axis=0, keepdims=True) * (1.0 / blk)
                kmean[h, pl.ds(j, 1), :] = jnp.where(head[:1], mean, 0.0)
                return carry

            lax.fori_loop(0, n_blocks, build, 0)
            km = kmean[h]
            hi = km.astype(BF16)
            kmean_hi[h] = hi
            kmean_lo[h] = (km - hi.astype(F32)).astype(BF16)

            head_id = 2 * hp + h
            far = relb_ref[REL_BUCKETS - 1, head_id]
            key = lax.broadcasted_iota(jnp.int32, (blk, blk), 0)
            qry = lax.broadcasted_iota(jnp.int32, (blk, blk), 1)
            for t, base in enumerate((0, blk)):
                dist = base + qry - key
                val = jnp.full((blk, blk), relb_ref[0, head_id] - far, F32)
                for kk, thr in enumerate(_BUCKET_THRESHOLDS, start=1):
                    val = jnp.where(dist >= thr, relb_ref[kk, head_id] - far, val)
                bias_t[h, t] = jnp.where(dist >= 0, val, MASK_VALUE)

    qt = qt_ref[...]
    block_id = lax.broadcasted_iota(jnp.int32, (n_slots, blk), 0)
    past = block_id < i
    for h in range(2):
        off = other_off(h)
        qs = jnp.where(in_head(dim, h), qt * 0.125, jnp.zeros_like(qt))
        gate = _dot(kmean_hi[h], qs) + _dot(kmean_lo[h], qs)
        gate = jnp.where(past, gate, -jnp.inf)
        chosen = jnp.zeros(gate.shape, jnp.bool_)
        for _ in range(MOBA_TOPK):
            top = jnp.max(gate, axis=0, keepdims=True)
            first = jnp.min(jnp.where(gate == top, block_id, n_slots), axis=0, keepdims=True)
            hit = block_id == first
            chosen = chosen | (hit & (top > -jnp.inf))
            gate = jnp.where(hit, -jnp.inf, gate)
        qaug_t[h] = qs
        qaug_t[h, off:off + n_slots, :] = jnp.where(
            past & jnp.logical_not(chosen), MASK_VALUE, 0.0).astype(BF16)

    n_steps = i + 1
    slots = (logits_a, logits_b)

    def block_rows(t):
        j = jnp.maximum(i - t, 0)
        return pl.ds(pl.multiple_of(j * blk, blk), blk)

    def compute_logits(t, slot, bias_tile):
        for h in range(2):
            s = _dot(kaug[h, block_rows(t), :], qaug_t[h])
            if bias_tile is not None:
                s = s + bias_t[h, bias_tile]
            slots[slot][h] = s.astype(BF16)
            smax[slot, h] = jnp.max(s, axis=0, keepdims=True).astype(BF16).astype(F32)

    def softmax_update(t, slot):
        for h in range(2):
            s = slots[slot][h]
            m_old = mcol[h]
            m_new = jnp.maximum(m_old, smax[slot, h])
            p = jnp.exp(s - m_new.astype(BF16))
            acc_t[h] = (acc_t[h] * jnp.exp(m_old - m_new)
                        + _dot(vaug_t[h, :, block_rows(t)], p))
            mcol[h] = m_new

    mcol[...] = jnp.full(mcol.shape, -jnp.inf, F32)
    acc_t[...] = jnp.zeros(acc_t.shape, F32)
    compute_logits(0, 0, 0)
    compute_logits(1, 1, 1)

    def step_pair(pair, carry):
        t = 2 * pair
        softmax_update(t, 0)
        compute_logits(t + 2, 0, None)
        softmax_update(t + 1, 1)
        compute_logits(t + 3, 1, None)
        return carry

    lax.fori_loop(0, n_steps // 2, step_pair, 0)

    @pl.when(n_steps % 2 == 1)
    def _last_step():
        softmax_update(n_steps - 1, 0)

    a0 = acc_t[0]
    a1 = acc_t[1]
    o_t = jnp.concatenate([a0[:HEAD_DIM] / a0[HEAD_DIM:HEAD_DIM + 1],
                           a1[HEAD_DIM:] / a1[0:1]], axis=0)
    o_ref[...] = o_t.T.astype(o_ref.dtype)


def _attention(proj3d, qv_t, rel_bias):
    b, s, _ = proj3d.shape
    n_blocks = s // MOBA_BLOCK
    n_slots = HEAD_DIM // 2
    assert s % MOBA_BLOCK == 0 and n_blocks <= n_slots
    blk = MOBA_BLOCK
    scratch = [
        pltpu.VMEM((2, s, LANES), BF16),
        pltpu.VMEM((2, LANES, s), BF16),
        pltpu.VMEM((2, n_slots, LANES), F32),
        pltpu.VMEM((2, n_slots, LANES), BF16),
        pltpu.VMEM((2, n_slots, LANES), BF16),
        pltpu.VMEM((2, 2, blk, blk), F32),
        pltpu.VMEM((2, LANES, blk), BF16),
        pltpu.VMEM((2, LANES, blk), F32),
        pltpu.VMEM((2, 1, blk), F32),
        pltpu.VMEM((2, blk, blk), BF16),
        pltpu.VMEM((2, blk, blk), BF16),
        pltpu.VMEM((2, 2, 1, blk), F32),
    ]
    vmem = 2 * 2 * s * LANES * 2 + 2 * 2 * s * LANES * 2 + 4 * blk * blk * 4 + (4 << 20)
    return pl.pallas_call(
        _attn_kernel,
        grid=(b, HEAD_PAIRS, n_blocks),
        in_specs=[
            pl.BlockSpec(memory_space=pltpu.SMEM),
            pl.BlockSpec((None, LANES, blk), lambda bi, hp, i: (bi, hp, i)),
            pl.BlockSpec((None, s, LANES), lambda bi, hp, i: (bi, 0, COL_K * HEAD_PAIRS + hp)),
            pl.BlockSpec((None, LANES, s), lambda bi, hp, i: (bi, HEAD_PAIRS + hp, 0)),
        ],
        out_specs=pl.BlockSpec((None, blk, LANES), lambda bi, hp, i: (bi, i, hp)),
        out_shape=jax.ShapeDtypeStruct((b, s, ATTN_WIDTH), BF16),
        scratch_shapes=scratch,
        compiler_params=pltpu.CompilerParams(
            dimension_semantics=("parallel", "parallel", "arbitrary"),
            vmem_limit_bytes=_vmem_limit(vmem)),
        name="moba_attn",
    )(rel_bias, qv_t, proj3d, qv_t)


def _ssm_weights(a_re, a_im, log_dt, b_re, b_im, c_re, c_im, d_skip):
    g, p_states, c = b_re.shape
    big_l = SSM_CHUNK
    dt = jnp.exp(log_dt.astype(F32))[:, None]
    ar = a_re.astype(F32)
    ai = a_im.astype(F32)
    mag = jnp.exp(dt * ar)
    ang = dt * ai
    abar_re = mag * jnp.cos(ang)
    abar_im = mag * jnp.sin(ang)
    den = ar * ar + ai * ai
    nr = abar_re - 1.0
    ni = abar_im
    fr = (nr * ar + ni * ai) / den
    fi = (ni * ar - nr * ai) / den
    br = b_re.astype(F32)
    bi = b_im.astype(F32)
    bbar_re = fr[..., None] * br - fi[..., None] * bi
    bbar_im = fr[..., None] * bi + fi[..., None] * br
    n = jnp.arange(big_l + 1, dtype=F32)[:, None, None]
    pmag = jnp.exp(n * (dt * ar)[None])
    pw_re = pmag * jnp.cos(n * ang[None])
    pw_im = pmag * jnp.sin(n * ang[None])
    cr = c_re.astype(F32)
    ci = c_im.astype(F32)
    ab_re = pw_re[..., None] * bbar_re[None] - pw_im[..., None] * bbar_im[None]
    ab_im = pw_re[..., None] * bbar_im[None] + pw_im[..., None] * bbar_re[None]
    hi = lax.Precision.HIGHEST
    kern = (jnp.einsum('gdp,ngpc->ngcd', cr, ab_re[:big_l], precision=hi)
            - jnp.einsum('gdp,ngpc->ngcd', ci, ab_im[:big_l], precision=hi))
    lag = jnp.arange(big_l)[None, :] - jnp.arange(big_l)[:, None]
    toep = kern[jnp.clip(lag, 0, big_l - 1)]
    toep = jnp.where((lag >= 0)[:, :, None, None, None], toep, 0.0)
    toep = toep.transpose(2, 0, 3, 1, 4).reshape(g, big_l * c, big_l * c)
    skip = jnp.tile(d_skip.astype(F32).reshape(g, 1, c), (1, big_l, 1)).reshape(g, big_l * c)
    toep = toep + skip[:, :, None] * jnp.eye(big_l * c, dtype=F32)[None]
    e_re = ab_re[:big_l][::-1].transpose(1, 0, 3, 2).reshape(g, big_l * c, p_states)
    e_im = ab_im[:big_l][::-1].transpose(1, 0, 3, 2).reshape(g, big_l * c, p_states)
    w_in = jnp.concatenate([e_re, e_im, e_im, e_re], axis=-1)
    ca_re = cr[None] * pw_re[1:, :, None, :] - ci[None] * pw_im[1:, :, None, :]
    ca_im = cr[None] * pw_im[1:, :, None, :] + ci[None] * pw_re[1:, :, None, :]
    o_re = ca_re.transpose(1, 3, 0, 2).reshape(g, p_states, big_l * c)
    o_im = -ca_im.transpose(1, 3, 0, 2).reshape(g, p_states, big_l * c)
    w_out = jnp.concatenate([o_re, o_im], axis=1)
    are, aim = pw_re[big_l], pw_im[big_l]
    carry = jnp.stack([jnp.concatenate([are, are], -1),
                       jnp.concatenate([-aim, aim], -1),
                       jnp.concatenate([aim, -aim], -1)], axis=1)
    return toep.astype(BF16), w_in.astype(BF16), w_out.astype(BF16), carry


def _ssm_kernel(u_ref, toep_ref, win_ref, wout_ref, carry_ref, y_ref, e_sc, prev_sc, *, batch):
    n_chunks = u_ref.shape[0] // batch
    half = LANES
    u = u_ref[...]
    e = _dot(u, win_ref[...])
    e_sc[0] = e[:, :half]
    e_sc[1] = e[:, half:]
    a1 = jnp.broadcast_to(carry_ref[0:1, :], (batch, half))
    a2 = jnp.broadcast_to(carry_ref[1:2, :], (batch, half))
    a3 = jnp.broadcast_to(carry_ref[2:3, :], (batch, half))

    def step(kk, state):
        st, st_swapped = state
        rows = pl.ds(kk, batch, stride=n_chunks)
        prev_sc[rows, :] = st
        new = a1 * st + a2 * st_swapped + e_sc[0, rows, :]
        new_swapped = a1 * st_swapped + a3 * st + e_sc[1, rows, :]
        return new, new_swapped

    zero = jnp.zeros((batch, half), F32)
    lax.fori_loop(0, n_chunks, step, (zero, zero), unroll=8)
    y = _dot(u, toep_ref[...]) + _dot(prev_sc[...].astype(BF16), wout_ref[...])
    y_ref[...] = y.astype(y_ref.dtype)


def _ssm(u_t, toep, w_in, w_out, carry, *, batch):
    g, rows, width = u_t.shape
    vmem = 2 * 2 * rows * width * 2 + rows * width * 4 + rows * LANES * 4 + 3 * rows * width * 4
    return pl.pallas_call(
        functools.partial(_ssm_kernel, batch=batch),
        grid=(g,),
        in_specs=[
            pl.BlockSpec((None, rows, width), lambda gi: (gi, 0, 0)),
            pl.BlockSpec((None, width, width), lambda gi: (gi, 0, 0)),
            pl.BlockSpec((None, width, width), lambda gi: (gi, 0, 0)),
            pl.BlockSpec((None, LANES, width), lambda gi: (gi, 0, 0)),
            pl.BlockSpec((None, 3, LANES), lambda gi: (gi, 0, 0)),
        ],
        out_specs=pl.BlockSpec((None, rows, width), lambda gi: (gi, 0, 0)),
        out_shape=jax.ShapeDtypeStruct((g, rows, width), BF16),
        scratch_shapes=[pltpu.VMEM((2, rows, LANES), F32), pltpu.VMEM((rows, LANES), F32)],
        compiler_params=pltpu.CompilerParams(
            dimension_semantics=("parallel",), vmem_limit_bytes=_vmem_limit(vmem)),
        name="s5_ssm",
    )(u_t, toep, w_in, w_out, carry)


def _final_kernel(x_ref, p_ref, oa_ref, yf_ref, za_ref, zs_ref, ga_ref, gs_ref,
                  wap_ref, wglu_ref, wsp_ref, wout_ref, wpg_ref, wpp_ref, lng_ref, lnb_ref, o_ref, ys_sc):
    n_chunks = ys_sc.shape[1] // SSM_CHUNK
    groups_per_tile = LANES // SSM_GROUP
    folded = [yf_ref[g].astype(F32) for g in range(SSM_GROUPS)]
    for t in range(SSM_CHUNK):
        for gb in range(ys_sc.shape[0]):
            tile = jnp.concatenate(
                [folded[gb * groups_per_tile + gl][:, t * SSM_GROUP:(t + 1) * SSM_GROUP]
                 for gl in range(groups_per_tile)], axis=1)
            ys_sc[gb, pl.ds(t, n_chunks, stride=SSM_CHUNK), :] = tile
    x = x_ref[...]
    za = za_ref[...].astype(F32)
    a_in = oa_ref[...].astype(F32) * (za * _sigmoid(za))
    y_a = _dot(a_in.astype(BF16), wap_ref[...])
    ys = jnp.concatenate([ys_sc[gb] for gb in range(ys_sc.shape[0])], axis=1)
    gelu = 0.5 * ys * (1.0 + lax.erf(ys * (2.0 ** -0.5)))
    glu = _dot(gelu.astype(BF16), wglu_ref[...])
    zs = zs_ref[...].astype(F32)
    s_in = glu[:, :SSM_WIDTH] * _sigmoid(glu[:, SSM_WIDTH:]) * (zs * _sigmoid(zs))
    y_s = _dot(s_in.astype(BF16), wsp_ref[...])
    merge = _sigmoid(ga_ref[...].astype(F32)) * y_a + _sigmoid(gs_ref[...].astype(F32)) * y_s
    mix = _dot(merge.astype(BF16), wout_ref[...])
    ple = _sigmoid(_dot(x.astype(BF16), wpg_ref[...])) * _dot(p_ref[...].astype(BF16), wpp_ref[...])
    hsum = DEEPNORM_ALPHA * x + mix + ple
    mu = jnp.mean(hsum, axis=-1, keepdims=True)
    cen = hsum - mu
    var = jnp.mean(cen * cen, axis=-1, keepdims=True)
    o_ref[...] = cen * lax.rsqrt(var + LN_EPS) * lng_ref[...] + lnb_ref[...]


def _final(x2d, p2d, proj, o_a, y_fold, w_ap, w_glu, w_sp, w_out, w_pg, w_pp, ln_g, ln_b, *, row_tile=256):
    m = x2d.shape[0]
    half, full = SSM_WIDTH, D_MODEL

    def rows(width, col):
        return pl.BlockSpec((row_tile, width), lambda i: (i, col))

    def whole(arr):
        return pl.BlockSpec(arr.shape, lambda i: (0, 0))

    weights = (w_ap, w_glu, w_sp, w_out, w_pg, w_pp, ln_g, ln_b)
    vmem = (2 * sum(int(np.prod(w.shape)) * w.dtype.itemsize for w in weights)
            + 2 * row_tile * (2 * full * 4 + PLE_DIM * 4 + (4 * half + 2 * full) * 2)
            + 12 * row_tile * full * 4)
    return pl.pallas_call(
        _final_kernel,
        grid=(m // row_tile,),
        in_specs=[rows(full, 0), rows(PLE_DIM, 0), rows(half, 0),
                  pl.BlockSpec((SSM_GROUPS, row_tile // SSM_CHUNK, SSM_CHUNK * SSM_GROUP), lambda i: (0, i, 0)),
                  rows(half, COL_ZA), rows(half, COL_ZS), rows(full, COL_GA), rows(full, COL_GS)]
                 + [whole(w) for w in weights],
        out_specs=rows(full, 0),
        out_shape=jax.ShapeDtypeStruct((m, full), F32),
        scratch_shapes=[pltpu.VMEM((half // LANES, row_tile, LANES), F32)],
        compiler_params=pltpu.CompilerParams(
            dimension_semantics=("parallel",), vmem_limit_bytes=_vmem_limit(vmem)),
        name="final",
    )(x2d, p2d, o_a, y_fold, proj, proj, proj, proj, *weights)


def kernel(x, p, w_in, w_attn_proj, w_ssm_proj, w_out, ssm_a_re, ssm_a_im, ssm_log_dt, ssm_b_re, ssm_b_im, ssm_c_re, ssm_c_im, ssm_d, w_glu, w_ple_gate, w_ple_proj, ln_g, ln_b, rel_bias):
    b, s, d = x.shape
    m = b * s
    n_chunks = s // SSM_CHUNK
    for i in range(w_in.shape[0]):
        x2d = x.reshape(m, d)
        wq, wk, wv, wza, wu, wzs, wga, wgs = jnp.split(w_in[i].astype(BF16), _IN_SPLITS, axis=1)
        w_main = jnp.concatenate([wga, wgs, wk, wza, wzs], axis=1)
        proj, qv_t, u_fold = _proj(x2d, w_main, jnp.concatenate([wq, wv], axis=1).T, wu, batch=b)
        o_a = _attention(proj.reshape(b, s, PROJ_WIDTH), qv_t, rel_bias.astype(F32))
        toep, s_in, s_out, carry = _ssm_weights(
            ssm_a_re[i], ssm_a_im[i], ssm_log_dt[i], ssm_b_re[i], ssm_b_im[i],
            ssm_c_re[i], ssm_c_im[i], ssm_d[i].reshape(SSM_GROUPS, SSM_GROUP))
        y_fold = _ssm(u_fold, toep, s_in, s_out, carry, batch=b)
        x2d = _final(x2d, p[i].reshape(m, PLE_DIM), proj, o_a.reshape(m, ATTN_WIDTH), y_fold,
                     w_attn_proj[i].astype(BF16), w_glu[i].astype(BF16), w_ssm_proj[i].astype(BF16),
                     w_out[i].astype(BF16), w_ple_gate[i].astype(BF16), w_ple_proj[i].astype(BF16),
                     ln_g[i].astype(F32).reshape(1, d), ln_b[i].astype(F32).reshape(1, d))
        x = x2d.reshape(b, s, d)
    return x
```

```python
import functools
import math

import numpy as np
import jax
import jax.numpy as jnp
from jax import lax
from jax.experimental import pallas as pl
from jax.experimental.pallas import tpu as pltpu

F32 = jnp.float32
BF16 = jnp.bfloat16

LANES = 128
SUBLANES = 8
V7X_VMEM_BYTES = 64 * 1024 * 1024

D_MODEL = 1024
PLE_DIM = 256
HEADS = 8
HEAD_DIM = 64
ATTN_WIDTH = HEADS * HEAD_DIM
HEAD_PAIRS = ATTN_WIDTH // LANES
MOBA_BLOCK = 256
MOBA_TOPK = 3
REL_BUCKETS = 32
REL_MAX_DIST = 128
SSM_WIDTH = 512
SSM_GROUP = 16
SSM_GROUPS = SSM_WIDTH // SSM_GROUP
SSM_STATE = 64
SSM_CHUNK = 16
IN_WIDTH = 4 * ATTN_WIDTH + 2 * SSM_WIDTH + 2 * D_MODEL
DEPTH = 1
DEEPNORM_ALPHA = (2.0 * DEPTH) ** 0.25
LN_EPS = 1e-5
MASK_VALUE = -1e30

PROJ_WIDTH = IN_WIDTH - 2 * ATTN_WIDTH - SSM_WIDTH
_IN_SPLITS = tuple(int(v) for v in np.cumsum(
    (ATTN_WIDTH,) * 4 + (SSM_WIDTH,) * 2 + (D_MODEL,) * 2)[:-1])
COL_GA, COL_GS = 0, 1
COL_K, COL_ZA, COL_ZS = 4, 5, 6


def _dot(a, b):
    return jnp.dot(a, b, preferred_element_type=F32)


def _dot_nt(a, b):
    return lax.dot_general(a, b, (((1,), (1,)), ((), ())), preferred_element_type=F32)


def _sigmoid(v):
    return 1.0 / (1.0 + jnp.exp(-v))


def _vmem_limit(nbytes):
    return int(min(V7X_VMEM_BYTES - (4 << 20), max(nbytes + (8 << 20), 32 << 20)))


def _proj_kernel(x_ref, w_ref, wt_ref, wu_ref, o_ref, t_ref, uf_ref, u_sc, *, col_tile):
    xb = x_ref[...].astype(BF16)
    u = _dot(xb, wu_ref[...])
    for gb in range(u_sc.shape[0]):
        u_sc[gb] = u[:, gb * LANES:(gb + 1) * LANES]
    n_cols = o_ref.shape[1]
    for start in range(0, n_cols, col_tile):
        cols = slice(start, min(start + col_tile, n_cols))
        o_ref[:, cols] = _dot(xb, w_ref[:, cols]).astype(BF16)
    t_ref[...] = _dot_nt(wt_ref[...], xb).astype(BF16)
    n_chunks = u_sc.shape[1] // SSM_CHUNK
    groups_per_tile = LANES // SSM_GROUP
    steps = [[u_sc[gb, pl.ds(t, n_chunks, stride=SSM_CHUNK), :]
              for gb in range(u_sc.shape[0])] for t in range(SSM_CHUNK)]
    for g in range(SSM_GROUPS):
        gb, lo = g // groups_per_tile, (g % groups_per_tile) * SSM_GROUP
        folded = jnp.concatenate([steps[t][gb][:, lo:lo + SSM_GROUP] for t in range(SSM_CHUNK)], axis=1)
        uf_ref[g] = folded.astype(BF16)


def _proj(x2d, w_bf16, wt_bf16, wu_bf16, *, batch, row_tile=512, col_tile=1024):
    m, k = x2d.shape
    n = w_bf16.shape[1]
    nt = wt_bf16.shape[0]
    nu = wu_bf16.shape[1]
    tiles_per_batch = m // batch // row_tile
    fold_rows = row_tile // SSM_CHUNK
    vmem = (2 * row_tile * k * 4 + 2 * k * (n + nt + nu) * 2 + 2 * row_tile * (n + nt + nu) * 2
            + row_tile * k * 2 + 2 * row_tile * col_tile * 4 + row_tile * nu * 4)
    return pl.pallas_call(
        functools.partial(_proj_kernel, col_tile=col_tile),
        grid=(m // row_tile,),
        in_specs=[pl.BlockSpec((row_tile, k), lambda i: (i, 0)),
                  pl.BlockSpec((k, n), lambda i: (0, 0)),
                  pl.BlockSpec((nt, k), lambda i: (0, 0)),
                  pl.BlockSpec((k, nu), lambda i: (0, 0))],
        out_specs=[pl.BlockSpec((row_tile, n), lambda i: (i, 0)),
                   pl.BlockSpec((None, nt, row_tile),
                                lambda i: (i // tiles_per_batch, 0, i % tiles_per_batch)),
                   pl.BlockSpec((SSM_GROUPS, fold_rows, SSM_CHUNK * SSM_GROUP), lambda i: (0, i, 0))],
        out_shape=[jax.ShapeDtypeStruct((m, n), BF16),
                   jax.ShapeDtypeStruct((batch, nt, m // batch), BF16),
                   jax.ShapeDtypeStruct((SSM_GROUPS, m // SSM_CHUNK, SSM_CHUNK * SSM_GROUP), BF16)],
        scratch_shapes=[pltpu.VMEM((nu // LANES, row_tile, LANES), F32)],
        compiler_params=pltpu.CompilerParams(
            dimension_semantics=("parallel",), vmem_limit_bytes=_vmem_limit(vmem)),
        name="proj",
    )(x2d, w_bf16, wt_bf16, wu_bf16)


def _t5_bucket_thresholds():
    max_exact = REL_BUCKETS // 2
    dist = np.arange(0, 2 * MOBA_BLOCK, dtype=np.int32)
    d = np.maximum(dist, 1).astype(np.float32)
    large = max_exact + (np.log(d / np.float32(max_exact)) / np.float32(math.log(REL_MAX_DIST / max_exact))
                         * np.float32(REL_BUCKETS - max_exact)).astype(np.int32)
    large = np.minimum(large, REL_BUCKETS - 1)
    bucket = np.where(dist < max_exact, dist, large)
    assert np.all(np.diff(bucket) >= 0) and bucket[-1] == REL_BUCKETS - 1
    return [int(np.argmax(bucket >= k)) for k in range(1, REL_BUCKETS)]


_BUCKET_THRESHOLDS = _t5_bucket_thresholds()


def _attn_kernel(relb_ref, qt_ref, k_ref, vt_ref, o_ref,
                 kaug, vaug_t, kmean, kmean_hi, kmean_lo, bias_t, qaug_t, acc_t, mcol, logits_a, logits_b, smax):
    hp = pl.program_id(1)
    blk = MOBA_BLOCK
    n_blocks = k_ref.shape[0] // blk
    n_slots = kmean.shape[1]
    lane = lax.broadcasted_iota(jnp.int32, (blk, LANES), 1)
    dim = lax.broadcasted_iota(jnp.int32, (LANES, blk), 0)

    def in_head(index, h):
        return (index >= HEAD_DIM * h) & (index < HEAD_DIM * (h + 1))

    def other_off(h):
        return HEAD_DIM * (1 - h)

    def _prepare():
        kmean[...] = jnp.zeros(kmean.shape, F32)
        for h in range(2):
            head = in_head(lane, h)
            off = other_off(h)

            def build(j, carry, h=h, head=head, off=off):
                rows = pl.ds(pl.multiple_of(j * blk, blk), blk)
                kb = k_ref[rows, :]
                onehot = jnp.where(lane - off == j, 1.0, 0.0).astype(BF16)
                kaug[h, rows, :] = jnp.where(head, kb, onehot)
                vaug_t[h, :HEAD_DIM, rows] = vt_ref[HEAD_DIM * h:HEAD_DIM * (h + 1), rows]
                vaug_t[h, HEAD_DIM:, rows] = jnp.ones((vaug_t.shape[1] - HEAD_DIM, blk), BF16)
                mean = jnp.sum(kb.astype(F32), axis=0, keepdims=True) * (1.0 / blk)
                kmean[h, pl.ds(j, 1), :] = jnp.where(head[:1], mean, 0.0)
                return carry

            lax.fori_loop(0, n_blocks, build, 0)
            km = kmean[h]
            hi = km.astype(BF16)
            kmean_hi[h] = hi
            kmean_lo[h] = (km - hi.astype(F32)).astype(BF16)

            head_id = 2 * hp + h
            far = relb_ref[REL_BUCKETS - 1, head_id]
            key = lax.broadcasted_iota(jnp.int32, (blk, blk), 0)
            qry = lax.broadcasted_iota(jnp.int32, (blk, blk), 1)
            for t, base in enumerate((0, blk)):
                dist = base + qry - key
                val = jnp.full((blk, blk), relb_ref[0, head_id] - far, F32)
                for kk, thr in enumerate(_BUCKET_THRESHOLDS, start=1):
                    val = jnp.where(dist >= thr, relb_ref[kk, head_id] - far, val)
                bias_t[h, t] = jnp.where(dist >= 0, val, MASK_VALUE)

    _prepare()
    block_id = lax.broadcasted_iota(jnp.int32, (n_slots, blk), 0)
    slots = (logits_a, logits_b)

    def query_block(i, carry):
        queries = pl.ds(pl.multiple_of(i * blk, blk), blk)
        qt = qt_ref[:, queries]
        past = block_id < i
        for h in range(2):
            off = other_off(h)
            qs = jnp.where(in_head(dim, h), qt * 0.125, jnp.zeros_like(qt))
            gate = _dot(kmean_hi[h], qs) + _dot(kmean_lo[h], qs)
            gate = jnp.where(past, gate, -jnp.inf)
            chosen = jnp.zeros(gate.shape, jnp.bool_)
            for _ in range(MOBA_TOPK):
                top = jnp.max(gate, axis=0, keepdims=True)
                first = jnp.min(jnp.where(gate == top, block_id, n_slots), axis=0, keepdims=True)
                hit = block_id == first
                chosen = chosen | (hit & (top > -jnp.inf))
                gate = jnp.where(hit, -jnp.inf, gate)
            qaug_t[h] = qs
            qaug_t[h, off:off + n_slots, :] = jnp.where(
                past & jnp.logical_not(chosen), MASK_VALUE, 0.0).astype(BF16)

        n_steps = i + 1
        n_pairs = n_steps // 2
        odd = n_steps % 2 == 1

        def block_rows(t):
            j = jnp.maximum(i - t, 0)
            return pl.ds(pl.multiple_of(j * blk, blk), blk)

        def compute_logits(t, slot, bias_tile):
            for h in range(2):
                s = _dot(kaug[h, block_rows(t), :], qaug_t[h])
                if bias_tile is not None:
                    s = s + bias_t[h, bias_tile]
                slots[slot][h] = s.astype(BF16)
                smax[slot, h] = jnp.max(s, axis=0, keepdims=True).astype(BF16).astype(F32)

        def softmax_update(t, slot):
            for h in range(2):
                s = slots[slot][h]
                m_old = mcol[h]
                m_new = jnp.maximum(m_old, smax[slot, h])
                p = jnp.exp(s - m_new.astype(BF16))
                acc_t[h] = (acc_t[h] * jnp.exp(m_old - m_new)
                            + _dot(vaug_t[h, :, block_rows(t)], p))
                mcol[h] = m_new

        mcol[...] = jnp.full(mcol.shape, -jnp.inf, F32)
        acc_t[...] = jnp.zeros(acc_t.shape, F32)
        compute_logits(0, 0, 0)
        compute_logits(1, 1, 1)

        def step_pair(pair, carry):
            t = 2 * pair
            softmax_update(t, 0)
            compute_logits(t + 2, 0, None)
            softmax_update(t + 1, 1)
            compute_logits(t + 3, 1, None)
            return carry

        lax.fori_loop(0, n_pairs - 1, step_pair, 0)
        t_last = 2 * (n_pairs - 1)

        @pl.when((n_pairs >= 1) & jnp.logical_not(odd))
        def _last_pair():
            softmax_update(t_last, 0)
            softmax_update(t_last + 1, 1)

        @pl.when((n_pairs >= 1) & odd)
        def _last_pair_and_step():
            softmax_update(t_last, 0)
            compute_logits(t_last + 2, 0, None)
            softmax_update(t_last + 1, 1)
            softmax_update(t_last + 2, 0)

        @pl.when(n_pairs == 0)
        def _only_step():
            softmax_update(0, 0)

        a0 = acc_t[0]
        a1 = acc_t[1]
        o_t = jnp.concatenate([a0[:HEAD_DIM] / a0[HEAD_DIM:HEAD_DIM + 1],
                               a1[:HEAD_DIM] / a1[HEAD_DIM:HEAD_DIM + 1]], axis=0)
        o_ref[queries, :] = o_t.T.astype(o_ref.dtype)
        return carry

    lax.fori_loop(0, n_blocks, query_block, 0)


def _attention(proj3d, qv_t, rel_bias):
    b, s, _ = proj3d.shape
    n_blocks = s // MOBA_BLOCK
    n_slots = HEAD_DIM // 2
    assert s % MOBA_BLOCK == 0 and n_blocks <= n_slots
    blk = MOBA_BLOCK
    pv_rows = HEAD_DIM + 16
    scratch = [
        pltpu.VMEM((2, s, LANES), BF16),
        pltpu.VMEM((2, pv_rows, s), BF16),
        pltpu.VMEM((2, n_slots, LANES), F32),
        pltpu.VMEM((2, n_slots, LANES), BF16),
        pltpu.VMEM((2, n_slots, LANES), BF16),
        pltpu.VMEM((2, 2, blk, blk), F32),
        pltpu.VMEM((2, LANES, blk), BF16),
        pltpu.VMEM((2, pv_rows, blk), F32),
        pltpu.VMEM((2, 1, blk), F32),
        pltpu.VMEM((2, blk, blk), BF16),
        pltpu.VMEM((2, blk, blk), BF16),
        pltpu.VMEM((2, 2, 1, blk), F32),
    ]
    vmem = (2 * s * LANES * 2 + 2 * pv_rows * s * 2
            + 2 * 4 * s * LANES * 2
            + 4 * blk * blk * 4 + 4 * blk * blk * 2 + (2 << 20))
    return pl.pallas_call(
        _attn_kernel,
        grid=(b, HEAD_PAIRS),
        in_specs=[
            pl.BlockSpec(memory_space=pltpu.SMEM),
            pl.BlockSpec((None, LANES, s), lambda bi, hp: (bi, hp, 0)),
            pl.BlockSpec((None, s, LANES), lambda bi, hp: (bi, 0, COL_K * HEAD_PAIRS + hp)),
            pl.BlockSpec((None, LANES, s), lambda bi, hp: (bi, HEAD_PAIRS + hp, 0)),
        ],
        out_specs=pl.BlockSpec((None, s, LANES), lambda bi, hp: (bi, 0, hp)),
        out_shape=jax.ShapeDtypeStruct((b, s, ATTN_WIDTH), BF16),
        scratch_shapes=scratch,
        compiler_params=pltpu.CompilerParams(
            dimension_semantics=("parallel", "parallel"),
            vmem_limit_bytes=_vmem_limit(vmem)),
        name="moba_attn",
    )(rel_bias, qv_t, proj3d, qv_t)


def _ssm_weights(a_re, a_im, log_dt, b_re, b_im, c_re, c_im, d_skip):
    g, p_states, c = b_re.shape
    big_l = SSM_CHUNK
    dt = jnp.exp(log_dt.astype(F32))[:, None]
    ar = a_re.astype(F32)
    ai = a_im.astype(F32)
    mag = jnp.exp(dt * ar)
    ang = dt * ai
    abar_re = mag * jnp.cos(ang)
    abar_im = mag * jnp.sin(ang)
    den = ar * ar + ai * ai
    nr = abar_re - 1.0
    ni = abar_im
    fr = (nr * ar + ni * ai) / den
    fi = (ni * ar - nr * ai) / den
    br = b_re.astype(F32)
    bi = b_im.astype(F32)
    bbar_re = fr[..., None] * br - fi[..., None] * bi
    bbar_im = fr[..., None] * bi + fi[..., None] * br
    n = jnp.arange(big_l + 1, dtype=F32)[:, None, None]
    pmag = jnp.exp(n * (dt * ar)[None])
    pw_re = pmag * jnp.cos(n * ang[None])
    pw_im = pmag * jnp.sin(n * ang[None])
    cr = c_re.astype(F32)
    ci = c_im.astype(F32)
    ab_re = pw_re[..., None] * bbar_re[None] - pw_im[..., None] * bbar_im[None]
    ab_im = pw_re[..., None] * bbar_im[None] + pw_im[..., None] * bbar_re[None]
    hi = lax.Precision.HIGHEST
    kern = (jnp.einsum('gdp,ngpc->ngcd', cr, ab_re[:big_l], precision=hi)
            - jnp.einsum('gdp,ngpc->ngcd', ci, ab_im[:big_l], precision=hi))
    lag = jnp.arange(big_l)[None, :] - jnp.arange(big_l)[:, None]
    toep = kern[jnp.clip(lag, 0, big_l - 1)]
    toep = jnp.where((lag >= 0)[:, :, None, None, None], toep, 0.0)
    toep = toep.transpose(2, 0, 3, 1, 4).reshape(g, big_l * c, big_l * c)
    skip = jnp.tile(d_skip.astype(F32).reshape(g, 1, c), (1, big_l, 1)).reshape(g, big_l * c)
    toep = toep + skip[:, :, None] * jnp.eye(big_l * c, dtype=F32)[None]
    e_re = ab_re[:big_l][::-1].transpose(1, 0, 3, 2).reshape(g, big_l * c, p_states)
    e_im = ab_im[:big_l][::-1].transpose(1, 0, 3, 2).reshape(g, big_l * c, p_states)
    w_in = jnp.concatenate([e_re, e_im, e_im, e_re], axis=-1)
    ca_re = cr[None] * pw_re[1:, :, None, :] - ci[None] * pw_im[1:, :, None, :]
    ca_im = cr[None] * pw_im[1:, :, None, :] + ci[None] * pw_re[1:, :, None, :]
    o_re = ca_re.transpose(1, 3, 0, 2).reshape(g, p_states, big_l * c)
    o_im = -ca_im.transpose(1, 3, 0, 2).reshape(g, p_states, big_l * c)
    w_out = jnp.concatenate([o_re, o_im], axis=1)
    are, aim = pw_re[big_l], pw_im[big_l]
    carry = jnp.stack([jnp.concatenate([are, are], -1),
                       jnp.concatenate([-aim, aim], -1),
                       jnp.concatenate([aim, -aim], -1)], axis=1)
    return toep.astype(BF16), w_in.astype(BF16), w_out.astype(BF16), carry


def _ssm_kernel(u_ref, toep_ref, win_ref, wout_ref, carry_ref, y_ref, e_sc, prev_sc, *, batch):
    n_chunks = u_ref.shape[0] // batch
    half = LANES
    u = u_ref[...]
    e = _dot(u, win_ref[...])
    e_sc[0] = e[:, :half]
    e_sc[1] = e[:, half:]
    a1 = jnp.broadcast_to(carry_ref[0:1, :], (batch, half))
    a2 = jnp.broadcast_to(carry_ref[1:2, :], (batch, half))
    a3 = jnp.broadcast_to(carry_ref[2:3, :], (batch, half))

    def step(kk, state):
        st, st_swapped = state
        rows = pl.ds(kk, batch, stride=n_chunks)
        prev_sc[rows, :] = st
        new = a1 * st + a2 * st_swapped + e_sc[0, rows, :]
        new_swapped = a1 * st_swapped + a3 * st + e_sc[1, rows, :]
        return new, new_swapped

    zero = jnp.zeros((batch, half), F32)
    lax.fori_loop(0, n_chunks, step, (zero, zero), unroll=8)
    y = _dot(u, toep_ref[...]) + _dot(prev_sc[...].astype(BF16), wout_ref[...])
    y_ref[...] = y.astype(y_ref.dtype)


def _ssm(u_t, toep, w_in, w_out, carry, *, batch):
    g, rows, width = u_t.shape
    vmem = 2 * 2 * rows * width * 2 + rows * width * 4 + rows * LANES * 4 + 3 * rows * width * 4
    return pl.pallas_call(
        functools.partial(_ssm_kernel, batch=batch),
        grid=(g,),
        in_specs=[
            pl.BlockSpec((None, rows, width), lambda gi: (gi, 0, 0)),
            pl.BlockSpec((None, width, width), lambda gi: (gi, 0, 0)),
            pl.BlockSpec((None, width, width), lambda gi: (gi, 0, 0)),
            pl.BlockSpec((None, LANES, width), lambda gi: (gi, 0, 0)),
            pl.BlockSpec((None, 3, LANES), lambda gi: (gi, 0, 0)),
        ],
        out_specs=pl.BlockSpec((None, rows, width), lambda gi: (gi, 0, 0)),
        out_shape=jax.ShapeDtypeStruct((g, rows, width), BF16),
        scratch_shapes=[pltpu.VMEM((2, rows, LANES), F32), pltpu.VMEM((rows, LANES), F32)],
        compiler_params=pltpu.CompilerParams(
            dimension_semantics=("parallel",), vmem_limit_bytes=_vmem_limit(vmem)),
        name="s5_ssm",
    )(u_t, toep, w_in, w_out, carry)


def _final_kernel(x_ref, p_ref, oa_ref, yf_ref, za_ref, zs_ref, ga_ref, gs_ref,
                  wap_ref, wglu_ref, wsp_ref, wout_ref, wpg_ref, wpp_ref, lng_ref, lnb_ref, o_ref, ys_sc,
                  *, sub_rows):
    for sub in range(x_ref.shape[0] // sub_rows):
        _final_rows(sub, sub_rows, x_ref, p_ref, oa_ref, yf_ref, za_ref, zs_ref, ga_ref, gs_ref,
                    wap_ref, wglu_ref, wsp_ref, wout_ref, wpg_ref, wpp_ref, lng_ref, lnb_ref, o_ref, ys_sc)


def _final_rows(sub, sub_rows, x_ref, p_ref, oa_ref, yf_ref, za_ref, zs_ref, ga_ref, gs_ref,
                wap_ref, wglu_ref, wsp_ref, wout_ref, wpg_ref, wpp_ref, lng_ref, lnb_ref, o_ref, ys_sc):
    rows = slice(sub * sub_rows, (sub + 1) * sub_rows)
    n_chunks = sub_rows // SSM_CHUNK
    chunks = slice(sub * n_chunks, (sub + 1) * n_chunks)
    groups_per_tile = LANES // SSM_GROUP
    folded = [yf_ref[g, chunks, :].astype(F32) for g in range(SSM_GROUPS)]
    for t in range(SSM_CHUNK):
        for gb in range(ys_sc.shape[1]):
            tile = jnp.concatenate(
                [folded[gb * groups_per_tile + gl][:, t * SSM_GROUP:(t + 1) * SSM_GROUP]
                 for gl in range(groups_per_tile)], axis=1)
            ys_sc[sub, gb, pl.ds(t, n_chunks, stride=SSM_CHUNK), :] = tile
    x = x_ref[rows, :]
    za = za_ref[rows, :]
    a_in = oa_ref[rows, :] * (za * _sigmoid(za))
    y_a = _dot(a_in, wap_ref[...])
    ys = jnp.concatenate([ys_sc[sub, gb] for gb in range(ys_sc.shape[1])], axis=1)
    gelu = 0.5 * ys * (1.0 + lax.erf(ys * (2.0 ** -0.5)))
    glu = _dot(gelu.astype(BF16), wglu_ref[...])
    zs = zs_ref[rows, :]
    s_in = glu[:, :SSM_WIDTH] * _sigmoid(glu[:, SSM_WIDTH:]) * (zs * _sigmoid(zs)).astype(F32)
    y_s = _dot(s_in.astype(BF16), wsp_ref[...])
    merge = _sigmoid(ga_ref[rows, :]) * y_a.astype(BF16) + _sigmoid(gs_ref[rows, :]) * y_s.astype(BF16)
    mix = _dot(merge, wout_ref[...])
    ple = (_sigmoid(_dot(x.astype(BF16), wpg_ref[...]))
           * _dot(p_ref[rows, :].astype(BF16), wpp_ref[...]))
    hsum = DEEPNORM_ALPHA * x + mix + ple
    mu = jnp.mean(hsum, axis=-1, keepdims=True)
    cen = hsum - mu
    var = jnp.mean(cen * cen, axis=-1, keepdims=True)
    o_ref[rows, :] = cen * lax.rsqrt(var + LN_EPS) * lng_ref[...] + lnb_ref[...]


def _final(x2d, p2d, proj, o_a, y_fold, w_ap, w_glu, w_sp, w_out, w_pg, w_pp, ln_g, ln_b, *,
           row_tile=512, sub_rows=256):
    m = x2d.shape[0]
    half, full = SSM_WIDTH, D_MODEL

    def rows(width, col):
        return pl.BlockSpec((row_tile, width), lambda i: (i, col))

    def whole(arr):
        return pl.BlockSpec(arr.shape, lambda i: (0, 0))

    weights = (w_ap, w_glu, w_sp, w_out, w_pg, w_pp, ln_g, ln_b)
    vmem = (2 * sum(int(np.prod(w.shape)) * w.dtype.itemsize for w in weights)
            + 2 * row_tile * (2 * full * 4 + PLE_DIM * 4 + (4 * half + 2 * full) * 2)
            + 12 * row_tile * full * 4)
    return pl.pallas_call(
        functools.partial(_final_kernel, sub_rows=sub_rows),
        grid=(m // row_tile,),
        in_specs=[rows(full, 0), rows(PLE_DIM, 0), rows(half, 0),
                  pl.BlockSpec((SSM_GROUPS, row_tile // SSM_CHUNK, SSM_CHUNK * SSM_GROUP), lambda i: (0, i, 0)),
                  rows(half, COL_ZA), rows(half, COL_ZS), rows(full, COL_GA), rows(full, COL_GS)]
                 + [whole(w) for w in weights],
        out_specs=rows(full, 0),
        out_shape=jax.ShapeDtypeStruct((m, full), F32),
        scratch_shapes=[pltpu.VMEM((row_tile // sub_rows, half // LANES, sub_rows, LANES), F32)],
        compiler_params=pltpu.CompilerParams(
            dimension_semantics=("parallel",), vmem_limit_bytes=_vmem_limit(vmem)),
        name="final",
    )(x2d, p2d, o_a, y_fold, proj, proj, proj, proj, *weights)


def kernel(x, p, w_in, w_attn_proj, w_ssm_proj, w_out, ssm_a_re, ssm_a_im, ssm_log_dt, ssm_b_re, ssm_b_im, ssm_c_re, ssm_c_im, ssm_d, w_glu, w_ple_gate, w_ple_proj, ln_g, ln_b, rel_bias):
    b, s, d = x.shape
    m = b * s
    n_chunks = s // SSM_CHUNK
    for i in range(w_in.shape[0]):
        x2d = x.reshape(m, d)
        wq, wk, wv, wza, wu, wzs, wga, wgs = jnp.split(w_in[i].astype(BF16), _IN_SPLITS, axis=1)
        w_main = jnp.concatenate([wga, wgs, wk, wza, wzs], axis=1)
        proj, qv_t, u_fold = _proj(x2d, w_main, jnp.concatenate([wq, wv], axis=1).T, wu, batch=b)
        o_a = _attention(proj.reshape(b, s, PROJ_WIDTH), qv_t, rel_bias.astype(F32))
        toep, s_in, s_out, carry = _ssm_weights(
            ssm_a_re[i], ssm_a_im[i], ssm_log_dt[i], ssm_b_re[i], ssm_b_im[i],
            ssm_c_re[i], ssm_c_im[i], ssm_d[i].reshape(SSM_GROUPS, SSM_GROUP))
        y_fold = _ssm(u_fold, toep, s_in, s_out, carry, batch=b)
        x2d = _final(x2d, p[i].reshape(m, PLE_DIM), proj, o_a.reshape(m, ATTN_WIDTH), y_fold,
                     w_attn_proj[i].astype(BF16), w_glu[i].astype(BF16), w_ssm_proj[i].astype(BF16),
                     w_out[i].astype(BF16), w_ple_gate[i].astype(BF16), w_ple_proj[i].astype(BF16),
                     ln_g[i].astype(F32).reshape(1, d), ln_b[i].astype(F32).reshape(1, d))
        x = x2d.reshape(b, s, d)
    return x
```

```python
import functools
import math

import numpy as np
import jax
import jax.numpy as jnp
from jax import lax
from jax.experimental import pallas as pl
from jax.experimental.pallas import tpu as pltpu

F32 = jnp.float32
BF16 = jnp.bfloat16

LANES = 128
V7X_VMEM_BYTES = 64 * 1024 * 1024

D_MODEL = 1024
PLE_DIM = 256
HEADS = 8
HEAD_DIM = 64
ATTN_WIDTH = HEADS * HEAD_DIM
HEAD_PAIRS = ATTN_WIDTH // LANES
MOBA_BLOCK = 256
MOBA_TOPK = 3
REL_BUCKETS = 32
REL_MAX_DIST = 128
SSM_WIDTH = 512
SSM_GROUP = 16
SSM_GROUPS = SSM_WIDTH // SSM_GROUP
SSM_STATE = 64
SSM_CHUNK = 16
IN_WIDTH = 4 * ATTN_WIDTH + 2 * SSM_WIDTH + 2 * D_MODEL
DEPTH = 1
DEEPNORM_ALPHA = (2.0 * DEPTH) ** 0.25
LN_EPS = 1e-5
MASK_VALUE = -1e30

PROJ_WIDTH = IN_WIDTH - 2 * ATTN_WIDTH - SSM_WIDTH
_IN_SPLITS = tuple(int(v) for v in np.cumsum(
    (ATTN_WIDTH,) * 4 + (SSM_WIDTH,) * 2 + (D_MODEL,) * 2)[:-1])
COL_GA, COL_GS = 0, 1
COL_K, COL_ZA, COL_ZS = 4, 5, 6


def _dot(a, b):
    return jnp.dot(a, b, preferred_element_type=F32)


def _dot_nt(a, b):
    return lax.dot_general(a, b, (((1,), (1,)), ((), ())), preferred_element_type=F32)


def _sigmoid(v):
    return 1.0 / (1.0 + jnp.exp(-v))


def _vmem_limit(nbytes):
    return int(min(V7X_VMEM_BYTES - (4 << 20), max(nbytes + (8 << 20), 32 << 20)))


def _proj_kernel(x_ref, w_ref, wt_ref, wu_ref, o_ref, t_ref, uf_ref, u_sc, *, col_tile):
    xb = x_ref[...].astype(BF16)
    u = _dot(xb, wu_ref[...])
    for gb in range(u_sc.shape[0]):
        u_sc[gb] = u[:, gb * LANES:(gb + 1) * LANES]
    n_cols = o_ref.shape[1]
    for start in range(0, n_cols, col_tile):
        cols = slice(start, min(start + col_tile, n_cols))
        o_ref[:, cols] = _dot(xb, w_ref[:, cols]).astype(BF16)
    t_ref[...] = _dot_nt(wt_ref[...], xb).astype(BF16)
    n_chunks = u_sc.shape[1] // SSM_CHUNK
    groups_per_tile = LANES // SSM_GROUP
    steps = [[u_sc[gb, pl.ds(t, n_chunks, stride=SSM_CHUNK), :]
              for gb in range(u_sc.shape[0])] for t in range(SSM_CHUNK)]
    for g in range(SSM_GROUPS):
        gb, lo = g // groups_per_tile, (g % groups_per_tile) * SSM_GROUP
        folded = jnp.concatenate([steps[t][gb][:, lo:lo + SSM_GROUP] for t in range(SSM_CHUNK)], axis=1)
        uf_ref[g] = folded.astype(BF16)


def _proj(x2d, w_bf16, wt_bf16, wu_bf16, *, batch, row_tile=512, col_tile=1024):
    m, k = x2d.shape
    n = w_bf16.shape[1]
    nt = wt_bf16.shape[0]
    nu = wu_bf16.shape[1]
    tiles_per_batch = m // batch // row_tile
    fold_rows = row_tile // SSM_CHUNK
    vmem = (2 * row_tile * k * 4 + 2 * k * (n + nt + nu) * 2 + 2 * row_tile * (n + nt + nu) * 2
            + row_tile * k * 2 + 2 * row_tile * col_tile * 4 + row_tile * nu * 4)
    return pl.pallas_call(
        functools.partial(_proj_kernel, col_tile=col_tile),
        grid=(m // row_tile,),
        in_specs=[pl.BlockSpec((row_tile, k), lambda i: (i, 0)),
                  pl.BlockSpec((k, n), lambda i: (0, 0)),
                  pl.BlockSpec((nt, k), lambda i: (0, 0)),
                  pl.BlockSpec((k, nu), lambda i: (0, 0))],
        out_specs=[pl.BlockSpec((row_tile, n), lambda i: (i, 0)),
                   pl.BlockSpec((None, nt, row_tile),
                                lambda i: (i // tiles_per_batch, 0, i % tiles_per_batch)),
                   pl.BlockSpec((SSM_GROUPS, fold_rows, SSM_CHUNK * SSM_GROUP), lambda i: (0, i, 0))],
        out_shape=[jax.ShapeDtypeStruct((m, n), BF16),
                   jax.ShapeDtypeStruct((batch, nt, m // batch), BF16),
                   jax.ShapeDtypeStruct((SSM_GROUPS, m // SSM_CHUNK, SSM_CHUNK * SSM_GROUP), BF16)],
        scratch_shapes=[pltpu.VMEM((nu // LANES, row_tile, LANES), F32)],
        compiler_params=pltpu.CompilerParams(
            dimension_semantics=("parallel",), vmem_limit_bytes=_vmem_limit(vmem)),
        name="proj",
    )(x2d, w_bf16, wt_bf16, wu_bf16)


def _t5_bucket_thresholds():
    max_exact = REL_BUCKETS // 2
    dist = np.arange(0, 2 * MOBA_BLOCK, dtype=np.int32)
    d = np.maximum(dist, 1).astype(np.float32)
    large = max_exact + (np.log(d / np.float32(max_exact)) / np.float32(math.log(REL_MAX_DIST / max_exact))
                         * np.float32(REL_BUCKETS - max_exact)).astype(np.int32)
    large = np.minimum(large, REL_BUCKETS - 1)
    bucket = np.where(dist < max_exact, dist, large)
    assert np.all(np.diff(bucket) >= 0) and bucket[-1] == REL_BUCKETS - 1
    return [int(np.argmax(bucket >= k)) for k in range(1, REL_BUCKETS)]


_BUCKET_THRESHOLDS = _t5_bucket_thresholds()


def _attn_kernel(relb_ref, qt_ref, k_ref, vt_ref, o_ref,
                 kaug, vaug_t, kmean, kmean_hi, kmean_lo, bias_t, qaug_t, acc_t, mcol, logits_a, logits_b, smax):
    hp = pl.program_id(1)
    blk = MOBA_BLOCK
    n_blocks = k_ref.shape[0] // blk
    n_slots = kmean.shape[1]
    lane = lax.broadcasted_iota(jnp.int32, (blk, LANES), 1)
    dim = lax.broadcasted_iota(jnp.int32, (LANES, blk), 0)

    def in_head(index, h):
        return (index >= HEAD_DIM * h) & (index < HEAD_DIM * (h + 1))

    def other_off(h):
        return HEAD_DIM * (1 - h)

    def _prepare():
        kmean[...] = jnp.zeros(kmean.shape, F32)
        for h in range(2):
            head = in_head(lane, h)
            off = other_off(h)

            def build(j, carry, h=h, head=head, off=off):
                rows = pl.ds(pl.multiple_of(j * blk, blk), blk)
                kb = k_ref[rows, :]
                onehot = jnp.where(lane - off == j, 1.0, 0.0).astype(BF16)
                kaug[h, rows, :] = jnp.where(head, kb, onehot)
                vaug_t[h, :HEAD_DIM, rows] = vt_ref[HEAD_DIM * h:HEAD_DIM * (h + 1), rows]
                vaug_t[h, HEAD_DIM:, rows] = jnp.ones((vaug_t.shape[1] - HEAD_DIM, blk), BF16)
                mean = jnp.sum(kb.astype(F32), axis=0, keepdims=True) * (1.0 / blk)
                kmean[h, pl.ds(j, 1), :] = jnp.where(head[:1], mean, 0.0)
                return carry

            lax.fori_loop(0, n_blocks, build, 0)
            km = kmean[h]
            hi = km.astype(BF16)
            kmean_hi[h] = hi
            kmean_lo[h] = (km - hi.astype(F32)).astype(BF16)

            head_id = 2 * hp + h
            far = relb_ref[REL_BUCKETS - 1, head_id]
            key = lax.broadcasted_iota(jnp.int32, (blk, blk), 0)
            qry = lax.broadcasted_iota(jnp.int32, (blk, blk), 1)
            for t, base in enumerate((0, blk)):
                dist = base + qry - key
                val = jnp.full((blk, blk), relb_ref[0, head_id] - far, F32)
                for kk, thr in enumerate(_BUCKET_THRESHOLDS, start=1):
                    val = jnp.where(dist >= thr, relb_ref[kk, head_id] - far, val)
                bias_t[h, t] = jnp.where(dist >= 0, val, MASK_VALUE)

    _prepare()
    block_id = lax.broadcasted_iota(jnp.int32, (n_slots, blk), 0)
    slots = (logits_a, logits_b)

    def block_rows(i, t):
        j = jnp.maximum(i - t, 0)
        return pl.ds(pl.multiple_of(j * blk, blk), blk)

    def compute_logits(i, t, slot, bias_tile):
        for h in range(2):
            s = _dot(kaug[h, block_rows(i, t), :], qaug_t[h])
            if bias_tile is not None:
                s = s + bias_t[h, bias_tile]
            slots[slot][h] = s.astype(BF16)
            smax[slot, h] = jnp.max(s, axis=0, keepdims=True).astype(BF16).astype(F32)

    def softmax_update(i, t, slot):
        for h in range(2):
            s = slots[slot][h]
            m_old = mcol[h]
            m_new = jnp.maximum(m_old, smax[slot, h])
            p = jnp.exp(s - m_new.astype(BF16))
            acc_t[h] = (acc_t[h] * jnp.exp(m_old - m_new)
                        + _dot(vaug_t[h, :, block_rows(i, t)], p))
            mcol[h] = m_new

    def start_block(i):
        i = jnp.minimum(i, n_blocks - 1)
        qt = qt_ref[:, pl.ds(pl.multiple_of(i * blk, blk), blk)]
        past = block_id < i
        for h in range(2):
            off = other_off(h)
            qs = jnp.where(in_head(dim, h), qt * 0.125, jnp.zeros_like(qt))
            gate = _dot(kmean_hi[h], qs) + _dot(kmean_lo[h], qs)
            gate = jnp.where(past, gate, -jnp.inf)
            chosen = jnp.zeros(gate.shape, jnp.bool_)
            for _ in range(MOBA_TOPK):
                top = jnp.max(gate, axis=0, keepdims=True)
                first = jnp.min(jnp.where(gate == top, block_id, n_slots), axis=0, keepdims=True)
                hit = block_id == first
                chosen = chosen | (hit & (top > -jnp.inf))
                gate = jnp.where(hit, -jnp.inf, gate)
            qaug_t[h] = qs
            qaug_t[h, off:off + n_slots, :] = jnp.where(
                past & jnp.logical_not(chosen), MASK_VALUE, 0.0).astype(BF16)
        compute_logits(i, 0, 0, 0)
        compute_logits(i, 1, 1, 1)

    def query_block(i, carry):
        n_steps = i + 1
        n_pairs = n_steps // 2
        odd = n_steps % 2 == 1
        mcol[...] = jnp.full(mcol.shape, -jnp.inf, F32)
        acc_t[...] = jnp.zeros(acc_t.shape, F32)

        def step_pair(pair, carry):
            t = 2 * pair
            softmax_update(i, t, 0)
            compute_logits(i, t + 2, 0, None)
            softmax_update(i, t + 1, 1)
            compute_logits(i, t + 3, 1, None)
            return carry

        lax.fori_loop(0, n_pairs - 1, step_pair, 0)
        t_last = 2 * (n_pairs - 1)

        @pl.when((n_pairs >= 1) & jnp.logical_not(odd))
        def _last_pair():
            softmax_update(i, t_last, 0)
            softmax_update(i, t_last + 1, 1)
            start_block(i + 1)

        @pl.when((n_pairs >= 1) & odd)
        def _last_pair_and_step():
            softmax_update(i, t_last, 0)
            compute_logits(i, t_last + 2, 0, None)
            softmax_update(i, t_last + 1, 1)
            softmax_update(i, t_last + 2, 0)
            start_block(i + 1)

        @pl.when(n_pairs == 0)
        def _only_step():
            softmax_update(i, 0, 0)
            start_block(i + 1)

        a0 = acc_t[0]
        a1 = acc_t[1]
        o_t = jnp.concatenate([a0[:HEAD_DIM] / a0[HEAD_DIM:HEAD_DIM + 1],
                               a1[:HEAD_DIM] / a1[HEAD_DIM:HEAD_DIM + 1]], axis=0)
        o_ref[pl.ds(pl.multiple_of(i * blk, blk), blk), :] = o_t.T.astype(o_ref.dtype)
        return carry

    start_block(0)
    lax.fori_loop(0, n_blocks, query_block, 0)


def _attention(proj3d, qv_t, rel_bias):
    b, s, _ = proj3d.shape
    n_blocks = s // MOBA_BLOCK
    n_slots = HEAD_DIM // 2
    assert s % MOBA_BLOCK == 0 and n_blocks <= n_slots
    blk = MOBA_BLOCK
    pv_rows = HEAD_DIM + 16
    scratch = [
        pltpu.VMEM((2, s, LANES), BF16),
        pltpu.VMEM((2, pv_rows, s), BF16),
        pltpu.VMEM((2, n_slots, LANES), F32),
        pltpu.VMEM((2, n_slots, LANES), BF16),
        pltpu.VMEM((2, n_slots, LANES), BF16),
        pltpu.VMEM((2, 2, blk, blk), F32),
        pltpu.VMEM((2, LANES, blk), BF16),
        pltpu.VMEM((2, pv_rows, blk), F32),
        pltpu.VMEM((2, 1, blk), F32),
        pltpu.VMEM((2, blk, blk), BF16),
        pltpu.VMEM((2, blk, blk), BF16),
        pltpu.VMEM((2, 2, 1, blk), F32),
    ]
    vmem = (2 * s * LANES * 2 + 2 * pv_rows * s * 2
            + 2 * 4 * s * LANES * 2
            + 4 * blk * blk * 4 + 4 * blk * blk * 2 + (2 << 20))
    return pl.pallas_call(
        _attn_kernel,
        grid=(b, HEAD_PAIRS),
        in_specs=[
            pl.BlockSpec(memory_space=pltpu.SMEM),
            pl.BlockSpec((None, LANES, s), lambda bi, hp: (bi, hp, 0)),
            pl.BlockSpec((None, s, LANES), lambda bi, hp: (bi, 0, COL_K * HEAD_PAIRS + hp)),
            pl.BlockSpec((None, LANES, s), lambda bi, hp: (bi, HEAD_PAIRS + hp, 0)),
        ],
        out_specs=pl.BlockSpec((None, s, LANES), lambda bi, hp: (bi, 0, hp)),
        out_shape=jax.ShapeDtypeStruct((b, s, ATTN_WIDTH), BF16),
        scratch_shapes=scratch,
        compiler_params=pltpu.CompilerParams(
            dimension_semantics=("parallel", "parallel"),
            vmem_limit_bytes=_vmem_limit(vmem)),
        name="moba_attn",
    )(rel_bias, qv_t, proj3d, qv_t)


def _ssm_weights(a_re, a_im, log_dt, b_re, b_im, c_re, c_im, d_skip):
    g, p_states, c = b_re.shape
    big_l = SSM_CHUNK
    dt = jnp.exp(log_dt.astype(F32))[:, None]
    ar = a_re.astype(F32)
    ai = a_im.astype(F32)
    mag = jnp.exp(dt * ar)
    ang = dt * ai
    abar_re = mag * jnp.cos(ang)
    abar_im = mag * jnp.sin(ang)
    den = ar * ar + ai * ai
    nr = abar_re - 1.0
    ni = abar_im
    fr = (nr * ar + ni * ai) / den
    fi = (ni * ar - nr * ai) / den
    br = b_re.astype(F32)
    bi = b_im.astype(F32)
    bbar_re = fr[..., None] * br - fi[..., None] * bi
    bbar_im = fr[..., None] * bi + fi[..., None] * br
    n = jnp.arange(big_l + 1, dtype=F32)[:, None, None]
    pmag = jnp.exp(n * (dt * ar)[None])
    pw_re = pmag * jnp.cos(n * ang[None])
    pw_im = pmag * jnp.sin(n * ang[None])
    cr = c_re.astype(F32)
    ci = c_im.astype(F32)
    ab_re = pw_re[..., None] * bbar_re[None] - pw_im[..., None] * bbar_im[None]
    ab_im = pw_re[..., None] * bbar_im[None] + pw_im[..., None] * bbar_re[None]
    hi = lax.Precision.HIGHEST
    kern = (jnp.einsum('gdp,ngpc->ngcd', cr, ab_re[:big_l], precision=hi)
            - jnp.einsum('gdp,ngpc->ngcd', ci, ab_im[:big_l], precision=hi))
    lag = jnp.arange(big_l)[None, :] - jnp.arange(big_l)[:, None]
    toep = kern[jnp.clip(lag, 0, big_l - 1)]
    toep = jnp.where((lag >= 0)[:, :, None, None, None], toep, 0.0)
    toep = toep.transpose(2, 0, 3, 1, 4).reshape(g, big_l * c, big_l * c)
    skip = jnp.tile(d_skip.astype(F32).reshape(g, 1, c), (1, big_l, 1)).reshape(g, big_l * c)
    toep = toep + skip[:, :, None] * jnp.eye(big_l * c, dtype=F32)[None]
    e_re = ab_re[:big_l][::-1].transpose(1, 0, 3, 2).reshape(g, big_l * c, p_states)
    e_im = ab_im[:big_l][::-1].transpose(1, 0, 3, 2).reshape(g, big_l * c, p_states)
    w_in = jnp.concatenate([e_re, e_im, e_im, e_re], axis=-1)
    ca_re = cr[None] * pw_re[1:, :, None, :] - ci[None] * pw_im[1:, :, None, :]
    ca_im = cr[None] * pw_im[1:, :, None, :] + ci[None] * pw_re[1:, :, None, :]
    o_re = ca_re.transpose(1, 3, 0, 2).reshape(g, p_states, big_l * c)
    o_im = -ca_im.transpose(1, 3, 0, 2).reshape(g, p_states, big_l * c)
    w_out = jnp.concatenate([o_re, o_im], axis=1)
    are, aim = pw_re[big_l], pw_im[big_l]
    carry = jnp.stack([jnp.concatenate([are, are], -1),
                       jnp.concatenate([-aim, aim], -1),
                       jnp.concatenate([aim, -aim], -1)], axis=1)
    return toep.astype(BF16), w_in.astype(BF16), w_out.astype(BF16), carry


def _ssm_kernel(u_ref, toep_ref, win_ref, wout_ref, carry_ref, y_ref, e_sc, prev_sc, *, batch):
    n_chunks = u_ref.shape[0] // batch
    half = LANES
    u = u_ref[...]
    e = _dot(u, win_ref[...])
    e_sc[0] = e[:, :half]
    e_sc[1] = e[:, half:]
    a1 = jnp.broadcast_to(carry_ref[0:1, :], (batch, half))
    a2 = jnp.broadcast_to(carry_ref[1:2, :], (batch, half))
    a3 = jnp.broadcast_to(carry_ref[2:3, :], (batch, half))

    def step(kk, state):
        st, st_swapped = state
        rows = pl.ds(kk, batch, stride=n_chunks)
        prev_sc[rows, :] = st
        new = a1 * st + a2 * st_swapped + e_sc[0, rows, :]
        new_swapped = a1 * st_swapped + a3 * st + e_sc[1, rows, :]
        return new, new_swapped

    zero = jnp.zeros((batch, half), F32)
    lax.fori_loop(0, n_chunks, step, (zero, zero), unroll=8)
    y = _dot(u, toep_ref[...]) + _dot(prev_sc[...].astype(BF16), wout_ref[...])
    y_ref[...] = y.astype(y_ref.dtype)


def _ssm(u_t, toep, w_in, w_out, carry, *, batch):
    g, rows, width = u_t.shape
    vmem = 2 * 2 * rows * width * 2 + rows * width * 4 + rows * LANES * 4 + 3 * rows * width * 4
    return pl.pallas_call(
        functools.partial(_ssm_kernel, batch=batch),
        grid=(g,),
        in_specs=[
            pl.BlockSpec((None, rows, width), lambda gi: (gi, 0, 0)),
            pl.BlockSpec((None, width, width), lambda gi: (gi, 0, 0)),
            pl.BlockSpec((None, width, width), lambda gi: (gi, 0, 0)),
            pl.BlockSpec((None, LANES, width), lambda gi: (gi, 0, 0)),
            pl.BlockSpec((None, 3, LANES), lambda gi: (gi, 0, 0)),
        ],
        out_specs=pl.BlockSpec((None, rows, width), lambda gi: (gi, 0, 0)),
        out_shape=jax.ShapeDtypeStruct((g, rows, width), BF16),
        scratch_shapes=[pltpu.VMEM((2, rows, LANES), F32), pltpu.VMEM((rows, LANES), F32)],
        compiler_params=pltpu.CompilerParams(
            dimension_semantics=("parallel",), vmem_limit_bytes=_vmem_limit(vmem)),
        name="s5_ssm",
    )(u_t, toep, w_in, w_out, carry)


def _final_kernel(x_ref, p_ref, oa_ref, yf_ref, za_ref, zs_ref, ga_ref, gs_ref,
                  wap_ref, wglu_ref, wsp_ref, wout_ref, wpg_ref, wpp_ref, lng_ref, lnb_ref, o_ref, ys_sc,
                  *, sub_rows):
    for sub in range(x_ref.shape[0] // sub_rows):
        _final_rows(sub, sub_rows, x_ref, p_ref, oa_ref, yf_ref, za_ref, zs_ref, ga_ref, gs_ref,
                    wap_ref, wglu_ref, wsp_ref, wout_ref, wpg_ref, wpp_ref, lng_ref, lnb_ref, o_ref, ys_sc)


def _final_rows(sub, sub_rows, x_ref, p_ref, oa_ref, yf_ref, za_ref, zs_ref, ga_ref, gs_ref,
                wap_ref, wglu_ref, wsp_ref, wout_ref, wpg_ref, wpp_ref, lng_ref, lnb_ref, o_ref, ys_sc):
    rows = slice(sub * sub_rows, (sub + 1) * sub_rows)
    n_chunks = sub_rows // SSM_CHUNK
    chunks = slice(sub * n_chunks, (sub + 1) * n_chunks)
    half = wpg_ref.shape[1] // 2
    x = x_ref[rows, :]
    xb = x.astype(BF16)
    za = za_ref[rows, :]
    a_in = oa_ref[rows, :] * (za * _sigmoid(za))
    y_a = _dot(a_in, wap_ref[...])
    gate_lo = _dot(xb, wpg_ref[:, :half])
    groups_per_tile = LANES // SSM_GROUP
    folded = [yf_ref[g, chunks, :].astype(F32) for g in range(SSM_GROUPS)]
    for t in range(SSM_CHUNK):
        for gb in range(ys_sc.shape[1]):
            tile = jnp.concatenate(
                [folded[gb * groups_per_tile + gl][:, t * SSM_GROUP:(t + 1) * SSM_GROUP]
                 for gl in range(groups_per_tile)], axis=1)
            ys_sc[sub, gb, pl.ds(t, n_chunks, stride=SSM_CHUNK), :] = tile
    ys = jnp.concatenate([ys_sc[sub, gb] for gb in range(ys_sc.shape[1])], axis=1)
    gelu = 0.5 * ys * (1.0 + lax.erf(ys * (2.0 ** -0.5)))
    glu = _dot(gelu.astype(BF16), wglu_ref[...])
    gate_hi = _dot(xb, wpg_ref[:, half:])
    zs = zs_ref[rows, :]
    s_in = glu[:, :SSM_WIDTH] * _sigmoid(glu[:, SSM_WIDTH:]) * (zs * _sigmoid(zs)).astype(F32)
    y_s = _dot(s_in.astype(BF16), wsp_ref[...])
    emb = _dot(p_ref[rows, :].astype(BF16), wpp_ref[...])
    merge = _sigmoid(ga_ref[rows, :]) * y_a.astype(BF16) + _sigmoid(gs_ref[rows, :]) * y_s.astype(BF16)
    mix = _dot(merge, wout_ref[...])
    ple = _sigmoid(jnp.concatenate([gate_lo, gate_hi], axis=1)) * emb
    hsum = DEEPNORM_ALPHA * x + mix + ple
    mu = jnp.mean(hsum, axis=-1, keepdims=True)
    cen = hsum - mu
    var = jnp.mean(cen * cen, axis=-1, keepdims=True)
    o_ref[rows, :] = cen * lax.rsqrt(var + LN_EPS) * lng_ref[...] + lnb_ref[...]


def _final(x2d, p2d, proj, o_a, y_fold, w_ap, w_glu, w_sp, w_out, w_pg, w_pp, ln_g, ln_b, *,
           row_tile=512, sub_rows=256):
    m = x2d.shape[0]
    half, full = SSM_WIDTH, D_MODEL

    def rows(width, col):
        return pl.BlockSpec((row_tile, width), lambda i: (i, col))

    def whole(arr):
        return pl.BlockSpec(arr.shape, lambda i: (0, 0))

    weights = (w_ap, w_glu, w_sp, w_out, w_pg, w_pp, ln_g, ln_b)
    vmem = (2 * sum(int(np.prod(w.shape)) * w.dtype.itemsize for w in weights)
            + 2 * row_tile * (2 * full * 4 + PLE_DIM * 4 + (4 * half + 2 * full) * 2)
            + 12 * row_tile * full * 4)
    return pl.pallas_call(
        functools.partial(_final_kernel, sub_rows=sub_rows),
        grid=(m // row_tile,),
        in_specs=[rows(full, 0), rows(PLE_DIM, 0), rows(half, 0),
                  pl.BlockSpec((SSM_GROUPS, row_tile // SSM_CHUNK, SSM_CHUNK * SSM_GROUP), lambda i: (0, i, 0)),
                  rows(half, COL_ZA), rows(half, COL_ZS), rows(full, COL_GA), rows(full, COL_GS)]
                 + [whole(w) for w in weights],
        out_specs=rows(full, 0),
        out_shape=jax.ShapeDtypeStruct((m, full), F32),
        scratch_shapes=[pltpu.VMEM((row_tile // sub_rows, half // LANES, sub_rows, LANES), F32)],
        compiler_params=pltpu.CompilerParams(
            dimension_semantics=("parallel",), vmem_limit_bytes=_vmem_limit(vmem)),
        name="final",
    )(x2d, p2d, o_a, y_fold, proj, proj, proj, proj, *weights)


def kernel(x, p, w_in, w_attn_proj, w_ssm_proj, w_out, ssm_a_re, ssm_a_im, ssm_log_dt, ssm_b_re, ssm_b_im, ssm_c_re, ssm_c_im, ssm_d, w_glu, w_ple_gate, w_ple_proj, ln_g, ln_b, rel_bias):
    b, s, d = x.shape
    m = b * s
    for i in range(w_in.shape[0]):
        x2d = x.reshape(m, d)
        wq, wk, wv, wza, wu, wzs, wga, wgs = jnp.split(w_in[i].astype(BF16), _IN_SPLITS, axis=1)
        w_main = jnp.concatenate([wga, wgs, wk, wza, wzs], axis=1)
        proj, qv_t, u_fold = _proj(x2d, w_main, jnp.concatenate([wq, wv], axis=1).T, wu, batch=b)
        o_a = _attention(proj.reshape(b, s, PROJ_WIDTH), qv_t, rel_bias.astype(F32))
        toep, s_in, s_out, carry = _ssm_weights(
            ssm_a_re[i], ssm_a_im[i], ssm_log_dt[i], ssm_b_re[i], ssm_b_im[i],
            ssm_c_re[i], ssm_c_im[i], ssm_d[i].reshape(SSM_GROUPS, SSM_GROUP))
        y_fold = _ssm(u_fold, toep, s_in, s_out, carry, batch=b)
        x2d = _final(x2d, p[i].reshape(m, PLE_DIM), proj, o_a.reshape(m, ATTN_WIDTH), y_fold,
                     w_attn_proj[i].astype(BF16), w_glu[i].astype(BF16), w_ssm_proj[i].astype(BF16),
                     w_out[i].astype(BF16), w_ple_gate[i].astype(BF16), w_ple_proj[i].astype(BF16),
                     ln_g[i].astype(F32).reshape(1, d), ln_b[i].astype(F32).reshape(1, d))
        x = x2d.reshape(b, s, d)
    return x
```

```python
import functools
import math

import numpy as np
import jax
import jax.numpy as jnp
from jax import lax
from jax.experimental import pallas as pl
from jax.experimental.pallas import tpu as pltpu

F32 = jnp.float32
BF16 = jnp.bfloat16

LANES = 128
V7X_VMEM_BYTES = 64 * 1024 * 1024

D_MODEL = 1024
PLE_DIM = 256
HEADS = 8
HEAD_DIM = 64
ATTN_WIDTH = HEADS * HEAD_DIM
HEAD_PAIRS = ATTN_WIDTH // LANES
MOBA_BLOCK = 256
MOBA_TOPK = 3
REL_BUCKETS = 32
REL_MAX_DIST = 128
SSM_WIDTH = 512
SSM_GROUP = 16
SSM_GROUPS = SSM_WIDTH // SSM_GROUP
SSM_STATE = 64
SSM_CHUNK = 16
IN_WIDTH = 4 * ATTN_WIDTH + 2 * SSM_WIDTH + 2 * D_MODEL
DEPTH = 1
DEEPNORM_ALPHA = (2.0 * DEPTH) ** 0.25
LN_EPS = 1e-5
MASK_VALUE = -1e30

PROJ_WIDTH = IN_WIDTH - 2 * ATTN_WIDTH - SSM_WIDTH
_IN_SPLITS = tuple(int(v) for v in np.cumsum(
    (ATTN_WIDTH,) * 4 + (SSM_WIDTH,) * 2 + (D_MODEL,) * 2)[:-1])
COL_GA, COL_GS = 0, 1
COL_K, COL_ZA, COL_ZS = 4, 5, 6


def _dot(a, b):
    return jnp.dot(a, b, preferred_element_type=F32)


def _dot_nt(a, b):
    return lax.dot_general(a, b, (((1,), (1,)), ((), ())), preferred_element_type=F32)


def _sigmoid(v):
    return 1.0 / (1.0 + jnp.exp(-v))


def _vmem_limit(nbytes):
    return int(min(V7X_VMEM_BYTES - (4 << 20), max(nbytes + (8 << 20), 32 << 20)))


def _proj_kernel(x_ref, w_ref, wt_ref, wu_ref, o_ref, t_ref, uf_ref, u_sc, *, col_tile):
    xb = x_ref[...].astype(BF16)
    u = _dot(xb, wu_ref[...])
    for gb in range(u_sc.shape[0]):
        u_sc[gb] = u[:, gb * LANES:(gb + 1) * LANES]
    n_cols = o_ref.shape[1]
    for start in range(0, n_cols, col_tile):
        cols = slice(start, min(start + col_tile, n_cols))
        o_ref[:, cols] = _dot(xb, w_ref[:, cols]).astype(BF16)
    t_ref[...] = _dot_nt(wt_ref[...], xb).astype(BF16)
    n_chunks = u_sc.shape[1] // SSM_CHUNK
    groups_per_tile = LANES // SSM_GROUP
    steps = [[u_sc[gb, pl.ds(t, n_chunks, stride=SSM_CHUNK), :]
              for gb in range(u_sc.shape[0])] for t in range(SSM_CHUNK)]
    for g in range(SSM_GROUPS):
        gb, lo = g // groups_per_tile, (g % groups_per_tile) * SSM_GROUP
        folded = jnp.concatenate([steps[t][gb][:, lo:lo + SSM_GROUP] for t in range(SSM_CHUNK)], axis=1)
        uf_ref[g] = folded.astype(BF16)


def _proj(x2d, w_bf16, wt_bf16, wu_bf16, *, batch, row_tile=512, col_tile=1024):
    m, k = x2d.shape
    n = w_bf16.shape[1]
    nt = wt_bf16.shape[0]
    nu = wu_bf16.shape[1]
    tiles_per_batch = m // batch // row_tile
    fold_rows = row_tile // SSM_CHUNK
    vmem = (2 * row_tile * k * 4 + 2 * k * (n + nt + nu) * 2 + 2 * row_tile * (n + nt + nu) * 2
            + row_tile * k * 2 + 2 * row_tile * col_tile * 4 + row_tile * nu * 4)
    return pl.pallas_call(
        functools.partial(_proj_kernel, col_tile=col_tile),
        grid=(m // row_tile,),
        in_specs=[pl.BlockSpec((row_tile, k), lambda i: (i, 0)),
                  pl.BlockSpec((k, n), lambda i: (0, 0)),
                  pl.BlockSpec((nt, k), lambda i: (0, 0)),
                  pl.BlockSpec((k, nu), lambda i: (0, 0))],
        out_specs=[pl.BlockSpec((row_tile, n), lambda i: (i, 0)),
                   pl.BlockSpec((None, nt, row_tile),
                                lambda i: (i // tiles_per_batch, 0, i % tiles_per_batch)),
                   pl.BlockSpec((SSM_GROUPS, fold_rows, SSM_CHUNK * SSM_GROUP), lambda i: (0, i, 0))],
        out_shape=[jax.ShapeDtypeStruct((m, n), BF16),
                   jax.ShapeDtypeStruct((batch, nt, m // batch), BF16),
                   jax.ShapeDtypeStruct((SSM_GROUPS, m // SSM_CHUNK, SSM_CHUNK * SSM_GROUP), BF16)],
        scratch_shapes=[pltpu.VMEM((nu // LANES, row_tile, LANES), F32)],
        compiler_params=pltpu.CompilerParams(
            dimension_semantics=("parallel",), vmem_limit_bytes=_vmem_limit(vmem)),
        name="proj",
    )(x2d, w_bf16, wt_bf16, wu_bf16)


def _t5_bucket_thresholds():
    max_exact = REL_BUCKETS // 2
    dist = np.arange(0, 2 * MOBA_BLOCK, dtype=np.int32)
    d = np.maximum(dist, 1).astype(np.float32)
    large = max_exact + (np.log(d / np.float32(max_exact)) / np.float32(math.log(REL_MAX_DIST / max_exact))
                         * np.float32(REL_BUCKETS - max_exact)).astype(np.int32)
    large = np.minimum(large, REL_BUCKETS - 1)
    bucket = np.where(dist < max_exact, dist, large)
    assert np.all(np.diff(bucket) >= 0) and bucket[-1] == REL_BUCKETS - 1
    return [int(np.argmax(bucket >= k)) for k in range(1, REL_BUCKETS)]


_BUCKET_THRESHOLDS = _t5_bucket_thresholds()


def _attn_kernel(relb_ref, qt_ref, k_ref, vt_ref, o_ref,
                 kaug, vaug_t, kmean, kmean_hi, kmean_lo, bias_t, qaug_t, acc_t, mcol, logits_a, logits_b, smax):
    hp = pl.program_id(1)
    blk = MOBA_BLOCK
    n_blocks = k_ref.shape[0] // blk
    n_slots = kmean.shape[1]
    lane = lax.broadcasted_iota(jnp.int32, (blk, LANES), 1)
    dim = lax.broadcasted_iota(jnp.int32, (LANES, blk), 0)

    def in_head(index, h):
        return (index >= HEAD_DIM * h) & (index < HEAD_DIM * (h + 1))

    def other_off(h):
        return HEAD_DIM * (1 - h)

    def _prepare():
        kmean[...] = jnp.zeros(kmean.shape, F32)
        for h in range(2):
            head = in_head(lane, h)
            off = other_off(h)

            def build(j, carry, h=h, head=head, off=off):
                rows = pl.ds(pl.multiple_of(j * blk, blk), blk)
                kb = k_ref[rows, :]
                onehot = jnp.where(lane - off == j, 1.0, 0.0).astype(BF16)
                kaug[h, rows, :] = jnp.where(head, kb, onehot)
                vaug_t[h, :HEAD_DIM, rows] = vt_ref[HEAD_DIM * h:HEAD_DIM * (h + 1), rows]
                vaug_t[h, HEAD_DIM:, rows] = jnp.ones((vaug_t.shape[1] - HEAD_DIM, blk), BF16)
                mean = jnp.sum(kb.astype(F32), axis=0, keepdims=True) * (1.0 / blk)
                kmean[h, pl.ds(j, 1), :] = jnp.where(head[:1], mean, 0.0)
                return carry

            lax.fori_loop(0, n_blocks, build, 0)
            km = kmean[h]
            hi = km.astype(BF16)
            kmean_hi[h] = hi
            kmean_lo[h] = (km - hi.astype(F32)).astype(BF16)

            head_id = 2 * hp + h
            far = relb_ref[REL_BUCKETS - 1, head_id]
            key = lax.broadcasted_iota(jnp.int32, (blk, blk), 0)
            qry = lax.broadcasted_iota(jnp.int32, (blk, blk), 1)
            for t, base in enumerate((0, blk)):
                dist = base + qry - key
                val = jnp.full((blk, blk), relb_ref[0, head_id] - far, F32)
                for kk, thr in enumerate(_BUCKET_THRESHOLDS, start=1):
                    val = jnp.where(dist >= thr, relb_ref[kk, head_id] - far, val)
                bias_t[h, t] = jnp.where(dist >= 0, val, MASK_VALUE)

    _prepare()
    block_id = lax.broadcasted_iota(jnp.int32, (n_slots, blk), 0)
    slots = (logits_a, logits_b)

    def block_rows(i, t):
        j = jnp.maximum(i - t, 0)
        return pl.ds(pl.multiple_of(j * blk, blk), blk)

    def compute_logits(i, t, slot, bias_tile, heads=(0, 1)):
        for h in heads:
            s = _dot(kaug[h, block_rows(i, t), :], qaug_t[i % 2, h])
            if bias_tile is not None:
                s = s + bias_t[h, bias_tile]
            slots[slot][h] = s.astype(BF16)
            smax[slot, h] = jnp.max(s, axis=0, keepdims=True).astype(BF16).astype(F32)

    def softmax_update(i, t, slot, heads=(0, 1)):
        for h in heads:
            s = slots[slot][h]
            m_old = mcol[h]
            m_new = jnp.maximum(m_old, smax[slot, h])
            p = jnp.exp(s - m_new.astype(BF16))
            acc_t[h] = (acc_t[h] * jnp.exp(m_old - m_new)
                        + _dot(vaug_t[h, :, block_rows(i, t)], p))
            mcol[h] = m_new

    def select_blocks(i):
        qt = qt_ref[:, pl.ds(pl.multiple_of(i * blk, blk), blk)]
        past = block_id < i
        for h in range(2):
            off = other_off(h)
            qs = jnp.where(in_head(dim, h), qt * 0.125, jnp.zeros_like(qt))
            gate = _dot(kmean_hi[h], qs) + _dot(kmean_lo[h], qs)
            gate = jnp.where(past, gate, -jnp.inf)
            chosen = jnp.zeros(gate.shape, jnp.bool_)
            for _ in range(MOBA_TOPK):
                top = jnp.max(gate, axis=0, keepdims=True)
                first = jnp.min(jnp.where(gate == top, block_id, n_slots), axis=0, keepdims=True)
                hit = block_id == first
                chosen = chosen | (hit & (top > -jnp.inf))
                gate = jnp.where(hit, -jnp.inf, gate)
            qaug_t[i % 2, h] = qs
            qaug_t[i % 2, h, off:off + n_slots, :] = jnp.where(
                past & jnp.logical_not(chosen), MASK_VALUE, 0.0).astype(BF16)

    def query_block(i, carry):
        n_steps = i + 1
        n_pairs = n_steps // 2
        odd = n_steps % 2 == 1
        mcol[...] = jnp.full(mcol.shape, -jnp.inf, F32)
        acc_t[...] = jnp.zeros(acc_t.shape, F32)

        def step_pair(pair, carry):
            t = 2 * pair
            for step, slot in ((t, 0), (t + 1, 1)):
                for h in range(2):
                    softmax_update(i, step, slot, (h,))
                    compute_logits(i, step + 2, slot, None, (h,))
            return carry

        n_full = jnp.maximum(n_pairs - 1, 0)

        def pairs_from(first, count):
            for k in range(count):
                step_pair(first + k, 0)

        def four_pairs(k, carry):
            pairs_from(4 * k, 4)
            return carry

        lax.fori_loop(0, n_full // 4, four_pairs, 0)

        @pl.when(n_full % 4 >= 2)
        def _two_more_pairs():
            pairs_from(n_full // 4 * 4, 2)

        @pl.when(n_full % 2 == 1)
        def _one_more_pair():
            pairs_from(n_full - 1, 1)

        t_last = 2 * (n_pairs - 1)

        nxt = jnp.where(i + 1 < n_blocks, i + 1, jnp.maximum(i - 1, 0))

        def update_then_next_logits(t, slot):
            for h in range(2):
                softmax_update(i, t, slot, (h,))
                compute_logits(nxt, slot, slot, slot, (h,))

        @pl.when((n_pairs >= 1) & jnp.logical_not(odd))
        def _last_pair():
            select_blocks(nxt)
            update_then_next_logits(t_last, 0)
            update_then_next_logits(t_last + 1, 1)

        @pl.when((n_pairs >= 1) & odd)
        def _last_pair_and_step():
            for h in range(2):
                softmax_update(i, t_last, 0, (h,))
                compute_logits(i, t_last + 2, 0, None, (h,))
            select_blocks(nxt)
            update_then_next_logits(t_last + 1, 1)
            update_then_next_logits(t_last + 2, 0)

        @pl.when(n_pairs == 0)
        def _only_step():
            select_blocks(nxt)
            update_then_next_logits(0, 0)
            compute_logits(nxt, 1, 1, 1)

        a0 = acc_t[0]
        a1 = acc_t[1]
        o_t = jnp.concatenate([a0[:HEAD_DIM] / a0[HEAD_DIM:HEAD_DIM + 1],
                               a1[:HEAD_DIM] / a1[HEAD_DIM:HEAD_DIM + 1]], axis=0)
        o_ref[pl.ds(pl.multiple_of(i * blk, blk), blk), :] = o_t.T.astype(o_ref.dtype)
        return carry

    select_blocks(0)
    compute_logits(0, 0, 0, 0)
    compute_logits(0, 1, 1, 1)
    lax.fori_loop(0, n_blocks, query_block, 0)


def _attention(proj3d, qv_t, rel_bias):
    b, s, _ = proj3d.shape
    n_blocks = s // MOBA_BLOCK
    n_slots = HEAD_DIM // 2
    assert s % MOBA_BLOCK == 0 and n_blocks <= n_slots
    blk = MOBA_BLOCK
    pv_rows = HEAD_DIM + 16
    scratch = [
        pltpu.VMEM((2, s, LANES), BF16),
        pltpu.VMEM((2, pv_rows, s), BF16),
        pltpu.VMEM((2, n_slots, LANES), F32),
        pltpu.VMEM((2, n_slots, LANES), BF16),
        pltpu.VMEM((2, n_slots, LANES), BF16),
        pltpu.VMEM((2, 2, blk, blk), F32),
        pltpu.VMEM((2, 2, LANES, blk), BF16),
        pltpu.VMEM((2, pv_rows, blk), F32),
        pltpu.VMEM((2, 1, blk), F32),
        pltpu.VMEM((2, blk, blk), BF16),
        pltpu.VMEM((2, blk, blk), BF16),
        pltpu.VMEM((2, 2, 1, blk), F32),
    ]
    vmem = (2 * s * LANES * 2 + 2 * pv_rows * s * 2
            + 2 * 4 * s * LANES * 2
            + 4 * blk * blk * 4 + 4 * blk * blk * 2 + (2 << 20))
    return pl.pallas_call(
        _attn_kernel,
        grid=(b, HEAD_PAIRS),
        in_specs=[
            pl.BlockSpec(memory_space=pltpu.SMEM),
            pl.BlockSpec((None, LANES, s), lambda bi, hp: (bi, hp, 0)),
            pl.BlockSpec((None, s, LANES), lambda bi, hp: (bi, 0, COL_K * HEAD_PAIRS + hp)),
            pl.BlockSpec((None, LANES, s), lambda bi, hp: (bi, HEAD_PAIRS + hp, 0)),
        ],
        out_specs=pl.BlockSpec((None, s, LANES), lambda bi, hp: (bi, 0, hp)),
        out_shape=jax.ShapeDtypeStruct((b, s, ATTN_WIDTH), BF16),
        scratch_shapes=scratch,
        compiler_params=pltpu.CompilerParams(
            dimension_semantics=("parallel", "parallel"),
            vmem_limit_bytes=_vmem_limit(vmem)),
        name="moba_attn",
    )(rel_bias, qv_t, proj3d, qv_t)


def _ssm_weights(a_re, a_im, log_dt, b_re, b_im, c_re, c_im, d_skip):
    g, p_states, c = b_re.shape
    big_l = SSM_CHUNK
    dt = jnp.exp(log_dt.astype(F32))[:, None]
    ar = a_re.astype(F32)
    ai = a_im.astype(F32)
    mag = jnp.exp(dt * ar)
    ang = dt * ai
    abar_re = mag * jnp.cos(ang)
    abar_im = mag * jnp.sin(ang)
    den = ar * ar + ai * ai
    nr = abar_re - 1.0
    ni = abar_im
    fr = (nr * ar + ni * ai) / den
    fi = (ni * ar - nr * ai) / den
    br = b_re.astype(F32)
    bi = b_im.astype(F32)
    bbar_re = fr[..., None] * br - fi[..., None] * bi
    bbar_im = fr[..., None] * bi + fi[..., None] * br
    n = jnp.arange(big_l + 1, dtype=F32)[:, None, None]
    pmag = jnp.exp(n * (dt * ar)[None])
    pw_re = pmag * jnp.cos(n * ang[None])
    pw_im = pmag * jnp.sin(n * ang[None])
    cr = c_re.astype(F32)
    ci = c_im.astype(F32)
    ab_re = pw_re[..., None] * bbar_re[None] - pw_im[..., None] * bbar_im[None]
    ab_im = pw_re[..., None] * bbar_im[None] + pw_im[..., None] * bbar_re[None]
    hi = lax.Precision.HIGHEST
    kern = (jnp.einsum('gdp,ngpc->ngcd', cr, ab_re[:big_l], precision=hi)
            - jnp.einsum('gdp,ngpc->ngcd', ci, ab_im[:big_l], precision=hi))
    lag = jnp.arange(big_l)[None, :] - jnp.arange(big_l)[:, None]
    toep = kern[jnp.clip(lag, 0, big_l - 1)]
    toep = jnp.where((lag >= 0)[:, :, None, None, None], toep, 0.0)
    toep = toep.transpose(2, 0, 3, 1, 4).reshape(g, big_l * c, big_l * c)
    skip = jnp.tile(d_skip.astype(F32).reshape(g, 1, c), (1, big_l, 1)).reshape(g, big_l * c)
    toep = toep + skip[:, :, None] * jnp.eye(big_l * c, dtype=F32)[None]
    e_re = ab_re[:big_l][::-1].transpose(1, 0, 3, 2).reshape(g, big_l * c, p_states)
    e_im = ab_im[:big_l][::-1].transpose(1, 0, 3, 2).reshape(g, big_l * c, p_states)
    w_in = jnp.concatenate([e_re, e_im, e_im, e_re], axis=-1)
    ca_re = cr[None] * pw_re[1:, :, None, :] - ci[None] * pw_im[1:, :, None, :]
    ca_im = cr[None] * pw_im[1:, :, None, :] + ci[None] * pw_re[1:, :, None, :]
    o_re = ca_re.transpose(1, 3, 0, 2).reshape(g, p_states, big_l * c)
    o_im = -ca_im.transpose(1, 3, 0, 2).reshape(g, p_states, big_l * c)
    w_out = jnp.concatenate([o_re, o_im], axis=1)
    are, aim = pw_re[big_l], pw_im[big_l]
    carry = jnp.stack([jnp.concatenate([are, are], -1),
                       jnp.concatenate([-aim, aim], -1),
                       jnp.concatenate([aim, -aim], -1)], axis=1)
    return toep.astype(BF16), w_in.astype(BF16), w_out.astype(BF16), carry


def _ssm_kernel(u_ref, toep_ref, win_ref, wout_ref, carry_ref, y_ref, e_sc, prev_sc, *, batch):
    n_chunks = u_ref.shape[0] // batch
    half = LANES
    u = u_ref[...]
    e = _dot(u, win_ref[...])
    e_sc[0] = e[:, :half]
    e_sc[1] = e[:, half:]
    a1 = jnp.broadcast_to(carry_ref[0:1, :], (batch, half))
    a2 = jnp.broadcast_to(carry_ref[1:2, :], (batch, half))
    a3 = jnp.broadcast_to(carry_ref[2:3, :], (batch, half))

    def step(kk, state):
        st, st_swapped = state
        rows = pl.ds(kk, batch, stride=n_chunks)
        prev_sc[rows, :] = st
        new = a1 * st + a2 * st_swapped + e_sc[0, rows, :]
        new_swapped = a1 * st_swapped + a3 * st + e_sc[1, rows, :]
        return new, new_swapped

    zero = jnp.zeros((batch, half), F32)
    lax.fori_loop(0, n_chunks, step, (zero, zero), unroll=8)
    y = _dot(u, toep_ref[...]) + _dot(prev_sc[...].astype(BF16), wout_ref[...])
    y_ref[...] = y.astype(y_ref.dtype)


def _ssm(u_t, toep, w_in, w_out, carry, *, batch):
    g, rows, width = u_t.shape
    vmem = 2 * 2 * rows * width * 2 + rows * width * 4 + rows * LANES * 4 + 3 * rows * width * 4
    return pl.pallas_call(
        functools.partial(_ssm_kernel, batch=batch),
        grid=(g,),
        in_specs=[
            pl.BlockSpec((None, rows, width), lambda gi: (gi, 0, 0)),
            pl.BlockSpec((None, width, width), lambda gi: (gi, 0, 0)),
            pl.BlockSpec((None, width, width), lambda gi: (gi, 0, 0)),
            pl.BlockSpec((None, LANES, width), lambda gi: (gi, 0, 0)),
            pl.BlockSpec((None, 3, LANES), lambda gi: (gi, 0, 0)),
        ],
        out_specs=pl.BlockSpec((None, rows, width), lambda gi: (gi, 0, 0)),
        out_shape=jax.ShapeDtypeStruct((g, rows, width), BF16),
        scratch_shapes=[pltpu.VMEM((2, rows, LANES), F32), pltpu.VMEM((rows, LANES), F32)],
        compiler_params=pltpu.CompilerParams(
            dimension_semantics=("parallel",), vmem_limit_bytes=_vmem_limit(vmem)),
        name="s5_ssm",
    )(u_t, toep, w_in, w_out, carry)


def _final_kernel(x_ref, p_ref, oa_ref, yf_ref, za_ref, zs_ref, ga_ref, gs_ref,
                  wap_ref, wglu_ref, wsp_ref, wout_ref, wpg_ref, wpp_ref, lng_ref, lnb_ref, o_ref, ys_sc,
                  *, sub_rows):
    for sub in range(x_ref.shape[0] // sub_rows):
        _final_rows(sub, sub_rows, x_ref, p_ref, oa_ref, yf_ref, za_ref, zs_ref, ga_ref, gs_ref,
                    wap_ref, wglu_ref, wsp_ref, wout_ref, wpg_ref, wpp_ref, lng_ref, lnb_ref, o_ref, ys_sc)


def _final_rows(sub, sub_rows, x_ref, p_ref, oa_ref, yf_ref, za_ref, zs_ref, ga_ref, gs_ref,
                wap_ref, wglu_ref, wsp_ref, wout_ref, wpg_ref, wpp_ref, lng_ref, lnb_ref, o_ref, ys_sc):
    rows = slice(sub * sub_rows, (sub + 1) * sub_rows)
    n_chunks = sub_rows // SSM_CHUNK
    chunks = slice(sub * n_chunks, (sub + 1) * n_chunks)
    half = wpg_ref.shape[1] // 2
    x = x_ref[rows, :]
    xb = x.astype(BF16)
    za = za_ref[rows, :]
    a_in = oa_ref[rows, :] * (za * _sigmoid(za))
    y_a = _dot(a_in, wap_ref[...])
    gate_lo = _dot(xb, wpg_ref[:, :half])
    groups_per_tile = LANES // SSM_GROUP
    folded = [yf_ref[g, chunks, :].astype(F32) for g in range(SSM_GROUPS)]
    for t in range(SSM_CHUNK):
        for gb in range(ys_sc.shape[1]):
            tile = jnp.concatenate(
                [folded[gb * groups_per_tile + gl][:, t * SSM_GROUP:(t + 1) * SSM_GROUP]
                 for gl in range(groups_per_tile)], axis=1)
            ys_sc[sub, gb, pl.ds(t, n_chunks, stride=SSM_CHUNK), :] = tile
    ys = jnp.concatenate([ys_sc[sub, gb] for gb in range(ys_sc.shape[1])], axis=1)
    gelu = 0.5 * ys * (1.0 + lax.erf(ys * (2.0 ** -0.5)))
    glu = _dot(gelu.astype(BF16), wglu_ref[...])
    gate_hi = _dot(xb, wpg_ref[:, half:])
    zs = zs_ref[rows, :]
    s_in = glu[:, :SSM_WIDTH] * _sigmoid(glu[:, SSM_WIDTH:]) * (zs * _sigmoid(zs)).astype(F32)
    y_s = _dot(s_in.astype(BF16), wsp_ref[...])
    emb = _dot(p_ref[rows, :].astype(BF16), wpp_ref[...])
    merge = _sigmoid(ga_ref[rows, :]) * y_a.astype(BF16) + _sigmoid(gs_ref[rows, :]) * y_s.astype(BF16)
    mix = _dot(merge, wout_ref[...])
    ple = _sigmoid(jnp.concatenate([gate_lo, gate_hi], axis=1)) * emb
    hsum = DEEPNORM_ALPHA * x + mix + ple
    mu = jnp.mean(hsum, axis=-1, keepdims=True)
    cen = hsum - mu
    var = jnp.mean(cen * cen, axis=-1, keepdims=True)
    o_ref[rows, :] = cen * lax.rsqrt(var + LN_EPS) * lng_ref[...] + lnb_ref[...]


def _final(x2d, p2d, proj, o_a, y_fold, w_ap, w_glu, w_sp, w_out, w_pg, w_pp, ln_g, ln_b, *,
           row_tile=512, sub_rows=256):
    m = x2d.shape[0]
    half, full = SSM_WIDTH, D_MODEL

    def rows(width, col):
        return pl.BlockSpec((row_tile, width), lambda i: (i, col))

    def whole(arr):
        return pl.BlockSpec(arr.shape, lambda i: (0, 0))

    weights = (w_ap, w_glu, w_sp, w_out, w_pg, w_pp, ln_g, ln_b)
    vmem = (2 * sum(int(np.prod(w.shape)) * w.dtype.itemsize for w in weights)
            + 2 * row_tile * (2 * full * 4 + PLE_DIM * 4 + (4 * half + 2 * full) * 2)
            + 12 * row_tile * full * 4)
    return pl.pallas_call(
        functools.partial(_final_kernel, sub_rows=sub_rows),
        grid=(m // row_tile,),
        in_specs=[rows(full, 0), rows(PLE_DIM, 0), rows(half, 0),
                  pl.BlockSpec((SSM_GROUPS, row_tile // SSM_CHUNK, SSM_CHUNK * SSM_GROUP), lambda i: (0, i, 0)),
                  rows(half, COL_ZA), rows(half, COL_ZS), rows(full, COL_GA), rows(full, COL_GS)]
                 + [whole(w) for w in weights],
        out_specs=rows(full, 0),
        out_shape=jax.ShapeDtypeStruct((m, full), F32),
        scratch_shapes=[pltpu.VMEM((row_tile // sub_rows, half // LANES, sub_rows, LANES), F32)],
        compiler_params=pltpu.CompilerParams(
            dimension_semantics=("parallel",), vmem_limit_bytes=_vmem_limit(vmem)),
        name="final",
    )(x2d, p2d, o_a, y_fold, proj, proj, proj, proj, *weights)


def kernel(x, p, w_in, w_attn_proj, w_ssm_proj, w_out, ssm_a_re, ssm_a_im, ssm_log_dt, ssm_b_re, ssm_b_im, ssm_c_re, ssm_c_im, ssm_d, w_glu, w_ple_gate, w_ple_proj, ln_g, ln_b, rel_bias):
    b, s, d = x.shape
    m = b * s
    for i in range(w_in.shape[0]):
        x2d = x.reshape(m, d)
        wq, wk, wv, wza, wu, wzs, wga, wgs = jnp.split(w_in[i].astype(BF16), _IN_SPLITS, axis=1)
        w_main = jnp.concatenate([wga, wgs, wk, wza, wzs], axis=1)
        proj, qv_t, u_fold = _proj(x2d, w_main, jnp.concatenate([wq, wv], axis=1).T, wu, batch=b)
        o_a = _attention(proj.reshape(b, s, PROJ_WIDTH), qv_t, rel_bias.astype(F32))
        toep, s_in, s_out, carry = _ssm_weights(
            ssm_a_re[i], ssm_a_im[i], ssm_log_dt[i], ssm_b_re[i], ssm_b_im[i],
            ssm_c_re[i], ssm_c_im[i], ssm_d[i].reshape(SSM_GROUPS, SSM_GROUP))
        y_fold = _ssm(u_fold, toep, s_in, s_out, carry, batch=b)
        x2d = _final(x2d, p[i].reshape(m, PLE_DIM), proj, o_a.reshape(m, ATTN_WIDTH), y_fold,
                     w_attn_proj[i].astype(BF16), w_glu[i].astype(BF16), w_ssm_proj[i].astype(BF16),
                     w_out[i].astype(BF16), w_ple_gate[i].astype(BF16), w_ple_proj[i].astype(BF16),
                     ln_g[i].astype(F32).reshape(1, d), ln_b[i].astype(F32).reshape(1, d))
        x = x2d.reshape(b, s, d)
    return x
```

```python
import functools
import math

import numpy as np
import jax
import jax.numpy as jnp
from jax import lax
from jax.experimental import pallas as pl
from jax.experimental.pallas import tpu as pltpu

F32 = jnp.float32
BF16 = jnp.bfloat16

LANES = 128
V7X_VMEM_BYTES = 64 * 1024 * 1024

D_MODEL = 1024
PLE_DIM = 256
HEADS = 8
HEAD_DIM = 64
ATTN_WIDTH = HEADS * HEAD_DIM
HEAD_PAIRS = ATTN_WIDTH // LANES
MOBA_BLOCK = 256
MOBA_TOPK = 3
REL_BUCKETS = 32
REL_MAX_DIST = 128
SSM_WIDTH = 512
SSM_GROUP = 16
SSM_GROUPS = SSM_WIDTH // SSM_GROUP
SSM_STATE = 64
SSM_CHUNK = 16
IN_WIDTH = 4 * ATTN_WIDTH + 2 * SSM_WIDTH + 2 * D_MODEL
DEPTH = 1
DEEPNORM_ALPHA = (2.0 * DEPTH) ** 0.25
LN_EPS = 1e-5
MASK_VALUE = -1e30

PROJ_WIDTH = IN_WIDTH - 2 * ATTN_WIDTH - SSM_WIDTH
_IN_SPLITS = tuple(int(v) for v in np.cumsum(
    (ATTN_WIDTH,) * 4 + (SSM_WIDTH,) * 2 + (D_MODEL,) * 2)[:-1])
COL_GA, COL_GS = 0, 1
COL_K, COL_ZA, COL_ZS = 4, 5, 6


def _dot(a, b):
    return jnp.dot(a, b, preferred_element_type=F32)


def _dot_nt(a, b):
    return lax.dot_general(a, b, (((1,), (1,)), ((), ())), preferred_element_type=F32)


def _sigmoid(v):
    return 1.0 / (1.0 + jnp.exp(-v))


def _vmem_limit(nbytes):
    return int(min(V7X_VMEM_BYTES - (4 << 20), max(nbytes + (8 << 20), 32 << 20)))


def _proj_kernel(x_ref, w_ref, wt_ref, wu_ref, o_ref, t_ref, uf_ref, u_sc, *, col_tile):
    xb = x_ref[...].astype(BF16)
    u = _dot(xb, wu_ref[...])
    for gb in range(u_sc.shape[0]):
        u_sc[gb] = u[:, gb * LANES:(gb + 1) * LANES]
    n_cols = o_ref.shape[1]
    for start in range(0, n_cols, col_tile):
        cols = slice(start, min(start + col_tile, n_cols))
        o_ref[:, cols] = _dot(xb, w_ref[:, cols]).astype(BF16)
    t_ref[...] = _dot_nt(wt_ref[...], xb).astype(BF16)
    n_chunks = u_sc.shape[1] // SSM_CHUNK
    groups_per_tile = LANES // SSM_GROUP
    steps = [[u_sc[gb, pl.ds(t, n_chunks, stride=SSM_CHUNK), :]
              for gb in range(u_sc.shape[0])] for t in range(SSM_CHUNK)]
    for g in range(SSM_GROUPS):
        gb, lo = g // groups_per_tile, (g % groups_per_tile) * SSM_GROUP
        folded = jnp.concatenate([steps[t][gb][:, lo:lo + SSM_GROUP] for t in range(SSM_CHUNK)], axis=1)
        uf_ref[g] = folded.astype(BF16)


def _proj(x2d, w_bf16, wt_bf16, wu_bf16, *, batch, row_tile=512, col_tile=1024):
    m, k = x2d.shape
    n = w_bf16.shape[1]
    nt = wt_bf16.shape[0]
    nu = wu_bf16.shape[1]
    tiles_per_batch = m // batch // row_tile
    fold_rows = row_tile // SSM_CHUNK
    vmem = (2 * row_tile * k * 4 + 2 * k * (n + nt + nu) * 2 + 2 * row_tile * (n + nt + nu) * 2
            + row_tile * k * 2 + 2 * row_tile * col_tile * 4 + row_tile * nu * 4)
    return pl.pallas_call(
        functools.partial(_proj_kernel, col_tile=col_tile),
        grid=(m // row_tile,),
        in_specs=[pl.BlockSpec((row_tile, k), lambda i: (i, 0)),
                  pl.BlockSpec((k, n), lambda i: (0, 0)),
                  pl.BlockSpec((nt, k), lambda i: (0, 0)),
                  pl.BlockSpec((k, nu), lambda i: (0, 0))],
        out_specs=[pl.BlockSpec((row_tile, n), lambda i: (i, 0)),
                   pl.BlockSpec((None, nt, row_tile),
                                lambda i: (i // tiles_per_batch, 0, i % tiles_per_batch)),
                   pl.BlockSpec((SSM_GROUPS, fold_rows, SSM_CHUNK * SSM_GROUP), lambda i: (0, i, 0))],
        out_shape=[jax.ShapeDtypeStruct((m, n), BF16),
                   jax.ShapeDtypeStruct((batch, nt, m // batch), BF16),
                   jax.ShapeDtypeStruct((SSM_GROUPS, m // SSM_CHUNK, SSM_CHUNK * SSM_GROUP), BF16)],
        scratch_shapes=[pltpu.VMEM((nu // LANES, row_tile, LANES), F32)],
        compiler_params=pltpu.CompilerParams(
            dimension_semantics=("parallel",), vmem_limit_bytes=_vmem_limit(vmem)),
        name="proj",
    )(x2d, w_bf16, wt_bf16, wu_bf16)


def _t5_bucket_thresholds():
    max_exact = REL_BUCKETS // 2
    dist = np.arange(0, 2 * MOBA_BLOCK, dtype=np.int32)
    d = np.maximum(dist, 1).astype(np.float32)
    large = max_exact + (np.log(d / np.float32(max_exact)) / np.float32(math.log(REL_MAX_DIST / max_exact))
                         * np.float32(REL_BUCKETS - max_exact)).astype(np.int32)
    large = np.minimum(large, REL_BUCKETS - 1)
    bucket = np.where(dist < max_exact, dist, large)
    assert np.all(np.diff(bucket) >= 0) and bucket[-1] == REL_BUCKETS - 1
    return [int(np.argmax(bucket >= k)) for k in range(1, REL_BUCKETS)]


_BUCKET_THRESHOLDS = _t5_bucket_thresholds()


def _attn_kernel(relb_ref, qt_ref, k_ref, vt_ref, o_ref,
                 kaug, vaug_t, kmean, kmean_hi, kmean_lo, bias_t, qaug_t, acc_t, mcol, logits_a, logits_b, smax):
    hp = pl.program_id(1)
    blk = MOBA_BLOCK
    n_blocks = k_ref.shape[0] // blk
    n_slots = kmean.shape[1]
    lane = lax.broadcasted_iota(jnp.int32, (blk, LANES), 1)
    dim = lax.broadcasted_iota(jnp.int32, (LANES, blk), 0)

    def in_head(index, h):
        return (index >= HEAD_DIM * h) & (index < HEAD_DIM * (h + 1))

    def other_off(h):
        return HEAD_DIM * (1 - h)

    def _prepare():
        kmean[...] = jnp.zeros(kmean.shape, F32)
        for h in range(2):
            head = in_head(lane, h)
            off = other_off(h)

            def build(j, carry, h=h, head=head, off=off):
                rows = pl.ds(pl.multiple_of(j * blk, blk), blk)
                kb = k_ref[rows, :]
                onehot = jnp.where(lane - off == j, 1.0, 0.0).astype(BF16)
                kaug[h, rows, :] = jnp.where(head, kb, onehot)
                vaug_t[h, :HEAD_DIM, rows] = vt_ref[HEAD_DIM * h:HEAD_DIM * (h + 1), rows]
                vaug_t[h, HEAD_DIM:, rows] = jnp.ones((vaug_t.shape[1] - HEAD_DIM, blk), BF16)
                mean = jnp.sum(kb.astype(F32), axis=0, keepdims=True) * (1.0 / blk)
                kmean[h, pl.ds(j, 1), :] = jnp.where(head[:1], mean, 0.0)
                return carry

            lax.fori_loop(0, n_blocks, build, 0)
            km = kmean[h]
            hi = km.astype(BF16)
            kmean_hi[h] = hi
            kmean_lo[h] = (km - hi.astype(F32)).astype(BF16)

            head_id = 2 * hp + h
            far = relb_ref[REL_BUCKETS - 1, head_id]
            key = lax.broadcasted_iota(jnp.int32, (blk, blk), 0)
            qry = lax.broadcasted_iota(jnp.int32, (blk, blk), 1)
            for t, base in enumerate((0, blk)):
                dist = base + qry - key
                val = jnp.full((blk, blk), relb_ref[0, head_id] - far, F32)
                for kk, thr in enumerate(_BUCKET_THRESHOLDS, start=1):
                    val = jnp.where(dist >= thr, relb_ref[kk, head_id] - far, val)
                bias_t[h, t] = jnp.where(dist >= 0, val, MASK_VALUE)

    _prepare()
    block_id = lax.broadcasted_iota(jnp.int32, (n_slots, blk), 0)
    slots = (logits_a, logits_b)

    def block_rows(i, t):
        j = jnp.maximum(i - t, 0)
        return pl.ds(pl.multiple_of(j * blk, blk), blk)

    def compute_logits(i, t, slot, bias_tile, heads=(0, 1)):
        for h in heads:
            s = _dot(kaug[h, block_rows(i, t), :], qaug_t[i % 2, h])
            if bias_tile is not None:
                s = s + bias_t[h, bias_tile]
            slots[slot][h] = s.astype(BF16)
            smax[slot, h] = jnp.max(s, axis=0, keepdims=True).astype(BF16).astype(F32)

    def softmax_update(i, t, slot, heads=(0, 1)):
        for h in heads:
            s = slots[slot][h]
            m_old = mcol[h]
            m_new = jnp.maximum(m_old, smax[slot, h])
            p = jnp.exp(s - m_new.astype(BF16))
            acc_t[h] = (acc_t[h] * jnp.exp(m_old - m_new)
                        + _dot(vaug_t[h, :, block_rows(i, t)], p))
            mcol[h] = m_new

    def select_blocks(i):
        qt = qt_ref[:, pl.ds(pl.multiple_of(i * blk, blk), blk)]
        past = block_id < i
        for h in range(2):
            off = other_off(h)
            qs = jnp.where(in_head(dim, h), qt * 0.125, jnp.zeros_like(qt))
            gate = _dot(kmean_hi[h], qs) + _dot(kmean_lo[h], qs)
            gate = jnp.where(past, gate, -jnp.inf)
            chosen = jnp.zeros(gate.shape, jnp.bool_)
            for _ in range(MOBA_TOPK):
                top = jnp.max(gate, axis=0, keepdims=True)
                first = jnp.min(jnp.where(gate == top, block_id, n_slots), axis=0, keepdims=True)
                hit = block_id == first
                chosen = chosen | (hit & (top > -jnp.inf))
                gate = jnp.where(hit, -jnp.inf, gate)
            qaug_t[i % 2, h] = qs
            qaug_t[i % 2, h, off:off + n_slots, :] = jnp.where(
                past & jnp.logical_not(chosen), MASK_VALUE, 0.0).astype(BF16)

    def query_block(i, carry):
        n_steps = i + 1
        n_pairs = n_steps // 2
        odd = n_steps % 2 == 1
        mcol[...] = jnp.full(mcol.shape, -jnp.inf, F32)
        acc_t[...] = jnp.zeros(acc_t.shape, F32)

        def step_pair(pair, carry):
            t = 2 * pair
            for step, slot in ((t, 0), (t + 1, 1)):
                for h in range(2):
                    softmax_update(i, step, slot, (h,))
                    compute_logits(i, step + 2, slot, None, (h,))
            return carry

        n_full = jnp.maximum(n_pairs - 1, 0)

        def pairs_from(first, count):
            for k in range(count):
                step_pair(first + k, 0)

        def four_pairs(k, carry):
            pairs_from(4 * k, 4)
            return carry

        lax.fori_loop(0, n_full // 4, four_pairs, 0)

        @pl.when(n_full % 4 >= 2)
        def _two_more_pairs():
            pairs_from(n_full // 4 * 4, 2)

        @pl.when(n_full % 2 == 1)
        def _one_more_pair():
            pairs_from(n_full - 1, 1)

        t_last = 2 * (n_pairs - 1)

        nxt = jnp.where(i + 1 < n_blocks, i + 1, jnp.maximum(i - 1, 0))

        def update_then_next_logits(t, slot):
            for h in range(2):
                softmax_update(i, t, slot, (h,))
                compute_logits(nxt, slot, slot, slot, (h,))

        @pl.when((n_pairs >= 1) & jnp.logical_not(odd))
        def _last_pair():
            select_blocks(nxt)
            update_then_next_logits(t_last, 0)
            update_then_next_logits(t_last + 1, 1)

        @pl.when((n_pairs >= 1) & odd)
        def _last_pair_and_step():
            for h in range(2):
                softmax_update(i, t_last, 0, (h,))
                compute_logits(i, t_last + 2, 0, None, (h,))
            select_blocks(nxt)
            update_then_next_logits(t_last + 1, 1)
            update_then_next_logits(t_last + 2, 0)

        @pl.when(n_pairs == 0)
        def _only_step():
            select_blocks(nxt)
            update_then_next_logits(0, 0)
            compute_logits(nxt, 1, 1, 1)

        a0 = acc_t[0]
        a1 = acc_t[1]
        o_t = jnp.concatenate([a0[:HEAD_DIM] / a0[HEAD_DIM:HEAD_DIM + 1],
                               a1[:HEAD_DIM] / a1[HEAD_DIM:HEAD_DIM + 1]], axis=0)
        o_ref[pl.ds(pl.multiple_of(i * blk, blk), blk), :] = o_t.T.astype(o_ref.dtype)
        return carry

    select_blocks(0)
    compute_logits(0, 0, 0, 0)
    compute_logits(0, 1, 1, 1)
    lax.fori_loop(0, n_blocks, query_block, 0)


def _attention(proj3d, qv_t, rel_bias):
    b, s, _ = proj3d.shape
    n_blocks = s // MOBA_BLOCK
    n_slots = HEAD_DIM // 2
    assert s % MOBA_BLOCK == 0 and n_blocks <= n_slots
    blk = MOBA_BLOCK
    pv_rows = HEAD_DIM + 16
    scratch = [
        pltpu.VMEM((2, s, LANES), BF16),
        pltpu.VMEM((2, pv_rows, s), BF16),
        pltpu.VMEM((2, n_slots, LANES), F32),
        pltpu.VMEM((2, n_slots, LANES), BF16),
        pltpu.VMEM((2, n_slots, LANES), BF16),
        pltpu.VMEM((2, 2, blk, blk), F32),
        pltpu.VMEM((2, 2, LANES, blk), BF16),
        pltpu.VMEM((2, pv_rows, blk), F32),
        pltpu.VMEM((2, 1, blk), F32),
        pltpu.VMEM((2, blk, blk), BF16),
        pltpu.VMEM((2, blk, blk), BF16),
        pltpu.VMEM((2, 2, 1, blk), F32),
    ]
    vmem = (2 * s * LANES * 2 + 2 * pv_rows * s * 2
            + 2 * 4 * s * LANES * 2
            + 4 * blk * blk * 4 + 4 * blk * blk * 2 + (2 << 20))
    return pl.pallas_call(
        _attn_kernel,
        grid=(b, HEAD_PAIRS),
        in_specs=[
            pl.BlockSpec(memory_space=pltpu.SMEM),
            pl.BlockSpec((None, LANES, s), lambda bi, hp: (bi, hp, 0)),
            pl.BlockSpec((None, s, LANES), lambda bi, hp: (bi, 0, COL_K * HEAD_PAIRS + hp)),
            pl.BlockSpec((None, LANES, s), lambda bi, hp: (bi, HEAD_PAIRS + hp, 0)),
        ],
        out_specs=pl.BlockSpec((None, s, LANES), lambda bi, hp: (bi, 0, hp)),
        out_shape=jax.ShapeDtypeStruct((b, s, ATTN_WIDTH), BF16),
        scratch_shapes=scratch,
        compiler_params=pltpu.CompilerParams(
            dimension_semantics=("parallel", "parallel"),
            vmem_limit_bytes=_vmem_limit(vmem)),
        name="moba_attn",
    )(rel_bias, qv_t, proj3d, qv_t)


def _ssm_weights(a_re, a_im, log_dt, b_re, b_im, c_re, c_im, d_skip):
    g, p_states, c = b_re.shape
    big_l = SSM_CHUNK
    dt = jnp.exp(log_dt.astype(F32))[:, None]
    ar = a_re.astype(F32)
    ai = a_im.astype(F32)
    mag = jnp.exp(dt * ar)
    ang = dt * ai
    abar_re = mag * jnp.cos(ang)
    abar_im = mag * jnp.sin(ang)
    den = ar * ar + ai * ai
    nr = abar_re - 1.0
    ni = abar_im
    fr = (nr * ar + ni * ai) / den
    fi = (ni * ar - nr * ai) / den
    br = b_re.astype(F32)
    bi = b_im.astype(F32)
    bbar_re = fr[..., None] * br - fi[..., None] * bi
    bbar_im = fr[..., None] * bi + fi[..., None] * br
    n = jnp.arange(big_l + 1, dtype=F32)[:, None, None]
    pmag = jnp.exp(n * (dt * ar)[None])
    pw_re = pmag * jnp.cos(n * ang[None])
    pw_im = pmag * jnp.sin(n * ang[None])
    cr = c_re.astype(F32)
    ci = c_im.astype(F32)
    ab_re = pw_re[..., None] * bbar_re[None] - pw_im[..., None] * bbar_im[None]
    ab_im = pw_re[..., None] * bbar_im[None] + pw_im[..., None] * bbar_re[None]
    hi = lax.Precision.HIGHEST
    kern = (jnp.einsum('gdp,ngpc->ngcd', cr, ab_re[:big_l], precision=hi)
            - jnp.einsum('gdp,ngpc->ngcd', ci, ab_im[:big_l], precision=hi))
    lag = np.arange(big_l)[None, :] - np.arange(big_l)[:, None]
    place = jnp.asarray(lag[:, :, None] == np.arange(big_l), F32)
    toep = jnp.einsum('stn,ngcd->gsctd', place, kern, precision=hi)
    toep = toep.reshape(g, big_l * c, big_l * c)
    skip = jnp.tile(d_skip.astype(F32).reshape(g, 1, c), (1, big_l, 1)).reshape(g, big_l * c)
    toep = toep + skip[:, :, None] * jnp.eye(big_l * c, dtype=F32)[None]
    e_re = ab_re[:big_l][::-1].transpose(1, 0, 3, 2).reshape(g, big_l * c, p_states)
    e_im = ab_im[:big_l][::-1].transpose(1, 0, 3, 2).reshape(g, big_l * c, p_states)
    w_in = jnp.concatenate([e_re, e_im, e_im, e_re], axis=-1)
    ca_re = cr[None] * pw_re[1:, :, None, :] - ci[None] * pw_im[1:, :, None, :]
    ca_im = cr[None] * pw_im[1:, :, None, :] + ci[None] * pw_re[1:, :, None, :]
    o_re = ca_re.transpose(1, 3, 0, 2).reshape(g, p_states, big_l * c)
    o_im = -ca_im.transpose(1, 3, 0, 2).reshape(g, p_states, big_l * c)
    w_out = jnp.concatenate([o_re, o_im], axis=1)
    are, aim = pw_re[big_l], pw_im[big_l]
    carry = jnp.stack([jnp.concatenate([are, are], -1),
                       jnp.concatenate([-aim, aim], -1),
                       jnp.concatenate([aim, -aim], -1)], axis=1)
    return toep.astype(BF16), w_in.astype(BF16), w_out.astype(BF16), carry


def _ssm_kernel(u_ref, toep_ref, win_ref, wout_ref, carry_ref, y_ref, e_sc, prev_sc, *, batch):
    n_groups = u_ref.shape[0]
    n_chunks = u_ref.shape[1] // batch
    half = LANES
    coef = []
    for g in range(n_groups):
        e = _dot(u_ref[g], win_ref[g])
        e_sc[g, 0] = e[:, :half]
        e_sc[g, 1] = e[:, half:]
        coef.append([jnp.broadcast_to(carry_ref[g, r:r + 1, :], (batch, half)) for r in range(3)])

    def step(kk, state):
        rows = pl.ds(kk, batch, stride=n_chunks)
        out = []
        for g in range(n_groups):
            st, st_swapped = state[2 * g], state[2 * g + 1]
            a1, a2, a3 = coef[g]
            prev_sc[g, rows, :] = st
            out.append(a1 * st + a2 * st_swapped + e_sc[g, 0, rows, :])
            out.append(a1 * st_swapped + a3 * st + e_sc[g, 1, rows, :])
        return tuple(out)

    zero = jnp.zeros((batch, half), F32)
    lax.fori_loop(0, n_chunks, step, (zero,) * (2 * n_groups), unroll=8)
    for g in range(n_groups):
        y = _dot(u_ref[g], toep_ref[g]) + _dot(prev_sc[g].astype(BF16), wout_ref[g])
        y_ref[g] = y.astype(y_ref.dtype)


def _ssm(u_t, toep, w_in, w_out, carry, *, batch, groups_per_step=2):
    g, rows, width = u_t.shape
    gps = groups_per_step
    vmem = gps * (2 * 2 * rows * width * 2 + rows * width * 4 + rows * LANES * 4 + 2 * rows * width * 4)

    def per_step(*tail):
        return pl.BlockSpec((gps,) + tail, lambda gi: (gi,) + (0,) * len(tail))

    return pl.pallas_call(
        functools.partial(_ssm_kernel, batch=batch),
        grid=(g // gps,),
        in_specs=[per_step(rows, width), per_step(width, width), per_step(width, width),
                  per_step(LANES, width), per_step(3, LANES)],
        out_specs=per_step(rows, width),
        out_shape=jax.ShapeDtypeStruct((g, rows, width), BF16),
        scratch_shapes=[pltpu.VMEM((gps, 2, rows, LANES), F32), pltpu.VMEM((gps, rows, LANES), F32)],
        compiler_params=pltpu.CompilerParams(
            dimension_semantics=("parallel",), vmem_limit_bytes=_vmem_limit(vmem)),
        name="s5_ssm",
    )(u_t, toep, w_in, w_out, carry)


def _final_kernel(x_ref, p_ref, oa_ref, yf_ref, za_ref, zs_ref, ga_ref, gs_ref,
                  wap_ref, wglu_ref, wsp_ref, wout_ref, wpg_ref, wpp_ref, lng_ref, lnb_ref, o_ref, ys_sc,
                  *, sub_rows):
    for sub in range(x_ref.shape[0] // sub_rows):
        _final_rows(sub, sub_rows, x_ref, p_ref, oa_ref, yf_ref, za_ref, zs_ref, ga_ref, gs_ref,
                    wap_ref, wglu_ref, wsp_ref, wout_ref, wpg_ref, wpp_ref, lng_ref, lnb_ref, o_ref, ys_sc)


def _final_rows(sub, sub_rows, x_ref, p_ref, oa_ref, yf_ref, za_ref, zs_ref, ga_ref, gs_ref,
                wap_ref, wglu_ref, wsp_ref, wout_ref, wpg_ref, wpp_ref, lng_ref, lnb_ref, o_ref, ys_sc):
    rows = slice(sub * sub_rows, (sub + 1) * sub_rows)
    n_chunks = sub_rows // SSM_CHUNK
    chunks = slice(sub * n_chunks, (sub + 1) * n_chunks)
    half = wpg_ref.shape[1] // 2
    x = x_ref[rows, :]
    xb = x.astype(BF16)
    za = za_ref[rows, :]
    a_in = oa_ref[rows, :] * (za * _sigmoid(za))
    y_a = _dot(a_in, wap_ref[...])
    gate_lo = _dot(xb, wpg_ref[:, :half])
    groups_per_tile = LANES // SSM_GROUP
    folded = [yf_ref[g, chunks, :].astype(F32) for g in range(SSM_GROUPS)]
    for t in range(SSM_CHUNK):
        for gb in range(ys_sc.shape[1]):
            tile = jnp.concatenate(
                [folded[gb * groups_per_tile + gl][:, t * SSM_GROUP:(t + 1) * SSM_GROUP]
                 for gl in range(groups_per_tile)], axis=1)
            ys_sc[sub, gb, pl.ds(t, n_chunks, stride=SSM_CHUNK), :] = tile
    ys = jnp.concatenate([ys_sc[sub, gb] for gb in range(ys_sc.shape[1])], axis=1)
    gelu = 0.5 * ys * (1.0 + lax.erf(ys * (2.0 ** -0.5)))
    glu = _dot(gelu.astype(BF16), wglu_ref[...])
    gate_hi = _dot(xb, wpg_ref[:, half:])
    zs = zs_ref[rows, :]
    s_in = glu[:, :SSM_WIDTH] * _sigmoid(glu[:, SSM_WIDTH:]) * (zs * _sigmoid(zs)).astype(F32)
    y_s = _dot(s_in.astype(BF16), wsp_ref[...])
    emb = _dot(p_ref[rows, :].astype(BF16), wpp_ref[...])
    merge = _sigmoid(ga_ref[rows, :]) * y_a.astype(BF16) + _sigmoid(gs_ref[rows, :]) * y_s.astype(BF16)
    mix = _dot(merge, wout_ref[...])
    ple = _sigmoid(jnp.concatenate([gate_lo, gate_hi], axis=1)) * emb
    hsum = DEEPNORM_ALPHA * x + mix + ple
    mu = jnp.mean(hsum, axis=-1, keepdims=True)
    cen = hsum - mu
    var = jnp.mean(cen * cen, axis=-1, keepdims=True)
    o_ref[rows, :] = cen * lax.rsqrt(var + LN_EPS) * lng_ref[...] + lnb_ref[...]


def _final(x2d, p2d, proj, o_a, y_fold, w_ap, w_glu, w_sp, w_out, w_pg, w_pp, ln_g, ln_b, *,
           row_tile=512, sub_rows=256):
    m = x2d.shape[0]
    half, full = SSM_WIDTH, D_MODEL

    def rows(width, col):
        return pl.BlockSpec((row_tile, width), lambda i: (i, col))

    def whole(arr):
        return pl.BlockSpec(arr.shape, lambda i: (0, 0))

    weights = (w_ap, w_glu, w_sp, w_out, w_pg, w_pp, ln_g, ln_b)
    vmem = (2 * sum(int(np.prod(w.shape)) * w.dtype.itemsize for w in weights)
            + 2 * row_tile * (2 * full * 4 + PLE_DIM * 4 + (4 * half + 2 * full) * 2)
            + 12 * row_tile * full * 4)
    return pl.pallas_call(
        functools.partial(_final_kernel, sub_rows=sub_rows),
        grid=(m // row_tile,),
        in_specs=[rows(full, 0), rows(PLE_DIM, 0), rows(half, 0),
                  pl.BlockSpec((SSM_GROUPS, row_tile // SSM_CHUNK, SSM_CHUNK * SSM_GROUP), lambda i: (0, i, 0)),
                  rows(half, COL_ZA), rows(half, COL_ZS), rows(full, COL_GA), rows(full, COL_GS)]
                 + [whole(w) for w in weights],
        out_specs=rows(full, 0),
        out_shape=jax.ShapeDtypeStruct((m, full), F32),
        scratch_shapes=[pltpu.VMEM((row_tile // sub_rows, half // LANES, sub_rows, LANES), F32)],
        compiler_params=pltpu.CompilerParams(
            dimension_semantics=("parallel",), vmem_limit_bytes=_vmem_limit(vmem)),
        name="final",
    )(x2d, p2d, o_a, y_fold, proj, proj, proj, proj, *weights)


def kernel(x, p, w_in, w_attn_proj, w_ssm_proj, w_out, ssm_a_re, ssm_a_im, ssm_log_dt, ssm_b_re, ssm_b_im, ssm_c_re, ssm_c_im, ssm_d, w_glu, w_ple_gate, w_ple_proj, ln_g, ln_b, rel_bias):
    b, s, d = x.shape
    m = b * s
    for i in range(w_in.shape[0]):
        x2d = x.reshape(m, d)
        wq, wk, wv, wza, wu, wzs, wga, wgs = jnp.split(w_in[i].astype(BF16), _IN_SPLITS, axis=1)
        w_main = jnp.concatenate([wga, wgs, wk, wza, wzs], axis=1)
        proj, qv_t, u_fold = _proj(x2d, w_main, jnp.concatenate([wq, wv], axis=1).T, wu, batch=b)
        o_a = _attention(proj.reshape(b, s, PROJ_WIDTH), qv_t, rel_bias.astype(F32))
        toep, s_in, s_out, carry = _ssm_weights(
            ssm_a_re[i], ssm_a_im[i], ssm_log_dt[i], ssm_b_re[i], ssm_b_im[i],
            ssm_c_re[i], ssm_c_im[i], ssm_d[i].reshape(SSM_GROUPS, SSM_GROUP))
        y_fold = _ssm(u_fold, toep, s_in, s_out, carry, batch=b)
        x2d = _final(x2d, p[i].reshape(m, PLE_DIM), proj, o_a.reshape(m, ATTN_WIDTH), y_fold,
                     w_attn_proj[i].astype(BF16), w_glu[i].astype(BF16), w_ssm_proj[i].astype(BF16),
                     w_out[i].astype(BF16), w_ple_gate[i].astype(BF16), w_ple_proj[i].astype(BF16),
                     ln_g[i].astype(F32).reshape(1, d), ln_b[i].astype(F32).reshape(1, d))
        x = x2d.reshape(b, s, d)
    return x
```

```python
import functools
import math

import numpy as np
import jax
import jax.numpy as jnp
from jax import lax
from jax.experimental import pallas as pl
from jax.experimental.pallas import tpu as pltpu

F32 = jnp.float32
BF16 = jnp.bfloat16

LANES = 128
BF16_SUBLANES = 16
V7X_VMEM_BYTES = 64 * 1024 * 1024

D_MODEL = 1024
PLE_DIM = 256
HEADS = 8
HEAD_DIM = 64
QUERY_SCALE = HEAD_DIM ** -0.5
ATTN_WIDTH = HEADS * HEAD_DIM
HEAD_PAIRS = ATTN_WIDTH // LANES
MOBA_BLOCK = 256
MOBA_TOPK = 3
REL_BUCKETS = 32
REL_MAX_DIST = 128
SSM_WIDTH = 512
SSM_GROUP = 16
SSM_GROUPS = SSM_WIDTH // SSM_GROUP
SSM_STATE = 64
SSM_CHUNK = 16
IN_WIDTH = 4 * ATTN_WIDTH + 2 * SSM_WIDTH + 2 * D_MODEL
DEPTH = 1
DEEPNORM_ALPHA = (2.0 * DEPTH) ** 0.25
LN_EPS = 1e-5
MASK_VALUE = -1e30

PROJ_WIDTH = IN_WIDTH - 2 * ATTN_WIDTH - SSM_WIDTH
_IN_SPLITS = tuple(int(v) for v in np.cumsum(
    (ATTN_WIDTH,) * 4 + (SSM_WIDTH,) * 2 + (D_MODEL,) * 2)[:-1])
COL_GA, COL_GS = 0, 1
COL_K, COL_ZA, COL_ZS = 4, 5, 6


def _dot(a, b):
    return jnp.dot(a, b, preferred_element_type=F32)


def _dot_nt(a, b):
    return lax.dot_general(a, b, (((1,), (1,)), ((), ())), preferred_element_type=F32)


def _sigmoid(v):
    return 1.0 / (1.0 + jnp.exp(-v))


def _vmem_limit(nbytes):
    return int(min(V7X_VMEM_BYTES - (4 << 20), max(nbytes + (8 << 20), 32 << 20)))


def _proj_kernel(x_ref, w_ref, wt_ref, wu_ref, o_ref, t_ref, uf_ref, u_sc, *, col_tile):
    xb = x_ref[...].astype(BF16)
    u = _dot(xb, wu_ref[...])
    for gb in range(u_sc.shape[0]):
        u_sc[gb] = u[:, gb * LANES:(gb + 1) * LANES]
    n_cols = o_ref.shape[1]
    for start in range(0, n_cols, col_tile):
        cols = slice(start, min(start + col_tile, n_cols))
        o_ref[:, cols] = _dot(xb, w_ref[:, cols]).astype(BF16)
    t_ref[...] = _dot_nt(wt_ref[...], xb).astype(BF16)
    n_chunks = u_sc.shape[1] // SSM_CHUNK
    groups_per_tile = LANES // SSM_GROUP
    steps = [[u_sc[gb, pl.ds(t, n_chunks, stride=SSM_CHUNK), :]
              for gb in range(u_sc.shape[0])] for t in range(SSM_CHUNK)]
    for g in range(SSM_GROUPS):
        gb, lo = g // groups_per_tile, (g % groups_per_tile) * SSM_GROUP
        folded = jnp.concatenate([steps[t][gb][:, lo:lo + SSM_GROUP] for t in range(SSM_CHUNK)], axis=1)
        uf_ref[g] = folded.astype(BF16)


def _proj(x2d, w_bf16, wt_bf16, wu_bf16, *, batch, row_tile=512, col_tile=1024):
    m, k = x2d.shape
    n = w_bf16.shape[1]
    nt = wt_bf16.shape[0]
    nu = wu_bf16.shape[1]
    tiles_per_batch = m // batch // row_tile
    fold_rows = row_tile // SSM_CHUNK
    vmem = (2 * row_tile * k * 4 + 2 * k * (n + nt + nu) * 2 + 2 * row_tile * (n + nt + nu) * 2
            + row_tile * k * 2 + 2 * row_tile * col_tile * 4 + row_tile * nu * 4)
    return pl.pallas_call(
        functools.partial(_proj_kernel, col_tile=col_tile),
        grid=(m // row_tile,),
        in_specs=[pl.BlockSpec((row_tile, k), lambda i: (i, 0)),
                  pl.BlockSpec((k, n), lambda i: (0, 0)),
                  pl.BlockSpec((nt, k), lambda i: (0, 0)),
                  pl.BlockSpec((k, nu), lambda i: (0, 0))],
        out_specs=[pl.BlockSpec((row_tile, n), lambda i: (i, 0)),
                   pl.BlockSpec((None, nt, row_tile),
                                lambda i: (i // tiles_per_batch, 0, i % tiles_per_batch)),
                   pl.BlockSpec((SSM_GROUPS, fold_rows, SSM_CHUNK * SSM_GROUP), lambda i: (0, i, 0))],
        out_shape=[jax.ShapeDtypeStruct((m, n), BF16),
                   jax.ShapeDtypeStruct((batch, nt, m // batch), BF16),
                   jax.ShapeDtypeStruct((SSM_GROUPS, m // SSM_CHUNK, SSM_CHUNK * SSM_GROUP), BF16)],
        scratch_shapes=[pltpu.VMEM((nu // LANES, row_tile, LANES), F32)],
        compiler_params=pltpu.CompilerParams(
            dimension_semantics=("parallel",), vmem_limit_bytes=_vmem_limit(vmem)),
        name="proj",
    )(x2d, w_bf16, wt_bf16, wu_bf16)


def _t5_bucket_thresholds():
    max_exact = REL_BUCKETS // 2
    dist = np.arange(0, 2 * MOBA_BLOCK, dtype=np.int32)
    d = np.maximum(dist, 1).astype(np.float32)
    large = max_exact + (np.log(d / np.float32(max_exact)) / np.float32(math.log(REL_MAX_DIST / max_exact))
                         * np.float32(REL_BUCKETS - max_exact)).astype(np.int32)
    large = np.minimum(large, REL_BUCKETS - 1)
    bucket = np.where(dist < max_exact, dist, large)
    assert np.all(np.diff(bucket) >= 0) and bucket[-1] == REL_BUCKETS - 1
    return [int(np.argmax(bucket >= k)) for k in range(1, REL_BUCKETS)]


_BUCKET_THRESHOLDS = _t5_bucket_thresholds()


def _attn_kernel(relb_ref, qt_ref, k_ref, vt_ref, o_ref,
                 kaug, vaug_t, kmean, kmean_hi, kmean_lo, bias_t, qaug_t, acc_t, mcol, logits_a, logits_b, smax):
    hp = pl.program_id(1)
    blk = MOBA_BLOCK
    qtile = 2 * blk
    n_tiles = k_ref.shape[0] // qtile
    n_slots = kmean.shape[1]
    lane = lax.broadcasted_iota(jnp.int32, (blk, LANES), 1)
    dim = lax.broadcasted_iota(jnp.int32, (LANES, qtile), 0)

    def in_head(index, h):
        return (index >= HEAD_DIM * h) & (index < HEAD_DIM * (h + 1))

    def other_off(h):
        return HEAD_DIM * (1 - h)

    def _prepare():
        kmean[...] = jnp.zeros(kmean.shape, F32)
        for h in range(2):
            head = in_head(lane, h)
            off = other_off(h)

            def build(j, carry, h=h, head=head, off=off):
                rows = pl.ds(pl.multiple_of(j * blk, blk), blk)
                kb = k_ref[rows, :]
                onehot = jnp.where(lane - off == j, 1.0, 0.0).astype(BF16)
                kaug[h, rows, :] = jnp.where(head, kb, onehot)
                vaug_t[h, :HEAD_DIM, rows] = vt_ref[HEAD_DIM * h:HEAD_DIM * (h + 1), rows]
                vaug_t[h, HEAD_DIM:, rows] = jnp.ones((vaug_t.shape[1] - HEAD_DIM, blk), BF16)
                mean = jnp.sum(kb.astype(F32), axis=0, keepdims=True) * (1.0 / blk)
                kmean[h, pl.ds(j, 1), :] = jnp.where(head[:1], mean, 0.0)
                return carry

            lax.fori_loop(0, 2 * n_tiles, build, 0)
            km = kmean[h]
            hi = km.astype(BF16)
            kmean_hi[h] = hi
            kmean_lo[h] = (km - hi.astype(F32)).astype(BF16)

            head_id = 2 * hp + h
            far = relb_ref[REL_BUCKETS - 1, head_id]
            key = lax.broadcasted_iota(jnp.int32, (blk, qtile), 0)
            qry = lax.broadcasted_iota(jnp.int32, (blk, qtile), 1)
            for t, key_start in enumerate((blk, 0, -blk)):
                dist = qry - (key + key_start)
                val = jnp.full((blk, qtile), relb_ref[0, head_id] - far, F32)
                for kk, thr in enumerate(_BUCKET_THRESHOLDS, start=1):
                    val = jnp.where(dist >= thr, relb_ref[kk, head_id] - far, val)
                bias_t[h, t] = jnp.where(dist >= 0, val, MASK_VALUE)

    _prepare()
    block_id = lax.broadcasted_iota(jnp.int32, (n_slots, qtile), 0)
    upper = (lax.broadcasted_iota(jnp.int32, (n_slots, qtile), 1) >= blk).astype(jnp.int32)
    slots = (logits_a, logits_b)

    def block_rows(j, t):
        b_idx = jnp.maximum(2 * j + 1 - t, 0)
        return pl.ds(pl.multiple_of(b_idx * blk, blk), blk)

    def compute_logits(j, t, slot, bias_tile, heads=(0, 1)):
        for h in heads:
            s = _dot(kaug[h, block_rows(j, t), :], qaug_t[j % 2, h])
            if bias_tile is not None:
                s = s + bias_t[h, bias_tile]
            slots[slot][h] = s.astype(BF16)
            smax[slot, h] = jnp.max(s, axis=0, keepdims=True).astype(BF16).astype(F32)

    def softmax_update(j, t, slot, heads=(0, 1)):
        for h in heads:
            s = slots[slot][h]
            m_old = mcol[h]
            m_new = jnp.maximum(m_old, smax[slot, h])
            p = jnp.exp(s - m_new.astype(BF16))
            acc_t[h] = (acc_t[h] * jnp.exp(m_old - m_new)
                        + _dot(vaug_t[h, :, block_rows(j, t)], p))
            mcol[h] = m_new

    def select_blocks(j):
        qt = qt_ref[:, pl.ds(pl.multiple_of(j * qtile, qtile), qtile)]
        past = block_id < 2 * j + upper
        for h in range(2):
            off = other_off(h)
            qs = jnp.where(in_head(dim, h), qt * QUERY_SCALE, jnp.zeros_like(qt))
            gate = _dot(kmean_hi[h], qs) + _dot(kmean_lo[h], qs)
            gate = jnp.where(past, gate, -jnp.inf)
            chosen = jnp.zeros(gate.shape, jnp.bool_)
            for _ in range(MOBA_TOPK):
                top = jnp.max(gate, axis=0, keepdims=True)
                first = jnp.min(jnp.where(gate == top, block_id, n_slots), axis=0, keepdims=True)
                hit = block_id == first
                chosen = chosen | (hit & (top > -jnp.inf))
                gate = jnp.where(hit, -jnp.inf, gate)
            qaug_t[j % 2, h] = qs
            qaug_t[j % 2, h, off:off + n_slots, :] = jnp.where(
                past & jnp.logical_not(chosen), MASK_VALUE, 0.0).astype(BF16)

    def query_tile(j, carry):
        n_pairs = j + 1
        mcol[...] = jnp.full(mcol.shape, -jnp.inf, F32)
        acc_t[...] = jnp.zeros(acc_t.shape, F32)

        def step_pair(pair, third_step_bias=None):
            t = 2 * pair
            for step, slot, bias_tile in ((t, 0, third_step_bias), (t + 1, 1, None)):
                for h in range(2):
                    softmax_update(j, step, slot, (h,))
                    compute_logits(j, step + 2, slot, bias_tile, (h,))

        @pl.when(n_pairs >= 2)
        def _first_pair():
            step_pair(0, 2)

        n_mid = jnp.maximum(n_pairs - 2, 0)

        def pairs_from(first, count):
            for k in range(count):
                step_pair(first + k)

        def four_pairs(k, carry):
            pairs_from(1 + 4 * k, 4)
            return carry

        lax.fori_loop(0, n_mid // 4, four_pairs, 0)

        @pl.when(n_mid % 4 >= 2)
        def _two_more_pairs():
            pairs_from(1 + n_mid // 4 * 4, 2)

        @pl.when(n_mid % 2 == 1)
        def _one_more_pair():
            pairs_from(n_mid, 1)

        nxt = jnp.where(j + 1 < n_tiles, j + 1, jnp.maximum(j - 1, 0))
        t_last = 2 * (n_pairs - 1)
        select_blocks(nxt)
        for slot in range(2):
            for h in range(2):
                softmax_update(j, t_last + slot, slot, (h,))
                compute_logits(nxt, slot, slot, slot, (h,))

        a0 = acc_t[0]
        a1 = acc_t[1]
        o_t = jnp.concatenate([a0[:HEAD_DIM] / a0[HEAD_DIM:HEAD_DIM + 1],
                               a1[:HEAD_DIM] / a1[HEAD_DIM:HEAD_DIM + 1]], axis=0)
        o_ref[pl.ds(pl.multiple_of(j * qtile, qtile), qtile), :] = o_t.T.astype(o_ref.dtype)
        return carry

    select_blocks(0)
    compute_logits(0, 0, 0, 0)
    compute_logits(0, 1, 1, 1)
    lax.fori_loop(0, n_tiles, query_tile, 0)


def _attention(proj3d, qv_t, rel_bias):
    b, s, _ = proj3d.shape
    blk = MOBA_BLOCK
    qtile = 2 * blk
    n_slots = HEAD_DIM // 2
    assert s % qtile == 0 and s // blk <= n_slots
    pv_rows = HEAD_DIM + BF16_SUBLANES
    scratch = [
        pltpu.VMEM((2, s, LANES), BF16),
        pltpu.VMEM((2, pv_rows, s), BF16),
        pltpu.VMEM((2, n_slots, LANES), F32),
        pltpu.VMEM((2, n_slots, LANES), BF16),
        pltpu.VMEM((2, n_slots, LANES), BF16),
        pltpu.VMEM((2, 3, blk, qtile), F32),
        pltpu.VMEM((2, 2, LANES, qtile), BF16),
        pltpu.VMEM((2, pv_rows, qtile), F32),
        pltpu.VMEM((2, 1, qtile), F32),
        pltpu.VMEM((2, blk, qtile), BF16),
        pltpu.VMEM((2, blk, qtile), BF16),
        pltpu.VMEM((2, 2, 1, qtile), F32),
    ]
    vmem = (2 * s * LANES * 2 + 2 * pv_rows * s * 2
            + 2 * 4 * s * LANES * 2
            + 6 * blk * qtile * 4 + 4 * blk * qtile * 2 + 8 * blk * qtile * 4)
    return pl.pallas_call(
        _attn_kernel,
        grid=(b, HEAD_PAIRS),
        in_specs=[
            pl.BlockSpec(memory_space=pltpu.SMEM),
            pl.BlockSpec((None, LANES, s), lambda bi, hp: (bi, hp, 0)),
            pl.BlockSpec((None, s, LANES), lambda bi, hp: (bi, 0, COL_K * HEAD_PAIRS + hp)),
            pl.BlockSpec((None, LANES, s), lambda bi, hp: (bi, HEAD_PAIRS + hp, 0)),
        ],
        out_specs=pl.BlockSpec((None, s, LANES), lambda bi, hp: (bi, 0, hp)),
        out_shape=jax.ShapeDtypeStruct((b, s, ATTN_WIDTH), BF16),
        scratch_shapes=scratch,
        compiler_params=pltpu.CompilerParams(
            dimension_semantics=("parallel", "parallel"),
            vmem_limit_bytes=_vmem_limit(vmem)),
        name="moba_attn",
    )(rel_bias, qv_t, proj3d, qv_t)


def _ssm_weights(a_re, a_im, log_dt, b_re, b_im, c_re, c_im, d_skip):
    g, p_states, c = b_re.shape
    big_l = SSM_CHUNK
    dt = jnp.exp(log_dt.astype(F32))[:, None]
    ar = a_re.astype(F32)
    ai = a_im.astype(F32)
    mag = jnp.exp(dt * ar)
    ang = dt * ai
    abar_re = mag * jnp.cos(ang)
    abar_im = mag * jnp.sin(ang)
    den = ar * ar + ai * ai
    nr = abar_re - 1.0
    ni = abar_im
    fr = (nr * ar + ni * ai) / den
    fi = (ni * ar - nr * ai) / den
    br = b_re.astype(F32)
    bi = b_im.astype(F32)
    bbar_re = fr[..., None] * br - fi[..., None] * bi
    bbar_im = fr[..., None] * bi + fi[..., None] * br
    n = jnp.arange(big_l + 1, dtype=F32)[:, None, None]
    pmag = jnp.exp(n * (dt * ar)[None])
    pw_re = pmag * jnp.cos(n * ang[None])
    pw_im = pmag * jnp.sin(n * ang[None])
    cr = c_re.astype(F32)
    ci = c_im.astype(F32)
    ab_re = pw_re[..., None] * bbar_re[None] - pw_im[..., None] * bbar_im[None]
    ab_im = pw_re[..., None] * bbar_im[None] + pw_im[..., None] * bbar_re[None]
    hi = lax.Precision.HIGHEST
    kern = (jnp.einsum('gdp,ngpc->ngcd', cr, ab_re[:big_l], precision=hi)
            - jnp.einsum('gdp,ngpc->ngcd', ci, ab_im[:big_l], precision=hi))
    lag = np.arange(big_l)[None, :] - np.arange(big_l)[:, None]
    place = jnp.asarray(lag[:, :, None] == np.arange(big_l), F32)
    toep = jnp.einsum('stn,ngcd->gsctd', place, kern, precision=hi)
    toep = toep.reshape(g, big_l * c, big_l * c)
    skip = jnp.tile(d_skip.astype(F32).reshape(g, 1, c), (1, big_l, 1)).reshape(g, big_l * c)
    toep = toep + skip[:, :, None] * jnp.eye(big_l * c, dtype=F32)[None]
    e_re = ab_re[:big_l][::-1].transpose(1, 0, 3, 2).reshape(g, big_l * c, p_states)
    e_im = ab_im[:big_l][::-1].transpose(1, 0, 3, 2).reshape(g, big_l * c, p_states)
    w_in = jnp.concatenate([e_re, e_im, e_im, e_re], axis=-1)
    ca_re = cr[None] * pw_re[1:, :, None, :] - ci[None] * pw_im[1:, :, None, :]
    ca_im = cr[None] * pw_im[1:, :, None, :] + ci[None] * pw_re[1:, :, None, :]
    o_re = ca_re.transpose(1, 3, 0, 2).reshape(g, p_states, big_l * c)
    o_im = -ca_im.transpose(1, 3, 0, 2).reshape(g, p_states, big_l * c)
    w_out = jnp.concatenate([o_re, o_im], axis=1)
    are, aim = pw_re[big_l], pw_im[big_l]
    carry = jnp.stack([jnp.concatenate([are, are], -1),
                       jnp.concatenate([-aim, aim], -1),
                       jnp.concatenate([aim, -aim], -1)], axis=1)
    return toep.astype(BF16), w_in.astype(BF16), w_out.astype(BF16), carry


def _ssm_kernel(u_ref, toep_ref, win_ref, wout_ref, carry_ref, y_ref, e_sc, prev_sc, *, batch):
    n_groups = u_ref.shape[0]
    n_chunks = u_ref.shape[1] // batch
    half = LANES
    coef = []
    for g in range(n_groups):
        e = _dot(u_ref[g], win_ref[g])
        e_sc[g, 0] = e[:, :half]
        e_sc[g, 1] = e[:, half:]
        coef.append([jnp.broadcast_to(carry_ref[g, r:r + 1, :], (batch, half)) for r in range(3)])

    def step(kk, state):
        rows = pl.ds(kk, batch, stride=n_chunks)
        out = []
        for g in range(n_groups):
            st, st_swapped = state[2 * g], state[2 * g + 1]
            a1, a2, a3 = coef[g]
            prev_sc[g, rows, :] = st
            out.append(a1 * st + a2 * st_swapped + e_sc[g, 0, rows, :])
            out.append(a1 * st_swapped + a3 * st + e_sc[g, 1, rows, :])
        return tuple(out)

    zero = jnp.zeros((batch, half), F32)
    lax.fori_loop(0, n_chunks, step, (zero,) * (2 * n_groups), unroll=8)
    for g in range(n_groups):
        y = _dot(u_ref[g], toep_ref[g]) + _dot(prev_sc[g].astype(BF16), wout_ref[g])
        y_ref[g] = y.astype(y_ref.dtype)


def _ssm(u_t, toep, w_in, w_out, carry, *, batch, groups_per_step=2):
    g, rows, width = u_t.shape
    gps = groups_per_step
    vmem = gps * (2 * 2 * rows * width * 2 + rows * width * 4 + rows * LANES * 4 + 2 * rows * width * 4)

    def per_step(*tail):
        return pl.BlockSpec((gps,) + tail, lambda gi: (gi,) + (0,) * len(tail))

    return pl.pallas_call(
        functools.partial(_ssm_kernel, batch=batch),
        grid=(g // gps,),
        in_specs=[per_step(rows, width), per_step(width, width), per_step(width, width),
                  per_step(LANES, width), per_step(3, LANES)],
        out_specs=per_step(rows, width),
        out_shape=jax.ShapeDtypeStruct((g, rows, width), BF16),
        scratch_shapes=[pltpu.VMEM((gps, 2, rows, LANES), F32), pltpu.VMEM((gps, rows, LANES), F32)],
        compiler_params=pltpu.CompilerParams(
            dimension_semantics=("parallel",), vmem_limit_bytes=_vmem_limit(vmem)),
        name="s5_ssm",
    )(u_t, toep, w_in, w_out, carry)


def _final_kernel(x_ref, p_ref, oa_ref, yf_ref, za_ref, zs_ref, ga_ref, gs_ref,
                  wap_ref, wglu_ref, wsp_ref, wout_ref, wpg_ref, wpp_ref, lng_ref, lnb_ref, o_ref, ys_sc,
                  *, sub_rows):
    for sub in range(x_ref.shape[0] // sub_rows):
        _final_rows(sub, sub_rows, x_ref, p_ref, oa_ref, yf_ref, za_ref, zs_ref, ga_ref, gs_ref,
                    wap_ref, wglu_ref, wsp_ref, wout_ref, wpg_ref, wpp_ref, lng_ref, lnb_ref, o_ref, ys_sc)


def _final_rows(sub, sub_rows, x_ref, p_ref, oa_ref, yf_ref, za_ref, zs_ref, ga_ref, gs_ref,
                wap_ref, wglu_ref, wsp_ref, wout_ref, wpg_ref, wpp_ref, lng_ref, lnb_ref, o_ref, ys_sc):
    rows = slice(sub * sub_rows, (sub + 1) * sub_rows)
    n_chunks = sub_rows // SSM_CHUNK
    chunks = slice(sub * n_chunks, (sub + 1) * n_chunks)
    half = wpg_ref.shape[1] // 2
    x = x_ref[rows, :]
    xb = x.astype(BF16)
    za = za_ref[rows, :]
    a_in = oa_ref[rows, :] * (za * _sigmoid(za))
    y_a = _dot(a_in, wap_ref[...])
    gate_lo = _dot(xb, wpg_ref[:, :half])
    groups_per_tile = LANES // SSM_GROUP
    folded = [yf_ref[g, chunks, :].astype(F32) for g in range(SSM_GROUPS)]
    for t in range(SSM_CHUNK):
        for gb in range(ys_sc.shape[1]):
            tile = jnp.concatenate(
                [folded[gb * groups_per_tile + gl][:, t * SSM_GROUP:(t + 1) * SSM_GROUP]
                 for gl in range(groups_per_tile)], axis=1)
            ys_sc[sub, gb, pl.ds(t, n_chunks, stride=SSM_CHUNK), :] = tile
    ys = jnp.concatenate([ys_sc[sub, gb] for gb in range(ys_sc.shape[1])], axis=1)
    gelu = 0.5 * ys * (1.0 + lax.erf(ys * (2.0 ** -0.5)))
    glu = _dot(gelu.astype(BF16), wglu_ref[...])
    gate_hi = _dot(xb, wpg_ref[:, half:])
    zs = zs_ref[rows, :]
    s_in = glu[:, :SSM_WIDTH] * _sigmoid(glu[:, SSM_WIDTH:]) * (zs * _sigmoid(zs)).astype(F32)
    y_s = _dot(s_in.astype(BF16), wsp_ref[...])
    emb = _dot(p_ref[rows, :].astype(BF16), wpp_ref[...])
    merge = _sigmoid(ga_ref[rows, :]) * y_a.astype(BF16) + _sigmoid(gs_ref[rows, :]) * y_s.astype(BF16)
    mix = _dot(merge, wout_ref[...])
    ple = _sigmoid(jnp.concatenate([gate_lo, gate_hi], axis=1)) * emb
    hsum = DEEPNORM_ALPHA * x + mix + ple
    mu = jnp.mean(hsum, axis=-1, keepdims=True)
    cen = hsum - mu
    var = jnp.mean(cen * cen, axis=-1, keepdims=True)
    o_ref[rows, :] = cen * lax.rsqrt(var + LN_EPS) * lng_ref[...] + lnb_ref[...]


def _final(x2d, p2d, proj, o_a, y_fold, w_ap, w_glu, w_sp, w_out, w_pg, w_pp, ln_g, ln_b, *,
           row_tile=512, sub_rows=256):
    m = x2d.shape[0]
    half, full = SSM_WIDTH, D_MODEL

    def rows(width, col):
        return pl.BlockSpec((row_tile, width), lambda i: (i, col))

    def whole(arr):
        return pl.BlockSpec(arr.shape, lambda i: (0, 0))

    weights = (w_ap, w_glu, w_sp, w_out, w_pg, w_pp, ln_g, ln_b)
    vmem = (2 * sum(int(np.prod(w.shape)) * w.dtype.itemsize for w in weights)
            + 2 * row_tile * (2 * full * 4 + PLE_DIM * 4 + (4 * half + 2 * full) * 2)
            + 12 * row_tile * full * 4)
    return pl.pallas_call(
        functools.partial(_final_kernel, sub_rows=sub_rows),
        grid=(m // row_tile,),
        in_specs=[rows(full, 0), rows(PLE_DIM, 0), rows(half, 0),
                  pl.BlockSpec((SSM_GROUPS, row_tile // SSM_CHUNK, SSM_CHUNK * SSM_GROUP), lambda i: (0, i, 0)),
                  rows(half, COL_ZA), rows(half, COL_ZS), rows(full, COL_GA), rows(full, COL_GS)]
                 + [whole(w) for w in weights],
        out_specs=rows(full, 0),
        out_shape=jax.ShapeDtypeStruct((m, full), F32),
        scratch_shapes=[pltpu.VMEM((row_tile // sub_rows, half // LANES, sub_rows, LANES), F32)],
        compiler_params=pltpu.CompilerParams(
            dimension_semantics=("parallel",), vmem_limit_bytes=_vmem_limit(vmem)),
        name="final",
    )(x2d, p2d, o_a, y_fold, proj, proj, proj, proj, *weights)


def kernel(x, p, w_in, w_attn_proj, w_ssm_proj, w_out, ssm_a_re, ssm_a_im, ssm_log_dt, ssm_b_re, ssm_b_im, ssm_c_re, ssm_c_im, ssm_d, w_glu, w_ple_gate, w_ple_proj, ln_g, ln_b, rel_bias):
    b, s, d = x.shape
    m = b * s
    for i in range(w_in.shape[0]):
        x2d = x.reshape(m, d)
        wq, wk, wv, wza, wu, wzs, wga, wgs = jnp.split(w_in[i].astype(BF16), _IN_SPLITS, axis=1)
        w_main = jnp.concatenate([wga, wgs, wk, wza, wzs], axis=1)
        proj, qv_t, u_fold = _proj(x2d, w_main, jnp.concatenate([wq, wv], axis=1).T, wu, batch=b)
        o_a = _attention(proj.reshape(b, s, PROJ_WIDTH), qv_t, rel_bias.astype(F32))
        toep, s_in, s_out, carry = _ssm_weights(
            ssm_a_re[i], ssm_a_im[i], ssm_log_dt[i], ssm_b_re[i], ssm_b_im[i],
            ssm_c_re[i], ssm_c_im[i], ssm_d[i].reshape(SSM_GROUPS, SSM_GROUP))
        y_fold = _ssm(u_fold, toep, s_in, s_out, carry, batch=b)
        x2d = _final(x2d, p[i].reshape(m, PLE_DIM), proj, o_a.reshape(m, ATTN_WIDTH), y_fold,
                     w_attn_proj[i].astype(BF16), w_glu[i].astype(BF16), w_ssm_proj[i].astype(BF16),
                     w_out[i].astype(BF16), w_ple_gate[i].astype(BF16), w_ple_proj[i].astype(BF16),
                     ln_g[i].astype(F32).reshape(1, d), ln_b[i].astype(F32).reshape(1, d))
        x = x2d.reshape(b, s, d)
    return x
```

```python
import functools
import math

import numpy as np
import jax
import jax.numpy as jnp
from jax import lax
from jax.experimental import pallas as pl
from jax.experimental.pallas import tpu as pltpu

F32 = jnp.float32
BF16 = jnp.bfloat16

LANES = 128
BF16_SUBLANES = 16
V7X_VMEM_BYTES = 64 * 1024 * 1024

D_MODEL = 1024
PLE_DIM = 256
HEADS = 8
HEAD_DIM = 64
QUERY_SCALE = HEAD_DIM ** -0.5
ATTN_WIDTH = HEADS * HEAD_DIM
HEAD_PAIRS = ATTN_WIDTH // LANES
MOBA_BLOCK = 256
MOBA_TOPK = 3
REL_BUCKETS = 32
REL_MAX_DIST = 128
SSM_WIDTH = 512
SSM_GROUP = 16
SSM_GROUPS = SSM_WIDTH // SSM_GROUP
SSM_STATE = 64
SSM_CHUNK = 16
IN_WIDTH = 4 * ATTN_WIDTH + 2 * SSM_WIDTH + 2 * D_MODEL
DEPTH = 1
DEEPNORM_ALPHA = (2.0 * DEPTH) ** 0.25
LN_EPS = 1e-5
MASK_VALUE = -1e30

PROJ_WIDTH = IN_WIDTH - 2 * ATTN_WIDTH - SSM_WIDTH
_IN_SPLITS = tuple(int(v) for v in np.cumsum(
    (ATTN_WIDTH,) * 4 + (SSM_WIDTH,) * 2 + (D_MODEL,) * 2)[:-1])
COL_GA, COL_GS = 0, 1
COL_K, COL_ZA, COL_ZS = 4, 5, 6


def _dot(a, b):
    return jnp.dot(a, b, preferred_element_type=F32)


def _dot_nt(a, b):
    return lax.dot_general(a, b, (((1,), (1,)), ((), ())), preferred_element_type=F32)


def _sigmoid(v):
    return 1.0 / (1.0 + jnp.exp(-v))


def _vmem_limit(nbytes):
    return int(min(V7X_VMEM_BYTES - (4 << 20), max(nbytes + (8 << 20), 32 << 20)))


def _proj_kernel(x_ref, w_ref, wt_ref, wu_ref, o_ref, t_ref, uf_ref, u_sc, *, col_tile):
    xb = x_ref[...].astype(BF16)
    u = _dot(xb, wu_ref[...])
    for gb in range(u_sc.shape[0]):
        u_sc[gb] = u[:, gb * LANES:(gb + 1) * LANES]
    n_cols = o_ref.shape[1]
    for start in range(0, n_cols, col_tile):
        cols = slice(start, min(start + col_tile, n_cols))
        o_ref[:, cols] = _dot(xb, w_ref[:, cols]).astype(BF16)
    t_ref[...] = _dot_nt(wt_ref[...], xb).astype(BF16)
    n_chunks = u_sc.shape[1] // SSM_CHUNK
    groups_per_tile = LANES // SSM_GROUP
    steps = [[u_sc[gb, pl.ds(t, n_chunks, stride=SSM_CHUNK), :]
              for gb in range(u_sc.shape[0])] for t in range(SSM_CHUNK)]
    for g in range(SSM_GROUPS):
        gb, lo = g // groups_per_tile, (g % groups_per_tile) * SSM_GROUP
        folded = jnp.concatenate([steps[t][gb][:, lo:lo + SSM_GROUP] for t in range(SSM_CHUNK)], axis=1)
        uf_ref[g] = folded.astype(BF16)


def _proj(x2d, w_bf16, wt_bf16, wu_bf16, *, batch, row_tile=512, col_tile=1024):
    m, k = x2d.shape
    n = w_bf16.shape[1]
    nt = wt_bf16.shape[0]
    nu = wu_bf16.shape[1]
    tiles_per_batch = m // batch // row_tile
    fold_rows = row_tile // SSM_CHUNK
    vmem = (2 * row_tile * k * 4 + 2 * k * (n + nt + nu) * 2 + 2 * row_tile * (n + nt + nu) * 2
            + row_tile * k * 2 + 2 * row_tile * col_tile * 4 + row_tile * nu * 4)
    return pl.pallas_call(
        functools.partial(_proj_kernel, col_tile=col_tile),
        grid=(m // row_tile,),
        in_specs=[pl.BlockSpec((row_tile, k), lambda i: (i, 0)),
                  pl.BlockSpec((k, n), lambda i: (0, 0)),
                  pl.BlockSpec((nt, k), lambda i: (0, 0)),
                  pl.BlockSpec((k, nu), lambda i: (0, 0))],
        out_specs=[pl.BlockSpec((row_tile, n), lambda i: (i, 0)),
                   pl.BlockSpec((None, nt, row_tile),
                                lambda i: (i // tiles_per_batch, 0, i % tiles_per_batch)),
                   pl.BlockSpec((SSM_GROUPS, fold_rows, SSM_CHUNK * SSM_GROUP), lambda i: (0, i, 0))],
        out_shape=[jax.ShapeDtypeStruct((m, n), BF16),
                   jax.ShapeDtypeStruct((batch, nt, m // batch), BF16),
                   jax.ShapeDtypeStruct((SSM_GROUPS, m // SSM_CHUNK, SSM_CHUNK * SSM_GROUP), BF16)],
        scratch_shapes=[pltpu.VMEM((nu // LANES, row_tile, LANES), F32)],
        compiler_params=pltpu.CompilerParams(
            dimension_semantics=("parallel",), vmem_limit_bytes=_vmem_limit(vmem)),
        name="proj",
    )(x2d, w_bf16, wt_bf16, wu_bf16)


def _t5_bucket_thresholds():
    max_exact = REL_BUCKETS // 2
    dist = np.arange(0, 2 * MOBA_BLOCK, dtype=np.int32)
    d = np.maximum(dist, 1).astype(np.float32)
    large = max_exact + (np.log(d / np.float32(max_exact)) / np.float32(math.log(REL_MAX_DIST / max_exact))
                         * np.float32(REL_BUCKETS - max_exact)).astype(np.int32)
    large = np.minimum(large, REL_BUCKETS - 1)
    bucket = np.where(dist < max_exact, dist, large)
    assert np.all(np.diff(bucket) >= 0) and bucket[-1] == REL_BUCKETS - 1
    return [int(np.argmax(bucket >= k)) for k in range(1, REL_BUCKETS)]


_BUCKET_THRESHOLDS = _t5_bucket_thresholds()


def _bias_strip_kernel(relb_ref, o_ref, *, lead):
    head = pl.program_id(0)
    far = relb_ref[REL_BUCKETS - 1, head]
    shape = o_ref.shape
    dist = (lax.broadcasted_iota(jnp.int32, shape, 1) - lax.broadcasted_iota(jnp.int32, shape, 0) - lead)
    val = jnp.full(shape, relb_ref[0, head] - far, F32)
    for kk, thr in enumerate(_BUCKET_THRESHOLDS, start=1):
        val = jnp.where(dist >= thr, relb_ref[kk, head] - far, val)
    o_ref[...] = jnp.where(dist >= 0, val, MASK_VALUE)


def _bias_strip(rel_bias, *, lead, width):
    return pl.pallas_call(
        functools.partial(_bias_strip_kernel, lead=lead),
        grid=(HEADS,),
        in_specs=[pl.BlockSpec(memory_space=pltpu.SMEM)],
        out_specs=pl.BlockSpec((None, MOBA_BLOCK, width), lambda h: (h, 0, 0)),
        out_shape=jax.ShapeDtypeStruct((HEADS, MOBA_BLOCK, width), F32),
        compiler_params=pltpu.CompilerParams(dimension_semantics=("parallel",)),
        name="bias_strip",
    )(rel_bias)


def _attn_kernel(qt_ref, k_ref, vt_ref, strip_ref, o_ref,
                 kaug, vaug_t, kmean, kmean_hi, kmean_lo, qaug_t, acc_t, mcol, logits_a, logits_b, smax,
                 *, tile_blocks):
    blk = MOBA_BLOCK
    qtile = tile_blocks * blk
    n_tiles = k_ref.shape[0] // qtile
    n_slots = kmean.shape[1]
    lane = lax.broadcasted_iota(jnp.int32, (blk, LANES), 1)
    dim = lax.broadcasted_iota(jnp.int32, (LANES, qtile), 0)

    def in_head(index, h):
        return (index >= HEAD_DIM * h) & (index < HEAD_DIM * (h + 1))

    def other_off(h):
        return HEAD_DIM * (1 - h)

    def _prepare():
        kmean[...] = jnp.zeros(kmean.shape, F32)
        for h in range(2):
            head = in_head(lane, h)
            off = other_off(h)

            def build(j, carry, h=h, head=head, off=off):
                rows = pl.ds(pl.multiple_of(j * blk, blk), blk)
                kb = k_ref[rows, :]
                onehot = jnp.where(lane - off == j, 1.0, 0.0).astype(BF16)
                kaug[h, rows, :] = jnp.where(head, kb, onehot)
                vaug_t[h, :HEAD_DIM, rows] = vt_ref[HEAD_DIM * h:HEAD_DIM * (h + 1), rows]
                vaug_t[h, HEAD_DIM:, rows] = jnp.ones((vaug_t.shape[1] - HEAD_DIM, blk), BF16)
                mean = jnp.sum(kb.astype(F32), axis=0, keepdims=True) * (1.0 / blk)
                kmean[h, pl.ds(j, 1), :] = jnp.where(head[:1], mean, 0.0)
                return carry

            lax.fori_loop(0, tile_blocks * n_tiles, build, 0)
            km = kmean[h]
            hi = km.astype(BF16)
            kmean_hi[h] = hi
            kmean_lo[h] = (km - hi.astype(F32)).astype(BF16)

    _prepare()
    block_id = lax.broadcasted_iota(jnp.int32, (n_slots, qtile), 0)
    qcol = lax.broadcasted_iota(jnp.int32, (n_slots, qtile), 1)
    block_in_tile = sum((qcol >= r * blk).astype(jnp.int32) for r in range(1, tile_blocks))
    slots = (logits_a, logits_b)

    def block_rows(j, t):
        b_idx = jnp.maximum(tile_blocks * (j + 1) - 1 - t, 0)
        return pl.ds(pl.multiple_of(b_idx * blk, blk), blk)

    def compute_logits(j, t, slot, biased, heads=(0, 1)):
        for h in heads:
            s = _dot(kaug[h, block_rows(j, t), :], qaug_t[j % 2, h])
            if biased:
                s = s + strip_ref[h, :, t * blk:t * blk + qtile]
            slots[slot][h] = s.astype(BF16)
            smax[slot, h] = jnp.max(s, axis=0, keepdims=True).astype(BF16).astype(F32)

    def softmax_update(j, t, slot, heads=(0, 1)):
        for h in heads:
            s = slots[slot][h]
            m_old = mcol[h]
            m_new = jnp.maximum(m_old, smax[slot, h])
            p = jnp.exp(s - m_new.astype(BF16))
            acc_t[h] = (acc_t[h] * jnp.exp(m_old - m_new)
                        + _dot(vaug_t[h, :, block_rows(j, t)], p))
            mcol[h] = m_new

    def select_blocks(j):
        qt = qt_ref[:, pl.ds(pl.multiple_of(j * qtile, qtile), qtile)]
        past = block_id < tile_blocks * j + block_in_tile
        for h in range(2):
            off = other_off(h)
            qs = jnp.where(in_head(dim, h), qt * QUERY_SCALE, jnp.zeros_like(qt))
            gate = _dot(kmean_hi[h], qs) + _dot(kmean_lo[h], qs)
            gate = jnp.where(past, gate, -jnp.inf)
            chosen = jnp.zeros(gate.shape, jnp.bool_)
            for _ in range(MOBA_TOPK):
                top = jnp.max(gate, axis=0, keepdims=True)
                first = jnp.min(jnp.where(gate == top, block_id, n_slots), axis=0, keepdims=True)
                hit = block_id == first
                chosen = chosen | (hit & (top > -jnp.inf))
                gate = jnp.where(hit, -jnp.inf, gate)
            qaug_t[j % 2, h] = qs
            qaug_t[j % 2, h, off:off + n_slots, :] = jnp.where(
                past & jnp.logical_not(chosen), MASK_VALUE, 0.0).astype(BF16)

    n_lead = tile_blocks // 2

    def query_tile(j, carry):
        n_pairs = tile_blocks * (j + 1) // 2
        mcol[...] = jnp.full(mcol.shape, -jnp.inf, F32)
        acc_t[...] = jnp.zeros(acc_t.shape, F32)

        def step_pair(pair, lead=False):
            t = 2 * pair
            for step, slot in ((t, 0), (t + 1, 1)):
                biased = lead and step + 2 <= tile_blocks
                for h in range(2):
                    softmax_update(j, step, slot, (h,))
                    compute_logits(j, step + 2, slot, biased, (h,))

        for pair in range(n_lead - 1):
            step_pair(pair, True)

        @pl.when(j >= 1)
        def _last_lead_pair():
            step_pair(n_lead - 1, True)

        n_mid = jnp.maximum(n_pairs - 1 - n_lead, 0)

        def pairs_from(first, count):
            for k in range(count):
                step_pair(first + k)

        def four_pairs(k, carry):
            pairs_from(n_lead + 4 * k, 4)
            return carry

        lax.fori_loop(0, n_mid // 4, four_pairs, 0)

        @pl.when(n_mid % 4 >= 2)
        def _two_more_pairs():
            pairs_from(n_lead + n_mid // 4 * 4, 2)

        @pl.when(n_mid % 2 == 1)
        def _one_more_pair():
            pairs_from(n_lead + n_mid - 1, 1)

        nxt = jnp.where(j + 1 < n_tiles, j + 1, jnp.maximum(j - 1, 0))
        t_last = 2 * (n_pairs - 1)
        select_blocks(nxt)
        for slot in range(2):
            for h in range(2):
                softmax_update(j, t_last + slot, slot, (h,))
                compute_logits(nxt, slot, slot, True, (h,))

        a0 = acc_t[0]
        a1 = acc_t[1]
        o_t = jnp.concatenate([a0[:HEAD_DIM] / a0[HEAD_DIM:HEAD_DIM + 1],
                               a1[:HEAD_DIM] / a1[HEAD_DIM:HEAD_DIM + 1]], axis=0)
        o_ref[pl.ds(pl.multiple_of(j * qtile, qtile), qtile), :] = o_t.T.astype(o_ref.dtype)
        return carry

    select_blocks(0)
    compute_logits(0, 0, 0, True)
    compute_logits(0, 1, 1, True)
    lax.fori_loop(0, n_tiles, query_tile, 0)


def _attention(proj3d, qv_t, rel_bias, *, tile_blocks=2):
    b, s, _ = proj3d.shape
    blk = MOBA_BLOCK
    qtile = tile_blocks * blk
    n_slots = HEAD_DIM // 2
    assert tile_blocks % 2 == 0 and s % qtile == 0 and s // blk <= n_slots
    pv_rows = HEAD_DIM + BF16_SUBLANES
    strip_width = 2 * qtile
    strip = _bias_strip(rel_bias, lead=(tile_blocks - 1) * blk, width=strip_width)
    scratch = [
        pltpu.VMEM((2, s, LANES), BF16),
        pltpu.VMEM((2, pv_rows, s), BF16),
        pltpu.VMEM((2, n_slots, LANES), F32),
        pltpu.VMEM((2, n_slots, LANES), BF16),
        pltpu.VMEM((2, n_slots, LANES), BF16),
        pltpu.VMEM((2, 2, LANES, qtile), BF16),
        pltpu.VMEM((2, pv_rows, qtile), F32),
        pltpu.VMEM((2, 1, qtile), F32),
        pltpu.VMEM((2, blk, qtile), BF16),
        pltpu.VMEM((2, blk, qtile), BF16),
        pltpu.VMEM((2, 2, 1, qtile), F32),
    ]
    vmem = (2 * s * LANES * 2 + 2 * pv_rows * s * 2
            + 2 * 4 * s * LANES * 2
            + 2 * 2 * blk * strip_width * 4
            + 4 * blk * qtile * 2 + 10 * blk * qtile * 4)
    return pl.pallas_call(
        functools.partial(_attn_kernel, tile_blocks=tile_blocks),
        grid=(b, HEAD_PAIRS),
        in_specs=[
            pl.BlockSpec((None, LANES, s), lambda bi, hp: (bi, hp, 0)),
            pl.BlockSpec((None, s, LANES), lambda bi, hp: (bi, 0, COL_K * HEAD_PAIRS + hp)),
            pl.BlockSpec((None, LANES, s), lambda bi, hp: (bi, HEAD_PAIRS + hp, 0)),
            pl.BlockSpec((2, blk, strip_width), lambda bi, hp: (hp, 0, 0)),
        ],
        out_specs=pl.BlockSpec((None, s, LANES), lambda bi, hp: (bi, 0, hp)),
        out_shape=jax.ShapeDtypeStruct((b, s, ATTN_WIDTH), BF16),
        scratch_shapes=scratch,
        compiler_params=pltpu.CompilerParams(
            dimension_semantics=("parallel", "parallel"),
            vmem_limit_bytes=_vmem_limit(vmem)),
        name="moba_attn",
    )(qv_t, proj3d, qv_t, strip)


def _ssm_weights(a_re, a_im, log_dt, b_re, b_im, c_re, c_im, d_skip):
    g, p_states, c = b_re.shape
    big_l = SSM_CHUNK
    dt = jnp.exp(log_dt.astype(F32))[:, None]
    ar = a_re.astype(F32)
    ai = a_im.astype(F32)
    mag = jnp.exp(dt * ar)
    ang = dt * ai
    abar_re = mag * jnp.cos(ang)
    abar_im = mag * jnp.sin(ang)
    den = ar * ar + ai * ai
    nr = abar_re - 1.0
    ni = abar_im
    fr = (nr * ar + ni * ai) / den
    fi = (ni * ar - nr * ai) / den
    br = b_re.astype(F32)
    bi = b_im.astype(F32)
    bbar_re = fr[..., None] * br - fi[..., None] * bi
    bbar_im = fr[..., None] * bi + fi[..., None] * br
    n = jnp.arange(big_l + 1, dtype=F32)[:, None, None]
    pmag = jnp.exp(n * (dt * ar)[None])
    pw_re = pmag * jnp.cos(n * ang[None])
    pw_im = pmag * jnp.sin(n * ang[None])
    cr = c_re.astype(F32)
    ci = c_im.astype(F32)
    ab_re = pw_re[..., None] * bbar_re[None] - pw_im[..., None] * bbar_im[None]
    ab_im = pw_re[..., None] * bbar_im[None] + pw_im[..., None] * bbar_re[None]
    hi = lax.Precision.HIGHEST
    kern = (jnp.einsum('gdp,ngpc->ngcd', cr, ab_re[:big_l], precision=hi)
            - jnp.einsum('gdp,ngpc->ngcd', ci, ab_im[:big_l], precision=hi))
    lag = np.arange(big_l)[None, :] - np.arange(big_l)[:, None]
    place = jnp.asarray(lag[:, :, None] == np.arange(big_l), F32)
    toep = jnp.einsum('stn,ngcd->gsctd', place, kern, precision=hi)
    toep = toep.reshape(g, big_l * c, big_l * c)
    skip = jnp.tile(d_skip.astype(F32).reshape(g, 1, c), (1, big_l, 1)).reshape(g, big_l * c)
    toep = toep + skip[:, :, None] * jnp.eye(big_l * c, dtype=F32)[None]
    e_re = ab_re[:big_l][::-1].transpose(1, 0, 3, 2).reshape(g, big_l * c, p_states)
    e_im = ab_im[:big_l][::-1].transpose(1, 0, 3, 2).reshape(g, big_l * c, p_states)
    w_in = jnp.concatenate([e_re, e_im, e_im, e_re], axis=-1)
    ca_re = cr[None] * pw_re[1:, :, None, :] - ci[None] * pw_im[1:, :, None, :]
    ca_im = cr[None] * pw_im[1:, :, None, :] + ci[None] * pw_re[1:, :, None, :]
    o_re = ca_re.transpose(1, 3, 0, 2).reshape(g, p_states, big_l * c)
    o_im = -ca_im.transpose(1, 3, 0, 2).reshape(g, p_states, big_l * c)
    w_out = jnp.concatenate([o_re, o_im], axis=1)
    are, aim = pw_re[big_l], pw_im[big_l]
    carry = jnp.stack([jnp.concatenate([are, are], -1),
                       jnp.concatenate([-aim, aim], -1),
                       jnp.concatenate([aim, -aim], -1)], axis=1)
    return toep.astype(BF16), w_in.astype(BF16), w_out.astype(BF16), carry


def _ssm_kernel(u_ref, toep_ref, win_ref, wout_ref, carry_ref, y_ref, e_sc, prev_sc, *, batch):
    n_groups = u_ref.shape[0]
    n_chunks = u_ref.shape[1] // batch
    half = LANES
    coef = []
    for g in range(n_groups):
        e = _dot(u_ref[g], win_ref[g])
        e_sc[g, 0] = e[:, :half]
        e_sc[g, 1] = e[:, half:]
        coef.append([jnp.broadcast_to(carry_ref[g, r:r + 1, :], (batch, half)) for r in range(3)])

    def step(kk, state):
        rows = pl.ds(kk, batch, stride=n_chunks)
        out = []
        for g in range(n_groups):
            st, st_swapped = state[2 * g], state[2 * g + 1]
            a1, a2, a3 = coef[g]
            prev_sc[g, rows, :] = st
            out.append(a1 * st + a2 * st_swapped + e_sc[g, 0, rows, :])
            out.append(a1 * st_swapped + a3 * st + e_sc[g, 1, rows, :])
        return tuple(out)

    zero = jnp.zeros((batch, half), F32)
    lax.fori_loop(0, n_chunks, step, (zero,) * (2 * n_groups), unroll=8)
    for g in range(n_groups):
        y = _dot(u_ref[g], toep_ref[g]) + _dot(prev_sc[g].astype(BF16), wout_ref[g])
        y_ref[g] = y.astype(y_ref.dtype)


def _ssm(u_t, toep, w_in, w_out, carry, *, batch, groups_per_step=2):
    g, rows, width = u_t.shape
    gps = groups_per_step
    vmem = gps * (2 * 2 * rows * width * 2 + rows * width * 4 + rows * LANES * 4 + 2 * rows * width * 4)

    def per_step(*tail):
        return pl.BlockSpec((gps,) + tail, lambda gi: (gi,) + (0,) * len(tail))

    return pl.pallas_call(
        functools.partial(_ssm_kernel, batch=batch),
        grid=(g // gps,),
        in_specs=[per_step(rows, width), per_step(width, width), per_step(width, width),
                  per_step(LANES, width), per_step(3, LANES)],
        out_specs=per_step(rows, width),
        out_shape=jax.ShapeDtypeStruct((g, rows, width), BF16),
        scratch_shapes=[pltpu.VMEM((gps, 2, rows, LANES), F32), pltpu.VMEM((gps, rows, LANES), F32)],
        compiler_params=pltpu.CompilerParams(
            dimension_semantics=("parallel",), vmem_limit_bytes=_vmem_limit(vmem)),
        name="s5_ssm",
    )(u_t, toep, w_in, w_out, carry)


def _final_kernel(x_ref, p_ref, oa_ref, yf_ref, za_ref, zs_ref, ga_ref, gs_ref,
                  wap_ref, wglu_ref, wsp_ref, wout_ref, wpg_ref, wpp_ref, lng_ref, lnb_ref, o_ref, ys_sc,
                  *, sub_rows):
    for sub in range(x_ref.shape[0] // sub_rows):
        _final_rows(sub, sub_rows, x_ref, p_ref, oa_ref, yf_ref, za_ref, zs_ref, ga_ref, gs_ref,
                    wap_ref, wglu_ref, wsp_ref, wout_ref, wpg_ref, wpp_ref, lng_ref, lnb_ref, o_ref, ys_sc)


def _final_rows(sub, sub_rows, x_ref, p_ref, oa_ref, yf_ref, za_ref, zs_ref, ga_ref, gs_ref,
                wap_ref, wglu_ref, wsp_ref, wout_ref, wpg_ref, wpp_ref, lng_ref, lnb_ref, o_ref, ys_sc):
    rows = slice(sub * sub_rows, (sub + 1) * sub_rows)
    n_chunks = sub_rows // SSM_CHUNK
    chunks = slice(sub * n_chunks, (sub + 1) * n_chunks)
    half = wpg_ref.shape[1] // 2
    x = x_ref[rows, :]
    xb = x.astype(BF16)
    za = za_ref[rows, :]
    a_in = oa_ref[rows, :] * (za * _sigmoid(za))
    y_a = _dot(a_in, wap_ref[...])
    gate_lo = _dot(xb, wpg_ref[:, :half])
    groups_per_tile = LANES // SSM_GROUP
    folded = [yf_ref[g, chunks, :].astype(F32) for g in range(SSM_GROUPS)]
    for t in range(SSM_CHUNK):
        for gb in range(ys_sc.shape[1]):
            tile = jnp.concatenate(
                [folded[gb * groups_per_tile + gl][:, t * SSM_GROUP:(t + 1) * SSM_GROUP]
                 for gl in range(groups_per_tile)], axis=1)
            ys_sc[sub, gb, pl.ds(t, n_chunks, stride=SSM_CHUNK), :] = tile
    ys = jnp.concatenate([ys_sc[sub, gb] for gb in range(ys_sc.shape[1])], axis=1)
    gelu = 0.5 * ys * (1.0 + lax.erf(ys * (2.0 ** -0.5)))
    glu = _dot(gelu.astype(BF16), wglu_ref[...])
    gate_hi = _dot(xb, wpg_ref[:, half:])
    zs = zs_ref[rows, :]
    s_in = glu[:, :SSM_WIDTH] * _sigmoid(glu[:, SSM_WIDTH:]) * (zs * _sigmoid(zs)).astype(F32)
    y_s = _dot(s_in.astype(BF16), wsp_ref[...])
    emb = _dot(p_ref[rows, :].astype(BF16), wpp_ref[...])
    merge = _sigmoid(ga_ref[rows, :]) * y_a.astype(BF16) + _sigmoid(gs_ref[rows, :]) * y_s.astype(BF16)
    mix = _dot(merge, wout_ref[...])
    ple = _sigmoid(jnp.concatenate([gate_lo, gate_hi], axis=1)) * emb
    hsum = DEEPNORM_ALPHA * x + mix + ple
    mu = jnp.mean(hsum, axis=-1, keepdims=True)
    cen = hsum - mu
    var = jnp.mean(cen * cen, axis=-1, keepdims=True)
    o_ref[rows, :] = cen * lax.rsqrt(var + LN_EPS) * lng_ref[...] + lnb_ref[...]


def _final(x2d, p2d, proj, o_a, y_fold, w_ap, w_glu, w_sp, w_out, w_pg, w_pp, ln_g, ln_b, *,
           row_tile=512, sub_rows=256):
    m = x2d.shape[0]
    half, full = SSM_WIDTH, D_MODEL

    def rows(width, col):
        return pl.BlockSpec((row_tile, width), lambda i: (i, col))

    def whole(arr):
        return pl.BlockSpec(arr.shape, lambda i: (0, 0))

    weights = (w_ap, w_glu, w_sp, w_out, w_pg, w_pp, ln_g, ln_b)
    vmem = (2 * sum(int(np.prod(w.shape)) * w.dtype.itemsize for w in weights)
            + 2 * row_tile * (2 * full * 4 + PLE_DIM * 4 + (4 * half + 2 * full) * 2)
            + 12 * row_tile * full * 4)
    return pl.pallas_call(
        functools.partial(_final_kernel, sub_rows=sub_rows),
        grid=(m // row_tile,),
        in_specs=[rows(full, 0), rows(PLE_DIM, 0), rows(half, 0),
                  pl.BlockSpec((SSM_GROUPS, row_tile // SSM_CHUNK, SSM_CHUNK * SSM_GROUP), lambda i: (0, i, 0)),
                  rows(half, COL_ZA), rows(half, COL_ZS), rows(full, COL_GA), rows(full, COL_GS)]
                 + [whole(w) for w in weights],
        out_specs=rows(full, 0),
        out_shape=jax.ShapeDtypeStruct((m, full), F32),
        scratch_shapes=[pltpu.VMEM((row_tile // sub_rows, half // LANES, sub_rows, LANES), F32)],
        compiler_params=pltpu.CompilerParams(
            dimension_semantics=("parallel",), vmem_limit_bytes=_vmem_limit(vmem)),
        name="final",
    )(x2d, p2d, o_a, y_fold, proj, proj, proj, proj, *weights)


def kernel(x, p, w_in, w_attn_proj, w_ssm_proj, w_out, ssm_a_re, ssm_a_im, ssm_log_dt, ssm_b_re, ssm_b_im, ssm_c_re, ssm_c_im, ssm_d, w_glu, w_ple_gate, w_ple_proj, ln_g, ln_b, rel_bias):
    b, s, d = x.shape
    m = b * s
    for i in range(w_in.shape[0]):
        x2d = x.reshape(m, d)
        wq, wk, wv, wza, wu, wzs, wga, wgs = jnp.split(w_in[i].astype(BF16), _IN_SPLITS, axis=1)
        w_main = jnp.concatenate([wga, wgs, wk, wza, wzs], axis=1)
        proj, qv_t, u_fold = _proj(x2d, w_main, jnp.concatenate([wq, wv], axis=1).T, wu, batch=b)
        o_a = _attention(proj.reshape(b, s, PROJ_WIDTH), qv_t, rel_bias.astype(F32))
        toep, s_in, s_out, carry = _ssm_weights(
            ssm_a_re[i], ssm_a_im[i], ssm_log_dt[i], ssm_b_re[i], ssm_b_im[i],
            ssm_c_re[i], ssm_c_im[i], ssm_d[i].reshape(SSM_GROUPS, SSM_GROUP))
        y_fold = _ssm(u_fold, toep, s_in, s_out, carry, batch=b)
        x2d = _final(x2d, p[i].reshape(m, PLE_DIM), proj, o_a.reshape(m, ATTN_WIDTH), y_fold,
                     w_attn_proj[i].astype(BF16), w_glu[i].astype(BF16), w_ssm_proj[i].astype(BF16),
                     w_out[i].astype(BF16), w_ple_gate[i].astype(BF16), w_ple_proj[i].astype(BF16),
                     ln_g[i].astype(F32).reshape(1, d), ln_b[i].astype(F32).reshape(1, d))
        x = x2d.reshape(b, s, d)
    return x
```

```python
import functools
import math

import numpy as np
import jax
import jax.numpy as jnp
from jax import lax
from jax.experimental import pallas as pl
from jax.experimental.pallas import tpu as pltpu

F32 = jnp.float32
BF16 = jnp.bfloat16

LANES = 128
BF16_SUBLANES = 16
V7X_VMEM_BYTES = 64 * 1024 * 1024

D_MODEL = 1024
PLE_DIM = 256
HEADS = 8
HEAD_DIM = 64
QUERY_SCALE = HEAD_DIM ** -0.5
ATTN_WIDTH = HEADS * HEAD_DIM
HEAD_PAIRS = ATTN_WIDTH // LANES
MOBA_BLOCK = 256
MOBA_TOPK = 3
REL_BUCKETS = 32
REL_MAX_DIST = 128
SSM_WIDTH = 512
SSM_GROUP = 16
SSM_GROUPS = SSM_WIDTH // SSM_GROUP
SSM_STATE = 64
SSM_CHUNK = 16
IN_WIDTH = 4 * ATTN_WIDTH + 2 * SSM_WIDTH + 2 * D_MODEL
DEPTH = 1
DEEPNORM_ALPHA = (2.0 * DEPTH) ** 0.25
LN_EPS = 1e-5
MASK_VALUE = -1e30

PROJ_WIDTH = IN_WIDTH - 2 * ATTN_WIDTH - SSM_WIDTH
_IN_SPLITS = tuple(int(v) for v in np.cumsum(
    (ATTN_WIDTH,) * 4 + (SSM_WIDTH,) * 2 + (D_MODEL,) * 2)[:-1])
COL_GA, COL_GS = 0, 1
COL_K, COL_ZA, COL_ZS = 4, 5, 6


def _dot(a, b):
    return jnp.dot(a, b, preferred_element_type=F32)


def _dot_nt(a, b):
    return lax.dot_general(a, b, (((1,), (1,)), ((), ())), preferred_element_type=F32)


def _sigmoid(v):
    return 1.0 / (1.0 + jnp.exp(-v))


def _vmem_limit(nbytes):
    return int(min(V7X_VMEM_BYTES - (4 << 20), max(nbytes + (8 << 20), 32 << 20)))


def _proj_kernel(x_ref, w_ref, wt_ref, wu_ref, o_ref, t_ref, uf_ref, u_sc, *, col_tile):
    xb = x_ref[...].astype(BF16)
    u = _dot(xb, wu_ref[...])
    for gb in range(u_sc.shape[0]):
        u_sc[gb] = u[:, gb * LANES:(gb + 1) * LANES]
    n_cols = o_ref.shape[1]
    for start in range(0, n_cols, col_tile):
        cols = slice(start, min(start + col_tile, n_cols))
        o_ref[:, cols] = _dot(xb, w_ref[:, cols]).astype(BF16)
    t_ref[...] = _dot_nt(wt_ref[...], xb).astype(BF16)
    n_chunks = u_sc.shape[1] // SSM_CHUNK
    groups_per_tile = LANES // SSM_GROUP
    steps = [[u_sc[gb, pl.ds(t, n_chunks, stride=SSM_CHUNK), :]
              for gb in range(u_sc.shape[0])] for t in range(SSM_CHUNK)]
    for g in range(SSM_GROUPS):
        gb, lo = g // groups_per_tile, (g % groups_per_tile) * SSM_GROUP
        folded = jnp.concatenate([steps[t][gb][:, lo:lo + SSM_GROUP] for t in range(SSM_CHUNK)], axis=1)
        uf_ref[g] = folded.astype(BF16)


def _proj(x2d, w_bf16, wt_bf16, wu_bf16, *, batch, row_tile=512, col_tile=1024):
    m, k = x2d.shape
    n = w_bf16.shape[1]
    nt = wt_bf16.shape[0]
    nu = wu_bf16.shape[1]
    tiles_per_batch = m // batch // row_tile
    fold_rows = row_tile // SSM_CHUNK
    vmem = (2 * row_tile * k * 4 + 2 * k * (n + nt + nu) * 2 + 2 * row_tile * (n + nt + nu) * 2
            + row_tile * k * 2 + 2 * row_tile * col_tile * 4 + row_tile * nu * 4)
    return pl.pallas_call(
        functools.partial(_proj_kernel, col_tile=col_tile),
        grid=(m // row_tile,),
        in_specs=[pl.BlockSpec((row_tile, k), lambda i: (i, 0)),
                  pl.BlockSpec((k, n), lambda i: (0, 0)),
                  pl.BlockSpec((nt, k), lambda i: (0, 0)),
                  pl.BlockSpec((k, nu), lambda i: (0, 0))],
        out_specs=[pl.BlockSpec((row_tile, n), lambda i: (i, 0)),
                   pl.BlockSpec((None, nt, row_tile),
                                lambda i: (i // tiles_per_batch, 0, i % tiles_per_batch)),
                   pl.BlockSpec((SSM_GROUPS, fold_rows, SSM_CHUNK * SSM_GROUP), lambda i: (0, i, 0))],
        out_shape=[jax.ShapeDtypeStruct((m, n), BF16),
                   jax.ShapeDtypeStruct((batch, nt, m // batch), BF16),
                   jax.ShapeDtypeStruct((SSM_GROUPS, m // SSM_CHUNK, SSM_CHUNK * SSM_GROUP), BF16)],
        scratch_shapes=[pltpu.VMEM((nu // LANES, row_tile, LANES), F32)],
        compiler_params=pltpu.CompilerParams(
            dimension_semantics=("parallel",), vmem_limit_bytes=_vmem_limit(vmem)),
        name="proj",
    )(x2d, w_bf16, wt_bf16, wu_bf16)


def _t5_bucket_thresholds():
    max_exact = REL_BUCKETS // 2
    dist = np.arange(0, 2 * MOBA_BLOCK, dtype=np.int32)
    d = np.maximum(dist, 1).astype(np.float32)
    large = max_exact + (np.log(d / np.float32(max_exact)) / np.float32(math.log(REL_MAX_DIST / max_exact))
                         * np.float32(REL_BUCKETS - max_exact)).astype(np.int32)
    large = np.minimum(large, REL_BUCKETS - 1)
    bucket = np.where(dist < max_exact, dist, large)
    assert np.all(np.diff(bucket) >= 0) and bucket[-1] == REL_BUCKETS - 1
    return [int(np.argmax(bucket >= k)) for k in range(1, REL_BUCKETS)]


_BUCKET_THRESHOLDS = _t5_bucket_thresholds()


def _bias_strip_kernel(relb_ref, o_ref, *, lead):
    head = pl.program_id(0)
    far = relb_ref[REL_BUCKETS - 1, head]
    shape = o_ref.shape
    dist = (lax.broadcasted_iota(jnp.int32, shape, 1) - lax.broadcasted_iota(jnp.int32, shape, 0) - lead)
    val = jnp.full(shape, relb_ref[0, head] - far, F32)
    for kk, thr in enumerate(_BUCKET_THRESHOLDS, start=1):
        val = jnp.where(dist >= thr, relb_ref[kk, head] - far, val)
    o_ref[...] = jnp.where(dist >= 0, val, MASK_VALUE)


def _bias_strip(rel_bias, *, lead, width):
    return pl.pallas_call(
        functools.partial(_bias_strip_kernel, lead=lead),
        grid=(HEADS,),
        in_specs=[pl.BlockSpec(memory_space=pltpu.SMEM)],
        out_specs=pl.BlockSpec((None, MOBA_BLOCK, width), lambda h: (h, 0, 0)),
        out_shape=jax.ShapeDtypeStruct((HEADS, MOBA_BLOCK, width), F32),
        compiler_params=pltpu.CompilerParams(dimension_semantics=("parallel",)),
        name="bias_strip",
    )(rel_bias)


def _attn_kernel(qt_ref, k_ref, vt_ref, strip_ref, o_ref,
                 kaug, vaug_t, kmean, kmean_hi, kmean_lo, qaug_t, acc_t, mcol, logits_a, logits_b, smax,
                 *, tile_blocks):
    blk = MOBA_BLOCK
    qtile = tile_blocks * blk
    n_tiles = k_ref.shape[0] // qtile
    n_slots = kmean.shape[1]
    lane = lax.broadcasted_iota(jnp.int32, (blk, LANES), 1)
    dim = lax.broadcasted_iota(jnp.int32, (LANES, qtile), 0)

    def in_head(index, h):
        return (index >= HEAD_DIM * h) & (index < HEAD_DIM * (h + 1))

    def other_off(h):
        return HEAD_DIM * (1 - h)

    def _prepare():
        kmean[...] = jnp.zeros(kmean.shape, F32)
        for h in range(2):
            head = in_head(lane, h)
            off = other_off(h)

            def build(j, carry, h=h, head=head, off=off):
                rows = pl.ds(pl.multiple_of(j * blk, blk), blk)
                kb = k_ref[rows, :]
                onehot = jnp.where(lane - off == j, 1.0, 0.0).astype(BF16)
                kaug[h, rows, :] = jnp.where(head, kb, onehot)
                vaug_t[h, :HEAD_DIM, rows] = vt_ref[HEAD_DIM * h:HEAD_DIM * (h + 1), rows]
                vaug_t[h, HEAD_DIM:, rows] = jnp.ones((vaug_t.shape[1] - HEAD_DIM, blk), BF16)
                mean = jnp.sum(kb.astype(F32), axis=0, keepdims=True) * (1.0 / blk)
                kmean[h, pl.ds(j, 1), :] = jnp.where(head[:1], mean, 0.0)
                return carry

            lax.fori_loop(0, tile_blocks * n_tiles, build, 0)
            km = kmean[h]
            hi = km.astype(BF16)
            kmean_hi[h] = hi
            kmean_lo[h] = (km - hi.astype(F32)).astype(BF16)

    _prepare()
    block_id = lax.broadcasted_iota(jnp.int32, (n_slots, qtile), 0)
    qcol = lax.broadcasted_iota(jnp.int32, (n_slots, qtile), 1)
    block_in_tile = sum((qcol >= r * blk).astype(jnp.int32) for r in range(1, tile_blocks))
    slots = (logits_a, logits_b)

    def block_rows(j, t):
        b_idx = jnp.maximum(tile_blocks * (j + 1) - 1 - t, 0)
        return pl.ds(pl.multiple_of(b_idx * blk, blk), blk)

    def compute_logits(j, t, slot, biased, heads=(0, 1)):
        for h in heads:
            s = _dot(kaug[h, block_rows(j, t), :], qaug_t[j % 2, h])
            if biased:
                s = s + strip_ref[h, :, t * blk:t * blk + qtile]
            slots[slot][h] = s.astype(BF16)
            smax[slot, h] = jnp.max(s, axis=0, keepdims=True).astype(BF16).astype(F32)

    def softmax_update(j, t, slot, heads=(0, 1)):
        for h in heads:
            s = slots[slot][h]
            m_old = mcol[j % 2, h]
            m_new = jnp.maximum(m_old, smax[slot, h])
            p = jnp.exp(s - m_new.astype(BF16))
            acc_t[j % 2, h] = (acc_t[j % 2, h] * jnp.exp(m_old - m_new)
                               + _dot(vaug_t[h, :, block_rows(j, t)], p))
            mcol[j % 2, h] = m_new

    def select_blocks(j):
        qt = qt_ref[:, pl.ds(pl.multiple_of(j * qtile, qtile), qtile)]
        past = block_id < tile_blocks * j + block_in_tile
        for h in range(2):
            off = other_off(h)
            qs = jnp.where(in_head(dim, h), qt * QUERY_SCALE, jnp.zeros_like(qt))
            gate = _dot(kmean_hi[h], qs) + _dot(kmean_lo[h], qs)
            gate = jnp.where(past, gate, -jnp.inf)
            chosen = jnp.zeros(gate.shape, jnp.bool_)
            for _ in range(MOBA_TOPK):
                top = jnp.max(gate, axis=0, keepdims=True)
                first = jnp.min(jnp.where(gate == top, block_id, n_slots), axis=0, keepdims=True)
                hit = block_id == first
                chosen = chosen | (hit & (top > -jnp.inf))
                gate = jnp.where(hit, -jnp.inf, gate)
            qaug_t[j % 2, h] = qs
            qaug_t[j % 2, h, off:off + n_slots, :] = jnp.where(
                past & jnp.logical_not(chosen), MASK_VALUE, 0.0).astype(BF16)

    n_lead = tile_blocks // 2

    def reset_accumulators(j):
        mcol[j % 2] = jnp.full(mcol.shape[1:], -jnp.inf, F32)
        acc_t[j % 2] = jnp.zeros(acc_t.shape[1:], F32)

    def step_pair(j, pair, lead=False):
        t = 2 * pair
        for step, slot in ((t, 0), (t + 1, 1)):
            biased = lead and step + 2 <= tile_blocks
            for h in range(2):
                softmax_update(j, step, slot, (h,))
                compute_logits(j, step + 2, slot, biased, (h,))

    def query_tile(j, carry):
        n_pairs = tile_blocks * (j + 1) // 2

        n_mid = jnp.maximum(n_pairs - 1 - n_lead, 0)

        def pairs_from(first, count):
            for k in range(count):
                step_pair(j, first + k)

        def four_pairs(k, carry):
            pairs_from(n_lead + 4 * k, 4)
            return carry

        lax.fori_loop(0, n_mid // 4, four_pairs, 0)

        @pl.when(n_mid % 4 >= 2)
        def _two_more_pairs():
            pairs_from(n_lead + n_mid // 4 * 4, 2)

        @pl.when(n_mid % 2 == 1)
        def _one_more_pair():
            pairs_from(n_lead + n_mid - 1, 1)

        nxt = jnp.where(j + 1 < n_tiles, j + 1, jnp.maximum(j - 1, 0))
        t_last = 2 * (n_pairs - 1)
        select_blocks(nxt)
        reset_accumulators(nxt)
        for slot in range(2):
            for h in range(2):
                softmax_update(j, t_last + slot, slot, (h,))
                compute_logits(nxt, slot, slot, True, (h,))
        a0 = acc_t[j % 2, 0]
        a1 = acc_t[j % 2, 1]
        o_t = jnp.concatenate([a0[:HEAD_DIM] / a0[HEAD_DIM:HEAD_DIM + 1],
                               a1[:HEAD_DIM] / a1[HEAD_DIM:HEAD_DIM + 1]], axis=0)
        o_ref[pl.ds(pl.multiple_of(j * qtile, qtile), qtile), :] = o_t.T.astype(o_ref.dtype)
        for pair in range(n_lead):
            step_pair(nxt, pair, True)
        return carry

    select_blocks(0)
    reset_accumulators(0)
    compute_logits(0, 0, 0, True)
    compute_logits(0, 1, 1, True)
    for pair in range(n_lead - 1):
        step_pair(0, pair, True)
    lax.fori_loop(0, n_tiles, query_tile, 0)


def _attention(proj3d, qv_t, rel_bias, *, tile_blocks=2):
    b, s, _ = proj3d.shape
    blk = MOBA_BLOCK
    qtile = tile_blocks * blk
    n_slots = HEAD_DIM // 2
    assert tile_blocks % 2 == 0 and s % qtile == 0 and s // qtile >= 2 and s // blk <= n_slots
    pv_rows = HEAD_DIM + BF16_SUBLANES
    strip_width = 2 * qtile
    strip = _bias_strip(rel_bias, lead=(tile_blocks - 1) * blk, width=strip_width)
    scratch = [
        pltpu.VMEM((2, s, LANES), BF16),
        pltpu.VMEM((2, pv_rows, s), BF16),
        pltpu.VMEM((2, n_slots, LANES), F32),
        pltpu.VMEM((2, n_slots, LANES), BF16),
        pltpu.VMEM((2, n_slots, LANES), BF16),
        pltpu.VMEM((2, 2, LANES, qtile), BF16),
        pltpu.VMEM((2, 2, pv_rows, qtile), F32),
        pltpu.VMEM((2, 2, 1, qtile), F32),
        pltpu.VMEM((2, blk, qtile), BF16),
        pltpu.VMEM((2, blk, qtile), BF16),
        pltpu.VMEM((2, 2, 1, qtile), F32),
    ]
    vmem = (2 * s * LANES * 2 + 2 * pv_rows * s * 2
            + 2 * 4 * s * LANES * 2
            + 2 * 2 * blk * strip_width * 4
            + 4 * blk * qtile * 2 + 10 * blk * qtile * 4)
    return pl.pallas_call(
        functools.partial(_attn_kernel, tile_blocks=tile_blocks),
        grid=(b, HEAD_PAIRS),
        in_specs=[
            pl.BlockSpec((None, LANES, s), lambda bi, hp: (bi, hp, 0)),
            pl.BlockSpec((None, s, LANES), lambda bi, hp: (bi, 0, COL_K * HEAD_PAIRS + hp)),
            pl.BlockSpec((None, LANES, s), lambda bi, hp: (bi, HEAD_PAIRS + hp, 0)),
            pl.BlockSpec((2, blk, strip_width), lambda bi, hp: (hp, 0, 0)),
        ],
        out_specs=pl.BlockSpec((None, s, LANES), lambda bi, hp: (bi, 0, hp)),
        out_shape=jax.ShapeDtypeStruct((b, s, ATTN_WIDTH), BF16),
        scratch_shapes=scratch,
        compiler_params=pltpu.CompilerParams(
            dimension_semantics=("parallel", "parallel"),
            vmem_limit_bytes=_vmem_limit(vmem)),
        name="moba_attn",
    )(qv_t, proj3d, qv_t, strip)


def _ssm_weights(a_re, a_im, log_dt, b_re, b_im, c_re, c_im, d_skip):
    g, p_states, c = b_re.shape
    big_l = SSM_CHUNK
    dt = jnp.exp(log_dt.astype(F32))[:, None]
    ar = a_re.astype(F32)
    ai = a_im.astype(F32)
    mag = jnp.exp(dt * ar)
    ang = dt * ai
    abar_re = mag * jnp.cos(ang)
    abar_im = mag * jnp.sin(ang)
    den = ar * ar + ai * ai
    nr = abar_re - 1.0
    ni = abar_im
    fr = (nr * ar + ni * ai) / den
    fi = (ni * ar - nr * ai) / den
    br = b_re.astype(F32)
    bi = b_im.astype(F32)
    bbar_re = fr[..., None] * br - fi[..., None] * bi
    bbar_im = fr[..., None] * bi + fi[..., None] * br
    n = jnp.arange(big_l + 1, dtype=F32)[:, None, None]
    pmag = jnp.exp(n * (dt * ar)[None])
    pw_re = pmag * jnp.cos(n * ang[None])
    pw_im = pmag * jnp.sin(n * ang[None])
    cr = c_re.astype(F32)
    ci = c_im.astype(F32)
    ab_re = pw_re[..., None] * bbar_re[None] - pw_im[..., None] * bbar_im[None]
    ab_im = pw_re[..., None] * bbar_im[None] + pw_im[..., None] * bbar_re[None]
    hi = lax.Precision.HIGHEST
    kern = (jnp.einsum('gdp,ngpc->ngcd', cr, ab_re[:big_l], precision=hi)
            - jnp.einsum('gdp,ngpc->ngcd', ci, ab_im[:big_l], precision=hi))
    lag = np.arange(big_l)[None, :] - np.arange(big_l)[:, None]
    place = jnp.asarray(lag[:, :, None] == np.arange(big_l), F32)
    toep = jnp.einsum('stn,ngcd->gsctd', place, kern, precision=hi)
    toep = toep.reshape(g, big_l * c, big_l * c)
    skip = jnp.tile(d_skip.astype(F32).reshape(g, 1, c), (1, big_l, 1)).reshape(g, big_l * c)
    toep = toep + skip[:, :, None] * jnp.eye(big_l * c, dtype=F32)[None]
    e_re = ab_re[:big_l][::-1].transpose(1, 0, 3, 2).reshape(g, big_l * c, p_states)
    e_im = ab_im[:big_l][::-1].transpose(1, 0, 3, 2).reshape(g, big_l * c, p_states)
    w_in = jnp.concatenate([e_re, e_im, e_im, e_re], axis=-1)
    ca_re = cr[None] * pw_re[1:, :, None, :] - ci[None] * pw_im[1:, :, None, :]
    ca_im = cr[None] * pw_im[1:, :, None, :] + ci[None] * pw_re[1:, :, None, :]
    o_re = ca_re.transpose(1, 3, 0, 2).reshape(g, p_states, big_l * c)
    o_im = -ca_im.transpose(1, 3, 0, 2).reshape(g, p_states, big_l * c)
    w_out = jnp.concatenate([o_re, o_im], axis=1)
    are, aim = pw_re[big_l], pw_im[big_l]
    carry = jnp.stack([jnp.concatenate([are, are], -1),
                       jnp.concatenate([-aim, aim], -1),
                       jnp.concatenate([aim, -aim], -1)], axis=1)
    return toep.astype(BF16), w_in.astype(BF16), w_out.astype(BF16), carry


def _ssm_kernel(u_ref, toep_ref, win_ref, wout_ref, carry_ref, y_ref, e_sc, prev_sc, *, batch):
    n_groups = u_ref.shape[0]
    n_chunks = u_ref.shape[1] // batch
    half = LANES
    coef = []
    for g in range(n_groups):
        e = _dot(u_ref[g], win_ref[g])
        e_sc[g, 0] = e[:, :half]
        e_sc[g, 1] = e[:, half:]
        coef.append([jnp.broadcast_to(carry_ref[g, r:r + 1, :], (batch, half)) for r in range(3)])

    def step(kk, state):
        rows = pl.ds(kk, batch, stride=n_chunks)
        out = []
        for g in range(n_groups):
            st, st_swapped = state[2 * g], state[2 * g + 1]
            a1, a2, a3 = coef[g]
            prev_sc[g, rows, :] = st
            out.append(a1 * st + a2 * st_swapped + e_sc[g, 0, rows, :])
            out.append(a1 * st_swapped + a3 * st + e_sc[g, 1, rows, :])
        return tuple(out)

    zero = jnp.zeros((batch, half), F32)
    lax.fori_loop(0, n_chunks, step, (zero,) * (2 * n_groups), unroll=8)
    for g in range(n_groups):
        y = _dot(u_ref[g], toep_ref[g]) + _dot(prev_sc[g].astype(BF16), wout_ref[g])
        y_ref[g] = y.astype(y_ref.dtype)


def _ssm(u_t, toep, w_in, w_out, carry, *, batch, groups_per_step=2):
    g, rows, width = u_t.shape
    gps = groups_per_step
    vmem = gps * (2 * 2 * rows * width * 2 + rows * width * 4 + rows * LANES * 4 + 2 * rows * width * 4)

    def per_step(*tail):
        return pl.BlockSpec((gps,) + tail, lambda gi: (gi,) + (0,) * len(tail))

    return pl.pallas_call(
        functools.partial(_ssm_kernel, batch=batch),
        grid=(g // gps,),
        in_specs=[per_step(rows, width), per_step(width, width), per_step(width, width),
                  per_step(LANES, width), per_step(3, LANES)],
        out_specs=per_step(rows, width),
        out_shape=jax.ShapeDtypeStruct((g, rows, width), BF16),
        scratch_shapes=[pltpu.VMEM((gps, 2, rows, LANES), F32), pltpu.VMEM((gps, rows, LANES), F32)],
        compiler_params=pltpu.CompilerParams(
            dimension_semantics=("parallel",), vmem_limit_bytes=_vmem_limit(vmem)),
        name="s5_ssm",
    )(u_t, toep, w_in, w_out, carry)


def _final_kernel(x_ref, p_ref, oa_ref, yf_ref, za_ref, zs_ref, ga_ref, gs_ref,
                  wap_ref, wglu_ref, wsp_ref, wout_ref, wpg_ref, wpp_ref, lng_ref, lnb_ref, o_ref, ys_sc,
                  *, sub_rows):
    for sub in range(x_ref.shape[0] // sub_rows):
        _final_rows(sub, sub_rows, x_ref, p_ref, oa_ref, yf_ref, za_ref, zs_ref, ga_ref, gs_ref,
                    wap_ref, wglu_ref, wsp_ref, wout_ref, wpg_ref, wpp_ref, lng_ref, lnb_ref, o_ref, ys_sc)


def _final_rows(sub, sub_rows, x_ref, p_ref, oa_ref, yf_ref, za_ref, zs_ref, ga_ref, gs_ref,
                wap_ref, wglu_ref, wsp_ref, wout_ref, wpg_ref, wpp_ref, lng_ref, lnb_ref, o_ref, ys_sc):
    rows = slice(sub * sub_rows, (sub + 1) * sub_rows)
    n_chunks = sub_rows // SSM_CHUNK
    chunks = slice(sub * n_chunks, (sub + 1) * n_chunks)
    half = wpg_ref.shape[1] // 2
    x = x_ref[rows, :]
    xb = x.astype(BF16)
    za = za_ref[rows, :]
    a_in = oa_ref[rows, :] * (za * _sigmoid(za))
    y_a = _dot(a_in, wap_ref[...])
    gate_lo = _dot(xb, wpg_ref[:, :half])
    groups_per_tile = LANES // SSM_GROUP
    folded = [yf_ref[g, chunks, :].astype(F32) for g in range(SSM_GROUPS)]
    for t in range(SSM_CHUNK):
        for gb in range(ys_sc.shape[1]):
            tile = jnp.concatenate(
                [folded[gb * groups_per_tile + gl][:, t * SSM_GROUP:(t + 1) * SSM_GROUP]
                 for gl in range(groups_per_tile)], axis=1)
            ys_sc[sub, gb, pl.ds(t, n_chunks, stride=SSM_CHUNK), :] = tile
    ys = jnp.concatenate([ys_sc[sub, gb] for gb in range(ys_sc.shape[1])], axis=1)
    gelu = 0.5 * ys * (1.0 + lax.erf(ys * (2.0 ** -0.5)))
    glu = _dot(gelu.astype(BF16), wglu_ref[...])
    gate_hi = _dot(xb, wpg_ref[:, half:])
    zs = zs_ref[rows, :]
    s_in = glu[:, :SSM_WIDTH] * _sigmoid(glu[:, SSM_WIDTH:]) * (zs * _sigmoid(zs)).astype(F32)
    y_s = _dot(s_in.astype(BF16), wsp_ref[...])
    emb = _dot(p_ref[rows, :].astype(BF16), wpp_ref[...])
    merge = _sigmoid(ga_ref[rows, :]) * y_a.astype(BF16) + _sigmoid(gs_ref[rows, :]) * y_s.astype(BF16)
    mix = _dot(merge, wout_ref[...])
    ple = _sigmoid(jnp.concatenate([gate_lo, gate_hi], axis=1)) * emb
    hsum = DEEPNORM_ALPHA * x + mix + ple
    mu = jnp.mean(hsum, axis=-1, keepdims=True)
    cen = hsum - mu
    var = jnp.mean(cen * cen, axis=-1, keepdims=True)
    o_ref[rows, :] = cen * lax.rsqrt(var + LN_EPS) * lng_ref[...] + lnb_ref[...]


def _final(x2d, p2d, proj, o_a, y_fold, w_ap, w_glu, w_sp, w_out, w_pg, w_pp, ln_g, ln_b, *,
           row_tile=512, sub_rows=256):
    m = x2d.shape[0]
    half, full = SSM_WIDTH, D_MODEL

    def rows(width, col):
        return pl.BlockSpec((row_tile, width), lambda i: (i, col))

    def whole(arr):
        return pl.BlockSpec(arr.shape, lambda i: (0, 0))

    weights = (w_ap, w_glu, w_sp, w_out, w_pg, w_pp, ln_g, ln_b)
    vmem = (2 * sum(int(np.prod(w.shape)) * w.dtype.itemsize for w in weights)
            + 2 * row_tile * (2 * full * 4 + PLE_DIM * 4 + (4 * half + 2 * full) * 2)
            + 12 * row_tile * full * 4)
    return pl.pallas_call(
        functools.partial(_final_kernel, sub_rows=sub_rows),
        grid=(m // row_tile,),
        in_specs=[rows(full, 0), rows(PLE_DIM, 0), rows(half, 0),
                  pl.BlockSpec((SSM_GROUPS, row_tile // SSM_CHUNK, SSM_CHUNK * SSM_GROUP), lambda i: (0, i, 0)),
                  rows(half, COL_ZA), rows(half, COL_ZS), rows(full, COL_GA), rows(full, COL_GS)]
                 + [whole(w) for w in weights],
        out_specs=rows(full, 0),
        out_shape=jax.ShapeDtypeStruct((m, full), F32),
        scratch_shapes=[pltpu.VMEM((row_tile // sub_rows, half // LANES, sub_rows, LANES), F32)],
        compiler_params=pltpu.CompilerParams(
            dimension_semantics=("parallel",), vmem_limit_bytes=_vmem_limit(vmem)),
        name="final",
    )(x2d, p2d, o_a, y_fold, proj, proj, proj, proj, *weights)


def kernel(x, p, w_in, w_attn_proj, w_ssm_proj, w_out, ssm_a_re, ssm_a_im, ssm_log_dt, ssm_b_re, ssm_b_im, ssm_c_re, ssm_c_im, ssm_d, w_glu, w_ple_gate, w_ple_proj, ln_g, ln_b, rel_bias):
    b, s, d = x.shape
    m = b * s
    for i in range(w_in.shape[0]):
        x2d = x.reshape(m, d)
        wq, wk, wv, wza, wu, wzs, wga, wgs = jnp.split(w_in[i].astype(BF16), _IN_SPLITS, axis=1)
        w_main = jnp.concatenate([wga, wgs, wk, wza, wzs], axis=1)
        proj, qv_t, u_fold = _proj(x2d, w_main, jnp.concatenate([wq, wv], axis=1).T, wu, batch=b)
        o_a = _attention(proj.reshape(b, s, PROJ_WIDTH), qv_t, rel_bias.astype(F32))
        toep, s_in, s_out, carry = _ssm_weights(
            ssm_a_re[i], ssm_a_im[i], ssm_log_dt[i], ssm_b_re[i], ssm_b_im[i],
            ssm_c_re[i], ssm_c_im[i], ssm_d[i].reshape(SSM_GROUPS, SSM_GROUP))
        y_fold = _ssm(u_fold, toep, s_in, s_out, carry, batch=b)
        x2d = _final(x2d, p[i].reshape(m, PLE_DIM), proj, o_a.reshape(m, ATTN_WIDTH), y_fold,
                     w_attn_proj[i].astype(BF16), w_glu[i].astype(BF16), w_ssm_proj[i].astype(BF16),
                     w_out[i].astype(BF16), w_ple_gate[i].astype(BF16), w_ple_proj[i].astype(BF16),
                     ln_g[i].astype(F32).reshape(1, d), ln_b[i].astype(F32).reshape(1, d))
        x = x2d.reshape(b, s, d)
    return x
```

```python
import functools
import math

import numpy as np
import jax
import jax.numpy as jnp
from jax import lax
from jax.experimental import pallas as pl
from jax.experimental.pallas import tpu as pltpu

F32 = jnp.float32
BF16 = jnp.bfloat16

LANES = 128
BF16_SUBLANES = 16
V7X_VMEM_BYTES = 64 * 1024 * 1024

D_MODEL = 1024
PLE_DIM = 256
HEADS = 8
HEAD_DIM = 64
QUERY_SCALE = HEAD_DIM ** -0.5
ATTN_WIDTH = HEADS * HEAD_DIM
HEAD_PAIRS = ATTN_WIDTH // LANES
MOBA_BLOCK = 256
MOBA_TOPK = 3
REL_BUCKETS = 32
REL_MAX_DIST = 128
SSM_WIDTH = 512
SSM_GROUP = 16
SSM_GROUPS = SSM_WIDTH // SSM_GROUP
SSM_STATE = 64
SSM_CHUNK = 16
IN_WIDTH = 4 * ATTN_WIDTH + 2 * SSM_WIDTH + 2 * D_MODEL
DEPTH = 1
DEEPNORM_ALPHA = (2.0 * DEPTH) ** 0.25
LN_EPS = 1e-5
MASK_VALUE = -1e30

PROJ_WIDTH = IN_WIDTH - 3 * ATTN_WIDTH - SSM_WIDTH
_IN_SPLITS = tuple(int(v) for v in np.cumsum(
    (ATTN_WIDTH,) * 4 + (SSM_WIDTH,) * 2 + (D_MODEL,) * 2)[:-1])
COL_GA, COL_GS = 0, 1
COL_ZA, COL_ZS = 4, 5


def _dot(a, b):
    return jnp.dot(a, b, preferred_element_type=F32)


def _dot_nt(a, b):
    return lax.dot_general(a, b, (((1,), (1,)), ((), ())), preferred_element_type=F32)


def _sigmoid(v):
    return 1.0 / (1.0 + jnp.exp(-v))


def _vmem_limit(nbytes):
    return int(min(V7X_VMEM_BYTES - (4 << 20), max(nbytes + (8 << 20), 32 << 20)))


def _proj_kernel(x_ref, w_ref, wt_ref, wk_ref, wu_ref, o_ref, qt_ref, vaug_ref, kaug_ref, uf_ref, u_sc,
                 *, col_tile, tiles_per_batch):
    xb = x_ref[...].astype(BF16)
    u = _dot(xb, wu_ref[...])
    for gb in range(u_sc.shape[0]):
        u_sc[gb] = u[:, gb * LANES:(gb + 1) * LANES]
    n_cols = o_ref.shape[1]
    for start in range(0, n_cols, col_tile):
        cols = slice(start, min(start + col_tile, n_cols))
        o_ref[:, cols] = _dot(xb, w_ref[:, cols]).astype(BF16)
    t = _dot_nt(wt_ref[...], xb).astype(BF16)
    qt_ref[...] = t[:ATTN_WIDTH]
    pv_rows = vaug_ref.shape[0] // HEADS
    for h in range(HEADS):
        vaug_ref[h * pv_rows:h * pv_rows + HEAD_DIM, :] = t[ATTN_WIDTH + h * HEAD_DIM:ATTN_WIDTH + (h + 1) * HEAD_DIM]
        vaug_ref[h * pv_rows + HEAD_DIM:(h + 1) * pv_rows, :] = jnp.ones((pv_rows - HEAD_DIM, t.shape[1]), BF16)
    k = _dot(xb, wk_ref[...])
    lane = lax.broadcasted_iota(jnp.int32, (k.shape[0], LANES), 1)
    row = lax.broadcasted_iota(jnp.int32, (k.shape[0], LANES), 0)
    blocks_per_tile = k.shape[0] // MOBA_BLOCK
    block = (pl.program_id(0) % tiles_per_batch) * blocks_per_tile + sum(
        (row >= r * MOBA_BLOCK).astype(jnp.int32) for r in range(1, blocks_per_tile))
    for hp in range(HEAD_PAIRS):
        k_pair = k[:, hp * LANES:(hp + 1) * LANES]
        for h in range(2):
            head = (lane >= HEAD_DIM * h) & (lane < HEAD_DIM * (h + 1))
            onehot = jnp.where(lane - HEAD_DIM * (1 - h) == block, 1.0, 0.0)
            tile = 2 * hp + h
            kaug_ref[:, tile * LANES:(tile + 1) * LANES] = jnp.where(head, k_pair, onehot).astype(BF16)
    n_chunks = u_sc.shape[1] // SSM_CHUNK
    groups_per_tile = LANES // SSM_GROUP
    steps = [[u_sc[gb, pl.ds(t_, n_chunks, stride=SSM_CHUNK), :]
              for gb in range(u_sc.shape[0])] for t_ in range(SSM_CHUNK)]
    for g in range(SSM_GROUPS):
        gb, lo = g // groups_per_tile, (g % groups_per_tile) * SSM_GROUP
        folded = jnp.concatenate([steps[t_][gb][:, lo:lo + SSM_GROUP] for t_ in range(SSM_CHUNK)], axis=1)
        uf_ref[g] = folded.astype(BF16)


def _proj(x2d, w_bf16, wt_bf16, wk_bf16, wu_bf16, *, batch, row_tile=512, col_tile=1024):
    m, k = x2d.shape
    n = w_bf16.shape[1]
    nt = wt_bf16.shape[0]
    nk = wk_bf16.shape[1]
    nu = wu_bf16.shape[1]
    seq = m // batch
    tiles_per_batch = seq // row_tile
    fold_rows = row_tile // SSM_CHUNK
    pv_rows = HEAD_DIM + BF16_SUBLANES
    n_out = n + nt + nu + HEADS * (pv_rows + LANES)
    vmem = (2 * row_tile * k * 4 + 2 * k * (n + nt + nk + nu) * 2 + 2 * row_tile * n_out * 2
            + row_tile * k * 2 + 2 * row_tile * col_tile * 4 + row_tile * (nu + nt + nk) * 4)

    def per_batch(rows):
        return pl.BlockSpec((None, rows, row_tile), lambda i: (i // tiles_per_batch, 0, i % tiles_per_batch))

    return pl.pallas_call(
        functools.partial(_proj_kernel, col_tile=col_tile, tiles_per_batch=tiles_per_batch),
        grid=(m // row_tile,),
        in_specs=[pl.BlockSpec((row_tile, k), lambda i: (i, 0)),
                  pl.BlockSpec((k, n), lambda i: (0, 0)),
                  pl.BlockSpec((nt, k), lambda i: (0, 0)),
                  pl.BlockSpec((k, nk), lambda i: (0, 0)),
                  pl.BlockSpec((k, nu), lambda i: (0, 0))],
        out_specs=[pl.BlockSpec((row_tile, n), lambda i: (i, 0)),
                   per_batch(ATTN_WIDTH),
                   per_batch(HEADS * pv_rows),
                   pl.BlockSpec((row_tile, HEADS * LANES), lambda i: (i, 0)),
                   pl.BlockSpec((SSM_GROUPS, fold_rows, SSM_CHUNK * SSM_GROUP), lambda i: (0, i, 0))],
        out_shape=[jax.ShapeDtypeStruct((m, n), BF16),
                   jax.ShapeDtypeStruct((batch, ATTN_WIDTH, seq), BF16),
                   jax.ShapeDtypeStruct((batch, HEADS * pv_rows, seq), BF16),
                   jax.ShapeDtypeStruct((m, HEADS * LANES), BF16),
                   jax.ShapeDtypeStruct((SSM_GROUPS, m // SSM_CHUNK, SSM_CHUNK * SSM_GROUP), BF16)],
        scratch_shapes=[pltpu.VMEM((nu // LANES, row_tile, LANES), F32)],
        compiler_params=pltpu.CompilerParams(
            dimension_semantics=("parallel",), vmem_limit_bytes=_vmem_limit(vmem)),
        name="proj",
    )(x2d, w_bf16, wt_bf16, wk_bf16, wu_bf16)


def _t5_bucket_thresholds():
    max_exact = REL_BUCKETS // 2
    dist = np.arange(0, 2 * MOBA_BLOCK, dtype=np.int32)
    d = np.maximum(dist, 1).astype(np.float32)
    large = max_exact + (np.log(d / np.float32(max_exact)) / np.float32(math.log(REL_MAX_DIST / max_exact))
                         * np.float32(REL_BUCKETS - max_exact)).astype(np.int32)
    large = np.minimum(large, REL_BUCKETS - 1)
    bucket = np.where(dist < max_exact, dist, large)
    assert np.all(np.diff(bucket) >= 0) and bucket[-1] == REL_BUCKETS - 1
    return [int(np.argmax(bucket >= k)) for k in range(1, REL_BUCKETS)]


_BUCKET_THRESHOLDS = _t5_bucket_thresholds()


def _bias_strip_kernel(relb_ref, o_ref, *, lead):
    head = pl.program_id(0)
    far = relb_ref[REL_BUCKETS - 1, head]
    shape = o_ref.shape
    dist = (lax.broadcasted_iota(jnp.int32, shape, 1) - lax.broadcasted_iota(jnp.int32, shape, 0) - lead)
    val = jnp.full(shape, relb_ref[0, head] - far, F32)
    for kk, thr in enumerate(_BUCKET_THRESHOLDS, start=1):
        val = jnp.where(dist >= thr, relb_ref[kk, head] - far, val)
    o_ref[...] = jnp.where(dist >= 0, val, MASK_VALUE)


def _bias_strip(rel_bias, *, lead, width):
    return pl.pallas_call(
        functools.partial(_bias_strip_kernel, lead=lead),
        grid=(HEADS,),
        in_specs=[pl.BlockSpec(memory_space=pltpu.SMEM)],
        out_specs=pl.BlockSpec((None, MOBA_BLOCK, width), lambda h: (h, 0, 0)),
        out_shape=jax.ShapeDtypeStruct((HEADS, MOBA_BLOCK, width), F32),
        compiler_params=pltpu.CompilerParams(dimension_semantics=("parallel",)),
        name="bias_strip",
    )(rel_bias)


def _attn_kernel(qt_ref, kaug_ref, vaug_ref, strip_ref, o_ref,
                 kmean, kmean_hi, kmean_lo, qaug_t, acc_t, mcol, logits_a, logits_b, smax,
                 *, tile_blocks):
    blk = MOBA_BLOCK
    qtile = tile_blocks * blk
    n_tiles = kaug_ref.shape[0] // qtile
    n_slots = kmean.shape[1]
    pv_rows = vaug_ref.shape[0] // 2
    lane = lax.broadcasted_iota(jnp.int32, (1, LANES), 1)
    dim = lax.broadcasted_iota(jnp.int32, (LANES, qtile), 0)

    def in_head(index, h):
        return (index >= HEAD_DIM * h) & (index < HEAD_DIM * (h + 1))

    def other_off(h):
        return HEAD_DIM * (1 - h)

    def keys(h, rows):
        return kaug_ref[rows, h * LANES:(h + 1) * LANES]

    def values_t(h, rows):
        return vaug_ref[h * pv_rows:(h + 1) * pv_rows, rows]

    def _prepare():
        kmean[...] = jnp.zeros(kmean.shape, F32)
        for h in range(2):
            head = in_head(lane, h)

            def block_mean(j, carry, h=h, head=head):
                kb = keys(h, pl.ds(pl.multiple_of(j * blk, blk), blk))
                mean = jnp.sum(kb.astype(F32), axis=0, keepdims=True) * (1.0 / blk)
                kmean[h, pl.ds(j, 1), :] = jnp.where(head, mean, 0.0)
                return carry

            lax.fori_loop(0, tile_blocks * n_tiles, block_mean, 0)
            km = kmean[h]
            hi = km.astype(BF16)
            kmean_hi[h] = hi
            kmean_lo[h] = (km - hi.astype(F32)).astype(BF16)

    _prepare()
    block_id = lax.broadcasted_iota(jnp.int32, (n_slots, qtile), 0)
    qcol = lax.broadcasted_iota(jnp.int32, (n_slots, qtile), 1)
    block_in_tile = sum((qcol >= r * blk).astype(jnp.int32) for r in range(1, tile_blocks))
    slots = (logits_a, logits_b)

    def block_rows(j, t):
        b_idx = jnp.maximum(tile_blocks * (j + 1) - 1 - t, 0)
        return pl.ds(pl.multiple_of(b_idx * blk, blk), blk)

    def compute_logits(j, t, slot, biased, heads=(0, 1)):
        for h in heads:
            s = _dot(keys(h, block_rows(j, t)), qaug_t[j % 2, h])
            if biased:
                s = s + strip_ref[h, :, t * blk:t * blk + qtile]
            slots[slot][h] = s.astype(BF16)
            smax[slot, h] = jnp.max(s, axis=0, keepdims=True).astype(BF16).astype(F32)

    def softmax_update(j, t, slot, heads=(0, 1)):
        for h in heads:
            s = slots[slot][h]
            m_old = mcol[j % 2, h]
            m_new = jnp.maximum(m_old, smax[slot, h])
            p = jnp.exp(s - m_new.astype(BF16))
            acc_t[j % 2, h] = (acc_t[j % 2, h] * jnp.exp(m_old - m_new)
                               + _dot(values_t(h, block_rows(j, t)), p))
            mcol[j % 2, h] = m_new

    def select_blocks(j):
        qt = qt_ref[:, pl.ds(pl.multiple_of(j * qtile, qtile), qtile)]
        past = block_id < tile_blocks * j + block_in_tile
        for h in range(2):
            off = other_off(h)
            qs = jnp.where(in_head(dim, h), qt * QUERY_SCALE, jnp.zeros_like(qt))
            gate = _dot(kmean_hi[h], qs) + _dot(kmean_lo[h], qs)
            gate = jnp.where(past, gate, -jnp.inf)
            chosen = jnp.zeros(gate.shape, jnp.bool_)
            for _ in range(MOBA_TOPK):
                top = jnp.max(gate, axis=0, keepdims=True)
                first = jnp.min(jnp.where(gate == top, block_id, n_slots), axis=0, keepdims=True)
                hit = block_id == first
                chosen = chosen | (hit & (top > -jnp.inf))
                gate = jnp.where(hit, -jnp.inf, gate)
            qaug_t[j % 2, h] = qs
            qaug_t[j % 2, h, off:off + n_slots, :] = jnp.where(
                past & jnp.logical_not(chosen), MASK_VALUE, 0.0).astype(BF16)

    n_lead = tile_blocks // 2

    def reset_accumulators(j):
        mcol[j % 2] = jnp.full(mcol.shape[1:], -jnp.inf, F32)
        acc_t[j % 2] = jnp.zeros(acc_t.shape[1:], F32)

    def step_pair(j, pair, lead=False):
        t = 2 * pair
        for step, slot in ((t, 0), (t + 1, 1)):
            biased = lead and step + 2 <= tile_blocks
            for h in range(2):
                softmax_update(j, step, slot, (h,))
                compute_logits(j, step + 2, slot, biased, (h,))

    def query_tile(j, carry):
        n_pairs = tile_blocks * (j + 1) // 2

        n_mid = jnp.maximum(n_pairs - 1 - n_lead, 0)

        def pairs_from(first, count):
            for k in range(count):
                step_pair(j, first + k)

        def four_pairs(k, carry):
            pairs_from(n_lead + 4 * k, 4)
            return carry

        lax.fori_loop(0, n_mid // 4, four_pairs, 0)

        @pl.when(n_mid % 4 >= 2)
        def _two_more_pairs():
            pairs_from(n_lead + n_mid // 4 * 4, 2)

        @pl.when(n_mid % 2 == 1)
        def _one_more_pair():
            pairs_from(n_lead + n_mid - 1, 1)

        nxt = jnp.where(j + 1 < n_tiles, j + 1, jnp.maximum(j - 1, 0))
        t_last = 2 * (n_pairs - 1)
        select_blocks(nxt)
        reset_accumulators(nxt)
        for slot in range(2):
            for h in range(2):
                softmax_update(j, t_last + slot, slot, (h,))
                compute_logits(nxt, slot, slot, True, (h,))
        a0 = acc_t[j % 2, 0]
        a1 = acc_t[j % 2, 1]
        o_t = jnp.concatenate([a0[:HEAD_DIM] / a0[HEAD_DIM:HEAD_DIM + 1],
                               a1[:HEAD_DIM] / a1[HEAD_DIM:HEAD_DIM + 1]], axis=0)
        o_ref[pl.ds(pl.multiple_of(j * qtile, qtile), qtile), :] = o_t.T.astype(o_ref.dtype)
        for pair in range(n_lead):
            step_pair(nxt, pair, True)
        return carry

    select_blocks(0)
    reset_accumulators(0)
    compute_logits(0, 0, 0, True)
    compute_logits(0, 1, 1, True)
    for pair in range(n_lead - 1):
        step_pair(0, pair, True)
    lax.fori_loop(0, n_tiles, query_tile, 0)


def _attention(q_t, kaug, vaug_t, rel_bias, *, tile_blocks=2):
    b, s, _ = kaug.shape
    blk = MOBA_BLOCK
    qtile = tile_blocks * blk
    n_slots = HEAD_DIM // 2
    assert tile_blocks % 2 == 0 and s % qtile == 0 and s // qtile >= 2 and s // blk <= n_slots
    pv_rows = HEAD_DIM + BF16_SUBLANES
    strip_width = 2 * qtile
    strip = _bias_strip(rel_bias, lead=(tile_blocks - 1) * blk, width=strip_width)
    scratch = [
        pltpu.VMEM((2, n_slots, LANES), F32),
        pltpu.VMEM((2, n_slots, LANES), BF16),
        pltpu.VMEM((2, n_slots, LANES), BF16),
        pltpu.VMEM((2, 2, LANES, qtile), BF16),
        pltpu.VMEM((2, 2, pv_rows, qtile), F32),
        pltpu.VMEM((2, 2, 1, qtile), F32),
        pltpu.VMEM((2, blk, qtile), BF16),
        pltpu.VMEM((2, blk, qtile), BF16),
        pltpu.VMEM((2, 2, 1, qtile), F32),
    ]
    vmem = (2 * (4 * s * LANES + 2 * pv_rows * s) * 2
            + 2 * 2 * blk * strip_width * 4
            + 4 * blk * qtile * 2 + 10 * blk * qtile * 4)
    return pl.pallas_call(
        functools.partial(_attn_kernel, tile_blocks=tile_blocks),
        grid=(b, HEAD_PAIRS),
        in_specs=[
            pl.BlockSpec((None, LANES, s), lambda bi, hp: (bi, hp, 0)),
            pl.BlockSpec((None, s, 2 * LANES), lambda bi, hp: (bi, 0, hp)),
            pl.BlockSpec((None, 2 * pv_rows, s), lambda bi, hp: (bi, hp, 0)),
            pl.BlockSpec((2, blk, strip_width), lambda bi, hp: (hp, 0, 0)),
        ],
        out_specs=pl.BlockSpec((None, s, LANES), lambda bi, hp: (bi, 0, hp)),
        out_shape=jax.ShapeDtypeStruct((b, s, ATTN_WIDTH), BF16),
        scratch_shapes=scratch,
        compiler_params=pltpu.CompilerParams(
            dimension_semantics=("parallel", "parallel"),
            vmem_limit_bytes=_vmem_limit(vmem)),
        name="moba_attn",
    )(q_t, kaug, vaug_t, strip)


def _ssm_weights(a_re, a_im, log_dt, b_re, b_im, c_re, c_im, d_skip):
    g, p_states, c = b_re.shape
    big_l = SSM_CHUNK
    dt = jnp.exp(log_dt.astype(F32))[:, None]
    ar = a_re.astype(F32)
    ai = a_im.astype(F32)
    mag = jnp.exp(dt * ar)
    ang = dt * ai
    abar_re = mag * jnp.cos(ang)
    abar_im = mag * jnp.sin(ang)
    den = ar * ar + ai * ai
    nr = abar_re - 1.0
    ni = abar_im
    fr = (nr * ar + ni * ai) / den
    fi = (ni * ar - nr * ai) / den
    br = b_re.astype(F32)
    bi = b_im.astype(F32)
    bbar_re = fr[..., None] * br - fi[..., None] * bi
    bbar_im = fr[..., None] * bi + fi[..., None] * br
    n = jnp.arange(big_l + 1, dtype=F32)[:, None, None]
    pmag = jnp.exp(n * (dt * ar)[None])
    pw_re = pmag * jnp.cos(n * ang[None])
    pw_im = pmag * jnp.sin(n * ang[None])
    cr = c_re.astype(F32)
    ci = c_im.astype(F32)
    ab_re = pw_re[..., None] * bbar_re[None] - pw_im[..., None] * bbar_im[None]
    ab_im = pw_re[..., None] * bbar_im[None] + pw_im[..., None] * bbar_re[None]
    hi = lax.Precision.HIGHEST
    kern = (jnp.einsum('gdp,ngpc->ngcd', cr, ab_re[:big_l], precision=hi)
            - jnp.einsum('gdp,ngpc->ngcd', ci, ab_im[:big_l], precision=hi))
    lag = np.arange(big_l)[None, :] - np.arange(big_l)[:, None]
    place = jnp.asarray(lag[:, :, None] == np.arange(big_l), F32)
    toep = jnp.einsum('stn,ngcd->gsctd', place, kern, precision=hi)
    toep = toep.reshape(g, big_l * c, big_l * c)
    skip = jnp.tile(d_skip.astype(F32).reshape(g, 1, c), (1, big_l, 1)).reshape(g, big_l * c)
    toep = toep + skip[:, :, None] * jnp.eye(big_l * c, dtype=F32)[None]
    e_re = ab_re[:big_l][::-1].transpose(1, 0, 3, 2).reshape(g, big_l * c, p_states)
    e_im = ab_im[:big_l][::-1].transpose(1, 0, 3, 2).reshape(g, big_l * c, p_states)
    w_in = jnp.concatenate([e_re, e_im, e_im, e_re], axis=-1)
    ca_re = cr[None] * pw_re[1:, :, None, :] - ci[None] * pw_im[1:, :, None, :]
    ca_im = cr[None] * pw_im[1:, :, None, :] + ci[None] * pw_re[1:, :, None, :]
    o_re = ca_re.transpose(1, 3, 0, 2).reshape(g, p_states, big_l * c)
    o_im = -ca_im.transpose(1, 3, 0, 2).reshape(g, p_states, big_l * c)
    w_out = jnp.concatenate([o_re, o_im], axis=1)
    are, aim = pw_re[big_l], pw_im[big_l]
    carry = jnp.stack([jnp.concatenate([are, are], -1),
                       jnp.concatenate([-aim, aim], -1),
                       jnp.concatenate([aim, -aim], -1)], axis=1)
    return toep.astype(BF16), w_in.astype(BF16), w_out.astype(BF16), carry


def _ssm_kernel(u_ref, toep_ref, win_ref, wout_ref, carry_ref, y_ref, e_sc, prev_sc, *, batch):
    n_groups = u_ref.shape[0]
    n_chunks = u_ref.shape[1] // batch
    half = LANES
    coef = []
    for g in range(n_groups):
        e = _dot(u_ref[g], win_ref[g])
        e_sc[g, 0] = e[:, :half]
        e_sc[g, 1] = e[:, half:]
        coef.append([jnp.broadcast_to(carry_ref[g, r:r + 1, :], (batch, half)) for r in range(3)])

    def step(kk, state):
        rows = pl.ds(kk, batch, stride=n_chunks)
        out = []
        for g in range(n_groups):
            st, st_swapped = state[2 * g], state[2 * g + 1]
            a1, a2, a3 = coef[g]
            prev_sc[g, rows, :] = st
            out.append(a1 * st + a2 * st_swapped + e_sc[g, 0, rows, :])
            out.append(a1 * st_swapped + a3 * st + e_sc[g, 1, rows, :])
        return tuple(out)

    zero = jnp.zeros((batch, half), F32)
    lax.fori_loop(0, n_chunks, step, (zero,) * (2 * n_groups), unroll=8)
    for g in range(n_groups):
        y = _dot(u_ref[g], toep_ref[g]) + _dot(prev_sc[g].astype(BF16), wout_ref[g])
        y_ref[g] = y.astype(y_ref.dtype)


def _ssm(u_t, toep, w_in, w_out, carry, *, batch, groups_per_step=2):
    g, rows, width = u_t.shape
    gps = groups_per_step
    vmem = gps * (2 * 2 * rows * width * 2 + rows * width * 4 + rows * LANES * 4 + 2 * rows * width * 4)

    def per_step(*tail):
        return pl.BlockSpec((gps,) + tail, lambda gi: (gi,) + (0,) * len(tail))

    return pl.pallas_call(
        functools.partial(_ssm_kernel, batch=batch),
        grid=(g // gps,),
        in_specs=[per_step(rows, width), per_step(width, width), per_step(width, width),
                  per_step(LANES, width), per_step(3, LANES)],
        out_specs=per_step(rows, width),
        out_shape=jax.ShapeDtypeStruct((g, rows, width), BF16),
        scratch_shapes=[pltpu.VMEM((gps, 2, rows, LANES), F32), pltpu.VMEM((gps, rows, LANES), F32)],
        compiler_params=pltpu.CompilerParams(
            dimension_semantics=("parallel",), vmem_limit_bytes=_vmem_limit(vmem)),
        name="s5_ssm",
    )(u_t, toep, w_in, w_out, carry)


def _final_kernel(x_ref, p_ref, oa_ref, yf_ref, za_ref, zs_ref, ga_ref, gs_ref,
                  wap_ref, wglu_ref, wsp_ref, wout_ref, wpg_ref, wpp_ref, lng_ref, lnb_ref, o_ref, ys_sc,
                  *, sub_rows):
    for sub in range(x_ref.shape[0] // sub_rows):
        _final_rows(sub, sub_rows, x_ref, p_ref, oa_ref, yf_ref, za_ref, zs_ref, ga_ref, gs_ref,
                    wap_ref, wglu_ref, wsp_ref, wout_ref, wpg_ref, wpp_ref, lng_ref, lnb_ref, o_ref, ys_sc)


def _final_rows(sub, sub_rows, x_ref, p_ref, oa_ref, yf_ref, za_ref, zs_ref, ga_ref, gs_ref,
                wap_ref, wglu_ref, wsp_ref, wout_ref, wpg_ref, wpp_ref, lng_ref, lnb_ref, o_ref, ys_sc):
    rows = slice(sub * sub_rows, (sub + 1) * sub_rows)
    n_chunks = sub_rows // SSM_CHUNK
    chunks = slice(sub * n_chunks, (sub + 1) * n_chunks)
    half = wpg_ref.shape[1] // 2
    x = x_ref[rows, :]
    xb = x.astype(BF16)
    za = za_ref[rows, :]
    a_in = oa_ref[rows, :] * (za * _sigmoid(za))
    y_a = _dot(a_in, wap_ref[...])
    gate_lo = _dot(xb, wpg_ref[:, :half])
    groups_per_tile = LANES // SSM_GROUP
    folded = [yf_ref[g, chunks, :].astype(F32) for g in range(SSM_GROUPS)]
    for t in range(SSM_CHUNK):
        for gb in range(ys_sc.shape[1]):
            tile = jnp.concatenate(
                [folded[gb * groups_per_tile + gl][:, t * SSM_GROUP:(t + 1) * SSM_GROUP]
                 for gl in range(groups_per_tile)], axis=1)
            ys_sc[sub, gb, pl.ds(t, n_chunks, stride=SSM_CHUNK), :] = tile
    ys = jnp.concatenate([ys_sc[sub, gb] for gb in range(ys_sc.shape[1])], axis=1)
    gelu = 0.5 * ys * (1.0 + lax.erf(ys * (2.0 ** -0.5)))
    glu = _dot(gelu.astype(BF16), wglu_ref[...])
    gate_hi = _dot(xb, wpg_ref[:, half:])
    zs = zs_ref[rows, :]
    s_in = glu[:, :SSM_WIDTH] * _sigmoid(glu[:, SSM_WIDTH:]) * (zs * _sigmoid(zs)).astype(F32)
    y_s = _dot(s_in.astype(BF16), wsp_ref[...])
    emb = _dot(p_ref[rows, :].astype(BF16), wpp_ref[...])
    merge = _sigmoid(ga_ref[rows, :]) * y_a.astype(BF16) + _sigmoid(gs_ref[rows, :]) * y_s.astype(BF16)
    mix = _dot(merge, wout_ref[...])
    ple = _sigmoid(jnp.concatenate([gate_lo, gate_hi], axis=1)) * emb
    hsum = DEEPNORM_ALPHA * x + mix + ple
    mu = jnp.mean(hsum, axis=-1, keepdims=True)
    cen = hsum - mu
    var = jnp.mean(cen * cen, axis=-1, keepdims=True)
    o_ref[rows, :] = cen * lax.rsqrt(var + LN_EPS) * lng_ref[...] + lnb_ref[...]


def _final(x2d, p2d, proj, o_a, y_fold, w_ap, w_glu, w_sp, w_out, w_pg, w_pp, ln_g, ln_b, *,
           row_tile=512, sub_rows=256):
    m = x2d.shape[0]
    half, full = SSM_WIDTH, D_MODEL

    def rows(width, col):
        return pl.BlockSpec((row_tile, width), lambda i: (i, col))

    def whole(arr):
        return pl.BlockSpec(arr.shape, lambda i: (0, 0))

    weights = (w_ap, w_glu, w_sp, w_out, w_pg, w_pp, ln_g, ln_b)
    vmem = (2 * sum(int(np.prod(w.shape)) * w.dtype.itemsize for w in weights)
            + 2 * row_tile * (2 * full * 4 + PLE_DIM * 4 + (4 * half + 2 * full) * 2)
            + 12 * row_tile * full * 4)
    return pl.pallas_call(
        functools.partial(_final_kernel, sub_rows=sub_rows),
        grid=(m // row_tile,),
        in_specs=[rows(full, 0), rows(PLE_DIM, 0), rows(half, 0),
                  pl.BlockSpec((SSM_GROUPS, row_tile // SSM_CHUNK, SSM_CHUNK * SSM_GROUP), lambda i: (0, i, 0)),
                  rows(half, COL_ZA), rows(half, COL_ZS), rows(full, COL_GA), rows(full, COL_GS)]
                 + [whole(w) for w in weights],
        out_specs=rows(full, 0),
        out_shape=jax.ShapeDtypeStruct((m, full), F32),
        scratch_shapes=[pltpu.VMEM((row_tile // sub_rows, half // LANES, sub_rows, LANES), F32)],
        compiler_params=pltpu.CompilerParams(
            dimension_semantics=("parallel",), vmem_limit_bytes=_vmem_limit(vmem)),
        name="final",
    )(x2d, p2d, o_a, y_fold, proj, proj, proj, proj, *weights)


def kernel(x, p, w_in, w_attn_proj, w_ssm_proj, w_out, ssm_a_re, ssm_a_im, ssm_log_dt, ssm_b_re, ssm_b_im, ssm_c_re, ssm_c_im, ssm_d, w_glu, w_ple_gate, w_ple_proj, ln_g, ln_b, rel_bias):
    b, s, d = x.shape
    m = b * s
    for i in range(w_in.shape[0]):
        x2d = x.reshape(m, d)
        wq, wk, wv, wza, wu, wzs, wga, wgs = jnp.split(w_in[i].astype(BF16), _IN_SPLITS, axis=1)
        w_main = jnp.concatenate([wga, wgs, wza, wzs], axis=1)
        proj, q_t, vaug_t, kaug, u_fold = _proj(
            x2d, w_main, jnp.concatenate([wq, wv], axis=1).T, wk, wu, batch=b)
        o_a = _attention(q_t, kaug.reshape(b, s, HEADS * LANES), vaug_t, rel_bias.astype(F32))
        toep, s_in, s_out, carry = _ssm_weights(
            ssm_a_re[i], ssm_a_im[i], ssm_log_dt[i], ssm_b_re[i], ssm_b_im[i],
            ssm_c_re[i], ssm_c_im[i], ssm_d[i].reshape(SSM_GROUPS, SSM_GROUP))
        y_fold = _ssm(u_fold, toep, s_in, s_out, carry, batch=b)
        x2d = _final(x2d, p[i].reshape(m, PLE_DIM), proj, o_a.reshape(m, ATTN_WIDTH), y_fold,
                     w_attn_proj[i].astype(BF16), w_glu[i].astype(BF16), w_ssm_proj[i].astype(BF16),
                     w_out[i].astype(BF16), w_ple_gate[i].astype(BF16), w_ple_proj[i].astype(BF16),
                     ln_g[i].astype(F32).reshape(1, d), ln_b[i].astype(F32).reshape(1, d))
        x = x2d.reshape(b, s, d)
    return x
```

```python
import functools
import math

import numpy as np
import jax
import jax.numpy as jnp
from jax import lax
from jax.experimental import pallas as pl
from jax.experimental.pallas import tpu as pltpu

F32 = jnp.float32
BF16 = jnp.bfloat16

LANES = 128
BF16_SUBLANES = 16
V7X_VMEM_BYTES = 64 * 1024 * 1024

D_MODEL = 1024
PLE_DIM = 256
HEADS = 8
HEAD_DIM = 64
QUERY_SCALE = HEAD_DIM ** -0.5
ATTN_WIDTH = HEADS * HEAD_DIM
HEAD_PAIRS = ATTN_WIDTH // LANES
MOBA_BLOCK = 256
MOBA_TOPK = 3
REL_BUCKETS = 32
REL_MAX_DIST = 128
SSM_WIDTH = 512
SSM_GROUP = 16
SSM_GROUPS = SSM_WIDTH // SSM_GROUP
SSM_STATE = 64
SSM_CHUNK = 16
IN_WIDTH = 4 * ATTN_WIDTH + 2 * SSM_WIDTH + 2 * D_MODEL
DEPTH = 1
DEEPNORM_ALPHA = (2.0 * DEPTH) ** 0.25
LN_EPS = 1e-5
MASK_VALUE = -1e30

PROJ_WIDTH = IN_WIDTH - 3 * ATTN_WIDTH - SSM_WIDTH
_IN_SPLITS = tuple(int(v) for v in np.cumsum(
    (ATTN_WIDTH,) * 4 + (SSM_WIDTH,) * 2 + (D_MODEL,) * 2)[:-1])
COL_GA, COL_GS = 0, 1
COL_ZA, COL_ZS = 4, 5


def _dot(a, b):
    return jnp.dot(a, b, preferred_element_type=F32)


def _dot_nt(a, b):
    return lax.dot_general(a, b, (((1,), (1,)), ((), ())), preferred_element_type=F32)


def _sigmoid(v):
    return 1.0 / (1.0 + jnp.exp(-v))


VMEM_INTERNAL_ALLOWANCE = 8 << 20
VMEM_LIMIT_FLOOR = 32 << 20
VMEM_LIMIT_CEILING = V7X_VMEM_BYTES - (4 << 20)


def _vmem_limit(nbytes):
    return int(min(VMEM_LIMIT_CEILING, max(nbytes + VMEM_INTERNAL_ALLOWANCE, VMEM_LIMIT_FLOOR)))


def _proj_kernel(x_ref, w_ref, wt_ref, wk_ref, wu_ref, o_ref, qt_ref, vaug_ref, kaug_ref, uf_ref, u_sc,
                 *, col_tile, tiles_per_batch):
    xb = x_ref[...].astype(BF16)
    u = _dot(xb, wu_ref[...])
    for gb in range(u_sc.shape[0]):
        u_sc[gb] = u[:, gb * LANES:(gb + 1) * LANES]
    n_cols = o_ref.shape[1]
    for start in range(0, n_cols, col_tile):
        cols = slice(start, min(start + col_tile, n_cols))
        o_ref[:, cols] = _dot(xb, w_ref[:, cols]).astype(BF16)
    t = _dot_nt(wt_ref[...], xb).astype(BF16)
    qt_ref[...] = t[:ATTN_WIDTH]
    pv_rows = vaug_ref.shape[0] // HEADS
    for h in range(HEADS):
        vaug_ref[h * pv_rows:h * pv_rows + HEAD_DIM, :] = t[ATTN_WIDTH + h * HEAD_DIM:ATTN_WIDTH + (h + 1) * HEAD_DIM]
        vaug_ref[h * pv_rows + HEAD_DIM:(h + 1) * pv_rows, :] = jnp.ones((pv_rows - HEAD_DIM, t.shape[1]), BF16)
    k = _dot(xb, wk_ref[...])
    lane = lax.broadcasted_iota(jnp.int32, (k.shape[0], LANES), 1)
    row = lax.broadcasted_iota(jnp.int32, (k.shape[0], LANES), 0)
    blocks_per_tile = k.shape[0] // MOBA_BLOCK
    block = (pl.program_id(0) % tiles_per_batch) * blocks_per_tile + sum(
        (row >= r * MOBA_BLOCK).astype(jnp.int32) for r in range(1, blocks_per_tile))
    for hp in range(HEAD_PAIRS):
        k_pair = k[:, hp * LANES:(hp + 1) * LANES]
        for h in range(2):
            head = (lane >= HEAD_DIM * h) & (lane < HEAD_DIM * (h + 1))
            onehot = jnp.where(lane - HEAD_DIM * (1 - h) == block, 1.0, 0.0)
            tile = 2 * hp + h
            kaug_ref[:, tile * LANES:(tile + 1) * LANES] = jnp.where(head, k_pair, onehot).astype(BF16)
    n_chunks = u_sc.shape[1] // SSM_CHUNK
    groups_per_tile = LANES // SSM_GROUP
    steps = [[u_sc[gb, pl.ds(t_, n_chunks, stride=SSM_CHUNK), :]
              for gb in range(u_sc.shape[0])] for t_ in range(SSM_CHUNK)]
    for g in range(SSM_GROUPS):
        gb, lo = g // groups_per_tile, (g % groups_per_tile) * SSM_GROUP
        folded = jnp.concatenate([steps[t_][gb][:, lo:lo + SSM_GROUP] for t_ in range(SSM_CHUNK)], axis=1)
        uf_ref[g] = folded.astype(BF16)


def _proj(x2d, w_bf16, wt_bf16, wk_bf16, wu_bf16, *, batch, row_tile=512, col_tile=1024):
    m, k = x2d.shape
    n = w_bf16.shape[1]
    nt = wt_bf16.shape[0]
    nk = wk_bf16.shape[1]
    nu = wu_bf16.shape[1]
    seq = m // batch
    tiles_per_batch = seq // row_tile
    fold_rows = row_tile // SSM_CHUNK
    pv_rows = HEAD_DIM + BF16_SUBLANES
    n_out = n + nt + nu + HEADS * (pv_rows + LANES)
    vmem = (2 * row_tile * k * 4 + 2 * k * (n + nt + nk + nu) * 2 + 2 * row_tile * n_out * 2
            + row_tile * k * 2 + 2 * row_tile * col_tile * 4 + row_tile * (nu + nt + nk) * 4)

    def per_batch(rows):
        return pl.BlockSpec((None, rows, row_tile), lambda i: (i // tiles_per_batch, 0, i % tiles_per_batch))

    return pl.pallas_call(
        functools.partial(_proj_kernel, col_tile=col_tile, tiles_per_batch=tiles_per_batch),
        grid=(m // row_tile,),
        in_specs=[pl.BlockSpec((row_tile, k), lambda i: (i, 0)),
                  pl.BlockSpec((k, n), lambda i: (0, 0)),
                  pl.BlockSpec((nt, k), lambda i: (0, 0)),
                  pl.BlockSpec((k, nk), lambda i: (0, 0)),
                  pl.BlockSpec((k, nu), lambda i: (0, 0))],
        out_specs=[pl.BlockSpec((row_tile, n), lambda i: (i, 0)),
                   per_batch(ATTN_WIDTH),
                   per_batch(HEADS * pv_rows),
                   pl.BlockSpec((row_tile, HEADS * LANES), lambda i: (i, 0)),
                   pl.BlockSpec((SSM_GROUPS, fold_rows, SSM_CHUNK * SSM_GROUP), lambda i: (0, i, 0))],
        out_shape=[jax.ShapeDtypeStruct((m, n), BF16),
                   jax.ShapeDtypeStruct((batch, ATTN_WIDTH, seq), BF16),
                   jax.ShapeDtypeStruct((batch, HEADS * pv_rows, seq), BF16),
                   jax.ShapeDtypeStruct((m, HEADS * LANES), BF16),
                   jax.ShapeDtypeStruct((SSM_GROUPS, m // SSM_CHUNK, SSM_CHUNK * SSM_GROUP), BF16)],
        scratch_shapes=[pltpu.VMEM((nu // LANES, row_tile, LANES), F32)],
        compiler_params=pltpu.CompilerParams(
            dimension_semantics=("parallel",), vmem_limit_bytes=_vmem_limit(vmem)),
        name="proj",
    )(x2d, w_bf16, wt_bf16, wk_bf16, wu_bf16)


def _t5_bucket_thresholds():
    max_exact = REL_BUCKETS // 2
    dist = np.arange(0, 2 * MOBA_BLOCK, dtype=np.int32)
    d = np.maximum(dist, 1).astype(np.float32)
    large = max_exact + (np.log(d / np.float32(max_exact)) / np.float32(math.log(REL_MAX_DIST / max_exact))
                         * np.float32(REL_BUCKETS - max_exact)).astype(np.int32)
    large = np.minimum(large, REL_BUCKETS - 1)
    bucket = np.where(dist < max_exact, dist, large)
    assert np.all(np.diff(bucket) >= 0) and bucket[-1] == REL_BUCKETS - 1
    return [int(np.argmax(bucket >= k)) for k in range(1, REL_BUCKETS)]


_BUCKET_THRESHOLDS = _t5_bucket_thresholds()


def _bias_strip_kernel(relb_ref, o_ref, *, lead):
    head = pl.program_id(0)
    far = relb_ref[REL_BUCKETS - 1, head]
    shape = o_ref.shape
    dist = (lax.broadcasted_iota(jnp.int32, shape, 1) - lax.broadcasted_iota(jnp.int32, shape, 0) - lead)
    val = jnp.full(shape, relb_ref[0, head] - far, F32)
    for kk, thr in enumerate(_BUCKET_THRESHOLDS, start=1):
        val = jnp.where(dist >= thr, relb_ref[kk, head] - far, val)
    o_ref[...] = jnp.where(dist >= 0, val, MASK_VALUE)


def _bias_strip(rel_bias, *, lead, width):
    return pl.pallas_call(
        functools.partial(_bias_strip_kernel, lead=lead),
        grid=(HEADS,),
        in_specs=[pl.BlockSpec(memory_space=pltpu.SMEM)],
        out_specs=pl.BlockSpec((None, MOBA_BLOCK, width), lambda h: (h, 0, 0)),
        out_shape=jax.ShapeDtypeStruct((HEADS, MOBA_BLOCK, width), F32),
        compiler_params=pltpu.CompilerParams(dimension_semantics=("parallel",)),
        name="bias_strip",
    )(rel_bias)


def _attn_kernel(qt_ref, kaug_ref, vaug_ref, strip_ref, o_ref,
                 kmean, kmean_hi, kmean_lo, qaug_t, acc_t, mcol, logits_a, logits_b, smax,
                 *, tile_blocks):
    blk = MOBA_BLOCK
    qtile = tile_blocks * blk
    n_tiles = kaug_ref.shape[0] // qtile
    n_slots = kmean.shape[1]
    pv_rows = vaug_ref.shape[0] // 2
    lane = lax.broadcasted_iota(jnp.int32, (1, LANES), 1)
    dim = lax.broadcasted_iota(jnp.int32, (LANES, qtile), 0)

    def in_head(index, h):
        return (index >= HEAD_DIM * h) & (index < HEAD_DIM * (h + 1))

    def other_off(h):
        return HEAD_DIM * (1 - h)

    def keys(h, rows):
        return kaug_ref[rows, h * LANES:(h + 1) * LANES]

    def values_t(h, rows):
        return vaug_ref[h * pv_rows:(h + 1) * pv_rows, rows]

    def _prepare():
        kmean[...] = jnp.zeros(kmean.shape, F32)
        for h in range(2):
            head = in_head(lane, h)

            def block_mean(j, carry, h=h, head=head):
                kb = keys(h, pl.ds(pl.multiple_of(j * blk, blk), blk))
                mean = jnp.sum(kb.astype(F32), axis=0, keepdims=True) * (1.0 / blk)
                kmean[h, pl.ds(j, 1), :] = jnp.where(head, mean, 0.0)
                return carry

            lax.fori_loop(0, tile_blocks * n_tiles, block_mean, 0)
            km = kmean[h]
            hi = km.astype(BF16)
            kmean_hi[h] = hi
            kmean_lo[h] = (km - hi.astype(F32)).astype(BF16)

    _prepare()
    block_id = lax.broadcasted_iota(jnp.int32, (n_slots, qtile), 0)
    qcol = lax.broadcasted_iota(jnp.int32, (n_slots, qtile), 1)
    block_in_tile = sum((qcol >= r * blk).astype(jnp.int32) for r in range(1, tile_blocks))
    slots = (logits_a, logits_b)

    def block_rows(j, t):
        b_idx = jnp.maximum(tile_blocks * (j + 1) - 1 - t, 0)
        return pl.ds(pl.multiple_of(b_idx * blk, blk), blk)

    def compute_logits(j, t, slot, biased, heads=(0, 1)):
        for h in heads:
            s = _dot(keys(h, block_rows(j, t)), qaug_t[j % 2, h])
            if biased:
                s = s + strip_ref[h, :, t * blk:t * blk + qtile]
            slots[slot][h] = s.astype(BF16)
            smax[slot, h] = jnp.max(s, axis=0, keepdims=True).astype(BF16).astype(F32)

    def softmax_update(j, t, slot, heads=(0, 1)):
        for h in heads:
            s = slots[slot][h]
            m_old = mcol[j % 2, h]
            m_new = jnp.maximum(m_old, smax[slot, h])
            p = jnp.exp(s - m_new.astype(BF16))
            acc_t[j % 2, h] = (acc_t[j % 2, h] * jnp.exp(m_old - m_new)
                               + _dot(values_t(h, block_rows(j, t)), p))
            mcol[j % 2, h] = m_new

    def select_blocks(j):
        qt = qt_ref[:, pl.ds(pl.multiple_of(j * qtile, qtile), qtile)]
        past = block_id < tile_blocks * j + block_in_tile
        for h in range(2):
            off = other_off(h)
            qs = jnp.where(in_head(dim, h), qt * QUERY_SCALE, jnp.zeros_like(qt))
            gate = _dot(kmean_hi[h], qs) + _dot(kmean_lo[h], qs)
            gate = jnp.where(past, gate, -jnp.inf)
            chosen = jnp.zeros(gate.shape, jnp.bool_)
            for _ in range(MOBA_TOPK):
                top = jnp.max(gate, axis=0, keepdims=True)
                first = jnp.min(jnp.where(gate == top, block_id, n_slots), axis=0, keepdims=True)
                hit = block_id == first
                chosen = chosen | (hit & (top > -jnp.inf))
                gate = jnp.where(hit, -jnp.inf, gate)
            qaug_t[j % 2, h] = qs
            qaug_t[j % 2, h, off:off + n_slots, :] = jnp.where(
                past & jnp.logical_not(chosen), MASK_VALUE, 0.0).astype(BF16)

    n_lead = tile_blocks // 2

    def reset_accumulators(j):
        mcol[j % 2] = jnp.full(mcol.shape[1:], -jnp.inf, F32)
        acc_t[j % 2] = jnp.zeros(acc_t.shape[1:], F32)

    def step_pair(j, pair, lead=False):
        t = 2 * pair
        for step, slot in ((t, 0), (t + 1, 1)):
            biased = lead and step + 2 <= tile_blocks
            for h in range(2):
                softmax_update(j, step, slot, (h,))
                compute_logits(j, step + 2, slot, biased, (h,))

    def query_tile(j, carry):
        n_pairs = tile_blocks * (j + 1) // 2

        n_mid = jnp.maximum(n_pairs - 1 - n_lead, 0)

        def pairs_from(first, count):
            for k in range(count):
                step_pair(j, first + k)

        def four_pairs(k, carry):
            pairs_from(n_lead + 4 * k, 4)
            return carry

        lax.fori_loop(0, n_mid // 4, four_pairs, 0)

        @pl.when(n_mid % 4 >= 2)
        def _two_more_pairs():
            pairs_from(n_lead + n_mid // 4 * 4, 2)

        @pl.when(n_mid % 2 == 1)
        def _one_more_pair():
            pairs_from(n_lead + n_mid - 1, 1)

        nxt = jnp.where(j + 1 < n_tiles, j + 1, jnp.maximum(j - 1, 0))
        t_last = 2 * (n_pairs - 1)
        select_blocks(nxt)
        reset_accumulators(nxt)
        for slot in range(2):
            for h in range(2):
                softmax_update(j, t_last + slot, slot, (h,))
                compute_logits(nxt, slot, slot, True, (h,))
        a0 = acc_t[j % 2, 0]
        a1 = acc_t[j % 2, 1]
        o_t = jnp.concatenate([a0[:HEAD_DIM] / a0[HEAD_DIM:HEAD_DIM + 1],
                               a1[:HEAD_DIM] / a1[HEAD_DIM:HEAD_DIM + 1]], axis=0)
        o_ref[pl.ds(pl.multiple_of(j * qtile, qtile), qtile), :] = o_t.T.astype(o_ref.dtype)
        for pair in range(n_lead):
            step_pair(nxt, pair, True)
        return carry

    select_blocks(0)
    reset_accumulators(0)
    compute_logits(0, 0, 0, True)
    compute_logits(0, 1, 1, True)
    for pair in range(n_lead - 1):
        step_pair(0, pair, True)
    lax.fori_loop(0, n_tiles, query_tile, 0)


def _attention(q_t, kaug, vaug_t, rel_bias, *, tile_blocks=2):
    b, s, _ = kaug.shape
    blk = MOBA_BLOCK
    qtile = tile_blocks * blk
    n_slots = HEAD_DIM // 2
    assert tile_blocks % 2 == 0 and s % qtile == 0 and s // qtile >= 2 and s // blk <= n_slots
    pv_rows = HEAD_DIM + BF16_SUBLANES
    strip_width = 2 * qtile
    strip = _bias_strip(rel_bias, lead=(tile_blocks - 1) * blk, width=strip_width)
    scratch = [
        pltpu.VMEM((2, n_slots, LANES), F32),
        pltpu.VMEM((2, n_slots, LANES), BF16),
        pltpu.VMEM((2, n_slots, LANES), BF16),
        pltpu.VMEM((2, 2, LANES, qtile), BF16),
        pltpu.VMEM((2, 2, pv_rows, qtile), F32),
        pltpu.VMEM((2, 2, 1, qtile), F32),
        pltpu.VMEM((2, blk, qtile), BF16),
        pltpu.VMEM((2, blk, qtile), BF16),
        pltpu.VMEM((2, 2, 1, qtile), F32),
    ]
    vmem = (2 * (4 * s * LANES + 2 * pv_rows * s) * 2
            + 2 * 2 * blk * strip_width * 4
            + 4 * blk * qtile * 2 + 10 * blk * qtile * 4)
    return pl.pallas_call(
        functools.partial(_attn_kernel, tile_blocks=tile_blocks),
        grid=(b, HEAD_PAIRS),
        in_specs=[
            pl.BlockSpec((None, LANES, s), lambda bi, hp: (bi, hp, 0)),
            pl.BlockSpec((None, s, 2 * LANES), lambda bi, hp: (bi, 0, hp)),
            pl.BlockSpec((None, 2 * pv_rows, s), lambda bi, hp: (bi, hp, 0)),
            pl.BlockSpec((2, blk, strip_width), lambda bi, hp: (hp, 0, 0)),
        ],
        out_specs=pl.BlockSpec((None, s, LANES), lambda bi, hp: (bi, 0, hp)),
        out_shape=jax.ShapeDtypeStruct((b, s, ATTN_WIDTH), BF16),
        scratch_shapes=scratch,
        compiler_params=pltpu.CompilerParams(
            dimension_semantics=("parallel", "parallel"),
            vmem_limit_bytes=_vmem_limit(vmem)),
        name="moba_attn",
    )(q_t, kaug, vaug_t, strip)


def _ssm_weights(a_re, a_im, log_dt, b_re, b_im, c_re, c_im, d_skip):
    g, p_states, c = b_re.shape
    big_l = SSM_CHUNK
    dt = jnp.exp(log_dt.astype(F32))[:, None]
    ar = a_re.astype(F32)
    ai = a_im.astype(F32)
    mag = jnp.exp(dt * ar)
    ang = dt * ai
    abar_re = mag * jnp.cos(ang)
    abar_im = mag * jnp.sin(ang)
    den = ar * ar + ai * ai
    nr = abar_re - 1.0
    ni = abar_im
    fr = (nr * ar + ni * ai) / den
    fi = (ni * ar - nr * ai) / den
    br = b_re.astype(F32)
    bi = b_im.astype(F32)
    bbar_re = fr[..., None] * br - fi[..., None] * bi
    bbar_im = fr[..., None] * bi + fi[..., None] * br
    n = jnp.arange(big_l + 1, dtype=F32)[:, None, None]
    pmag = jnp.exp(n * (dt * ar)[None])
    pw_re = pmag * jnp.cos(n * ang[None])
    pw_im = pmag * jnp.sin(n * ang[None])
    cr = c_re.astype(F32)
    ci = c_im.astype(F32)
    ab_re = pw_re[..., None] * bbar_re[None] - pw_im[..., None] * bbar_im[None]
    ab_im = pw_re[..., None] * bbar_im[None] + pw_im[..., None] * bbar_re[None]
    hi = lax.Precision.HIGHEST
    kern = (jnp.einsum('gdp,ngpc->ngcd', cr, ab_re[:big_l], precision=hi)
            - jnp.einsum('gdp,ngpc->ngcd', ci, ab_im[:big_l], precision=hi))
    lag = np.arange(big_l)[None, :] - np.arange(big_l)[:, None]
    place = jnp.asarray(lag[:, :, None] == np.arange(big_l), F32)
    toep = jnp.einsum('stn,ngcd->gsctd', place, kern, precision=hi)
    toep = toep.reshape(g, big_l * c, big_l * c)
    skip = jnp.tile(d_skip.astype(F32).reshape(g, 1, c), (1, big_l, 1)).reshape(g, big_l * c)
    toep = toep + skip[:, :, None] * jnp.eye(big_l * c, dtype=F32)[None]
    e_re = ab_re[:big_l][::-1].transpose(1, 0, 3, 2).reshape(g, big_l * c, p_states)
    e_im = ab_im[:big_l][::-1].transpose(1, 0, 3, 2).reshape(g, big_l * c, p_states)
    w_in = jnp.concatenate([e_re, e_im, e_im, e_re], axis=-1)
    ca_re = cr[None] * pw_re[1:, :, None, :] - ci[None] * pw_im[1:, :, None, :]
    ca_im = cr[None] * pw_im[1:, :, None, :] + ci[None] * pw_re[1:, :, None, :]
    o_re = ca_re.transpose(1, 3, 0, 2).reshape(g, p_states, big_l * c)
    o_im = -ca_im.transpose(1, 3, 0, 2).reshape(g, p_states, big_l * c)
    w_out = jnp.concatenate([o_re, o_im], axis=1)
    are, aim = pw_re[big_l], pw_im[big_l]
    carry = jnp.stack([jnp.concatenate([are, are], -1),
                       jnp.concatenate([-aim, aim], -1),
                       jnp.concatenate([aim, -aim], -1)], axis=1)
    return toep.astype(BF16), w_in.astype(BF16), w_out.astype(BF16), carry


def _ssm_kernel(u_ref, toep_ref, win_ref, wout_ref, carry_ref, y_ref, e_sc, prev_sc, *, batch):
    n_groups = u_ref.shape[0]
    n_chunks = u_ref.shape[1] // batch
    half = LANES
    coef = []
    for g in range(n_groups):
        e = _dot(u_ref[g], win_ref[g])
        e_sc[g, 0] = e[:, :half]
        e_sc[g, 1] = e[:, half:]
        coef.append([jnp.broadcast_to(carry_ref[g, r:r + 1, :], (batch, half)) for r in range(3)])

    def step(kk, state):
        rows = pl.ds(kk, batch, stride=n_chunks)
        out = []
        for g in range(n_groups):
            st, st_swapped = state[2 * g], state[2 * g + 1]
            a1, a2, a3 = coef[g]
            prev_sc[g, rows, :] = st
            out.append(a1 * st + a2 * st_swapped + e_sc[g, 0, rows, :])
            out.append(a1 * st_swapped + a3 * st + e_sc[g, 1, rows, :])
        return tuple(out)

    zero = jnp.zeros((batch, half), F32)
    lax.fori_loop(0, n_chunks, step, (zero,) * (2 * n_groups), unroll=8)
    for g in range(n_groups):
        y = _dot(u_ref[g], toep_ref[g]) + _dot(prev_sc[g].astype(BF16), wout_ref[g])
        y_ref[g] = y.astype(y_ref.dtype)


def _ssm(u_t, toep, w_in, w_out, carry, *, batch, groups_per_step=2):
    g, rows, width = u_t.shape
    gps = groups_per_step
    vmem = gps * (2 * 2 * rows * width * 2 + rows * width * 4 + rows * LANES * 4 + 2 * rows * width * 4)

    def per_step(*tail):
        return pl.BlockSpec((gps,) + tail, lambda gi: (gi,) + (0,) * len(tail))

    return pl.pallas_call(
        functools.partial(_ssm_kernel, batch=batch),
        grid=(g // gps,),
        in_specs=[per_step(rows, width), per_step(width, width), per_step(width, width),
                  per_step(LANES, width), per_step(3, LANES)],
        out_specs=per_step(rows, width),
        out_shape=jax.ShapeDtypeStruct((g, rows, width), BF16),
        scratch_shapes=[pltpu.VMEM((gps, 2, rows, LANES), F32), pltpu.VMEM((gps, rows, LANES), F32)],
        compiler_params=pltpu.CompilerParams(
            dimension_semantics=("parallel",), vmem_limit_bytes=_vmem_limit(vmem)),
        name="s5_ssm",
    )(u_t, toep, w_in, w_out, carry)


def _final_kernel(x_ref, p_ref, oa_ref, yf_ref, za_ref, zs_ref, ga_ref, gs_ref,
                  wap_ref, wglu_ref, wsp_ref, wout_ref, wpg_ref, wpp_ref, lng_ref, lnb_ref, o_ref, ys_sc,
                  *, sub_rows):
    for sub in range(x_ref.shape[0] // sub_rows):
        _final_rows(sub, sub_rows, x_ref, p_ref, oa_ref, yf_ref, za_ref, zs_ref, ga_ref, gs_ref,
                    wap_ref, wglu_ref, wsp_ref, wout_ref, wpg_ref, wpp_ref, lng_ref, lnb_ref, o_ref, ys_sc)


def _final_rows(sub, sub_rows, x_ref, p_ref, oa_ref, yf_ref, za_ref, zs_ref, ga_ref, gs_ref,
                wap_ref, wglu_ref, wsp_ref, wout_ref, wpg_ref, wpp_ref, lng_ref, lnb_ref, o_ref, ys_sc):
    rows = slice(sub * sub_rows, (sub + 1) * sub_rows)
    n_chunks = sub_rows // SSM_CHUNK
    chunks = slice(sub * n_chunks, (sub + 1) * n_chunks)
    half = wpg_ref.shape[1] // 2
    x = x_ref[rows, :]
    xb = x.astype(BF16)
    za = za_ref[rows, :]
    a_in = oa_ref[rows, :] * (za * _sigmoid(za))
    y_a = _dot(a_in, wap_ref[...])
    gate_lo = _dot(xb, wpg_ref[:, :half])
    groups_per_tile = LANES // SSM_GROUP
    folded = [yf_ref[g, chunks, :].astype(F32) for g in range(SSM_GROUPS)]
    for t in range(SSM_CHUNK):
        for gb in range(ys_sc.shape[1]):
            tile = jnp.concatenate(
                [folded[gb * groups_per_tile + gl][:, t * SSM_GROUP:(t + 1) * SSM_GROUP]
                 for gl in range(groups_per_tile)], axis=1)
            ys_sc[sub, gb, pl.ds(t, n_chunks, stride=SSM_CHUNK), :] = tile
    ys = jnp.concatenate([ys_sc[sub, gb] for gb in range(ys_sc.shape[1])], axis=1)
    gelu = 0.5 * ys * (1.0 + lax.erf(ys * (2.0 ** -0.5)))
    glu = _dot(gelu.astype(BF16), wglu_ref[...])
    gate_hi = _dot(xb, wpg_ref[:, half:])
    zs = zs_ref[rows, :]
    s_in = glu[:, :SSM_WIDTH] * _sigmoid(glu[:, SSM_WIDTH:]) * (zs * _sigmoid(zs)).astype(F32)
    y_s = _dot(s_in.astype(BF16), wsp_ref[...])
    emb = _dot(p_ref[rows, :].astype(BF16), wpp_ref[...])
    merge = _sigmoid(ga_ref[rows, :]) * y_a.astype(BF16) + _sigmoid(gs_ref[rows, :]) * y_s.astype(BF16)
    mix = _dot(merge, wout_ref[...])
    ple = _sigmoid(jnp.concatenate([gate_lo, gate_hi], axis=1)) * emb
    hsum = DEEPNORM_ALPHA * x + mix + ple
    mu = jnp.mean(hsum, axis=-1, keepdims=True)
    cen = hsum - mu
    var = jnp.mean(cen * cen, axis=-1, keepdims=True)
    o_ref[rows, :] = cen * lax.rsqrt(var + LN_EPS) * lng_ref[...] + lnb_ref[...]


def _final(x2d, p2d, proj, o_a, y_fold, w_ap, w_glu, w_sp, w_out, w_pg, w_pp, ln_g, ln_b, *,
           row_tile=512, sub_rows=256):
    m = x2d.shape[0]
    half, full = SSM_WIDTH, D_MODEL

    def rows(width, col):
        return pl.BlockSpec((row_tile, width), lambda i: (i, col))

    def whole(arr):
        return pl.BlockSpec(arr.shape, lambda i: (0, 0))

    weights = (w_ap, w_glu, w_sp, w_out, w_pg, w_pp, ln_g, ln_b)
    vmem = (2 * sum(int(np.prod(w.shape)) * w.dtype.itemsize for w in weights)
            + 2 * row_tile * (2 * full * 4 + PLE_DIM * 4 + (4 * half + 2 * full) * 2)
            + 12 * row_tile * full * 4)
    return pl.pallas_call(
        functools.partial(_final_kernel, sub_rows=sub_rows),
        grid=(m // row_tile,),
        in_specs=[rows(full, 0), rows(PLE_DIM, 0), rows(half, 0),
                  pl.BlockSpec((SSM_GROUPS, row_tile // SSM_CHUNK, SSM_CHUNK * SSM_GROUP), lambda i: (0, i, 0)),
                  rows(half, COL_ZA), rows(half, COL_ZS), rows(full, COL_GA), rows(full, COL_GS)]
                 + [whole(w) for w in weights],
        out_specs=rows(full, 0),
        out_shape=jax.ShapeDtypeStruct((m, full), F32),
        scratch_shapes=[pltpu.VMEM((row_tile // sub_rows, half // LANES, sub_rows, LANES), F32)],
        compiler_params=pltpu.CompilerParams(
            dimension_semantics=("parallel",), vmem_limit_bytes=_vmem_limit(vmem)),
        name="final",
    )(x2d, p2d, o_a, y_fold, proj, proj, proj, proj, *weights)


def kernel(x, p, w_in, w_attn_proj, w_ssm_proj, w_out, ssm_a_re, ssm_a_im, ssm_log_dt, ssm_b_re, ssm_b_im, ssm_c_re, ssm_c_im, ssm_d, w_glu, w_ple_gate, w_ple_proj, ln_g, ln_b, rel_bias):
    b, s, d = x.shape
    m = b * s
    for i in range(w_in.shape[0]):
        x2d = x.reshape(m, d)
        wq, wk, wv, wza, wu, wzs, wga, wgs = jnp.split(w_in[i].astype(BF16), _IN_SPLITS, axis=1)
        w_main = jnp.concatenate([wga, wgs, wza, wzs], axis=1)
        proj, q_t, vaug_t, kaug, u_fold = _proj(
            x2d, w_main, jnp.concatenate([wq, wv], axis=1).T, wk, wu, batch=b)
        o_a = _attention(q_t, kaug.reshape(b, s, HEADS * LANES), vaug_t, rel_bias.astype(F32))
        toep, s_in, s_out, carry = _ssm_weights(
            ssm_a_re[i], ssm_a_im[i], ssm_log_dt[i], ssm_b_re[i], ssm_b_im[i],
            ssm_c_re[i], ssm_c_im[i], ssm_d[i].reshape(SSM_GROUPS, SSM_GROUP))
        y_fold = _ssm(u_fold, toep, s_in, s_out, carry, batch=b)
        x2d = _final(x2d, p[i].reshape(m, PLE_DIM), proj, o_a.reshape(m, ATTN_WIDTH), y_fold,
                     w_attn_proj[i].astype(BF16), w_glu[i].astype(BF16), w_ssm_proj[i].astype(BF16),
                     w_out[i].astype(BF16), w_ple_gate[i].astype(BF16), w_ple_proj[i].astype(BF16),
                     ln_g[i].astype(F32).reshape(1, d), ln_b[i].astype(F32).reshape(1, d))
        x = x2d.reshape(b, s, d)
    return x
```

```python
import functools
import math

import numpy as np
import jax
import jax.numpy as jnp
from jax import lax
from jax.experimental import pallas as pl
from jax.experimental.pallas import tpu as pltpu

F32 = jnp.float32
BF16 = jnp.bfloat16

LANES = 128
BF16_SUBLANES = 16
V7X_VMEM_BYTES = 64 * 1024 * 1024

D_MODEL = 1024
PLE_DIM = 256
HEADS = 8
HEAD_DIM = 64
QUERY_SCALE = HEAD_DIM ** -0.5
ATTN_WIDTH = HEADS * HEAD_DIM
HEAD_PAIRS = ATTN_WIDTH // LANES
MOBA_BLOCK = 256
MOBA_TOPK = 3
REL_BUCKETS = 32
REL_MAX_DIST = 128
SSM_WIDTH = 512
SSM_GROUP = 16
SSM_GROUPS = SSM_WIDTH // SSM_GROUP
SSM_STATE = 64
SSM_CHUNK = 16
IN_WIDTH = 4 * ATTN_WIDTH + 2 * SSM_WIDTH + 2 * D_MODEL
DEPTH = 1
DEEPNORM_ALPHA = (2.0 * DEPTH) ** 0.25
LN_EPS = 1e-5
MASK_VALUE = -1e30

PROJ_WIDTH = IN_WIDTH - 3 * ATTN_WIDTH - SSM_WIDTH
_IN_SPLITS = tuple(int(v) for v in np.cumsum(
    (ATTN_WIDTH,) * 4 + (SSM_WIDTH,) * 2 + (D_MODEL,) * 2)[:-1])
COL_GA, COL_GS = 0, 1
COL_ZA, COL_ZS = 4, 5


def _dot(a, b):
    return jnp.dot(a, b, preferred_element_type=F32)


def _dot_nt(a, b):
    return lax.dot_general(a, b, (((1,), (1,)), ((), ())), preferred_element_type=F32)


def _sigmoid(v):
    return 1.0 / (1.0 + jnp.exp(-v))


VMEM_INTERNAL_ALLOWANCE = 8 << 20
VMEM_LIMIT_FLOOR = 32 << 20
VMEM_LIMIT_CEILING = V7X_VMEM_BYTES - (4 << 20)


def _vmem_limit(nbytes):
    return int(min(VMEM_LIMIT_CEILING, max(nbytes + VMEM_INTERNAL_ALLOWANCE, VMEM_LIMIT_FLOOR)))


def _proj_kernel(x_ref, w_ref, wt_ref, wk_ref, wu_ref, o_ref, qt_ref, vaug_ref, kaug_ref, uf_ref, u_sc,
                 *, col_tile, tiles_per_batch):
    xb = x_ref[...].astype(BF16)
    u = _dot(xb, wu_ref[...])
    for gb in range(u_sc.shape[0]):
        u_sc[gb] = u[:, gb * LANES:(gb + 1) * LANES]
    n_cols = o_ref.shape[1]
    for start in range(0, n_cols, col_tile):
        cols = slice(start, min(start + col_tile, n_cols))
        o_ref[:, cols] = _dot(xb, w_ref[:, cols]).astype(BF16)
    t = _dot_nt(wt_ref[...], xb).astype(BF16)
    qt_ref[...] = t[:ATTN_WIDTH]
    pv_rows = vaug_ref.shape[0] // HEADS
    for h in range(HEADS):
        vaug_ref[h * pv_rows:h * pv_rows + HEAD_DIM, :] = t[ATTN_WIDTH + h * HEAD_DIM:ATTN_WIDTH + (h + 1) * HEAD_DIM]
        vaug_ref[h * pv_rows + HEAD_DIM:(h + 1) * pv_rows, :] = jnp.ones((pv_rows - HEAD_DIM, t.shape[1]), BF16)
    k = _dot(xb, wk_ref[...])
    lane = lax.broadcasted_iota(jnp.int32, (k.shape[0], LANES), 1)
    row = lax.broadcasted_iota(jnp.int32, (k.shape[0], LANES), 0)
    blocks_per_tile = k.shape[0] // MOBA_BLOCK
    block = (pl.program_id(0) % tiles_per_batch) * blocks_per_tile + sum(
        (row >= r * MOBA_BLOCK).astype(jnp.int32) for r in range(1, blocks_per_tile))
    for hp in range(HEAD_PAIRS):
        k_pair = k[:, hp * LANES:(hp + 1) * LANES]
        for h in range(2):
            head = (lane >= HEAD_DIM * h) & (lane < HEAD_DIM * (h + 1))
            onehot = jnp.where(lane - HEAD_DIM * (1 - h) == block, 1.0, 0.0)
            tile = 2 * hp + h
            kaug_ref[:, tile * LANES:(tile + 1) * LANES] = jnp.where(head, k_pair, onehot).astype(BF16)
    n_chunks = u_sc.shape[1] // SSM_CHUNK
    groups_per_tile = LANES // SSM_GROUP
    steps = [[u_sc[gb, pl.ds(t_, n_chunks, stride=SSM_CHUNK), :]
              for gb in range(u_sc.shape[0])] for t_ in range(SSM_CHUNK)]
    for g in range(SSM_GROUPS):
        gb, lo = g // groups_per_tile, (g % groups_per_tile) * SSM_GROUP
        folded = jnp.concatenate([steps[t_][gb][:, lo:lo + SSM_GROUP] for t_ in range(SSM_CHUNK)], axis=1)
        uf_ref[g] = folded.astype(BF16)


def _proj(x2d, w_bf16, wt_bf16, wk_bf16, wu_bf16, *, batch, row_tile=512, col_tile=1024):
    m, k = x2d.shape
    n = w_bf16.shape[1]
    nt = wt_bf16.shape[0]
    nk = wk_bf16.shape[1]
    nu = wu_bf16.shape[1]
    seq = m // batch
    tiles_per_batch = seq // row_tile
    fold_rows = row_tile // SSM_CHUNK
    pv_rows = HEAD_DIM + BF16_SUBLANES
    n_out = n + nt + nu + HEADS * (pv_rows + LANES)
    vmem = (2 * row_tile * k * 4 + 2 * k * (n + nt + nk + nu) * 2 + 2 * row_tile * n_out * 2
            + row_tile * k * 2 + 2 * row_tile * col_tile * 4 + row_tile * (nu + nt + nk) * 4)

    def per_batch(rows):
        return pl.BlockSpec((None, rows, row_tile), lambda i: (i // tiles_per_batch, 0, i % tiles_per_batch))

    return pl.pallas_call(
        functools.partial(_proj_kernel, col_tile=col_tile, tiles_per_batch=tiles_per_batch),
        grid=(m // row_tile,),
        in_specs=[pl.BlockSpec((row_tile, k), lambda i: (i, 0)),
                  pl.BlockSpec((k, n), lambda i: (0, 0)),
                  pl.BlockSpec((nt, k), lambda i: (0, 0)),
                  pl.BlockSpec((k, nk), lambda i: (0, 0)),
                  pl.BlockSpec((k, nu), lambda i: (0, 0))],
        out_specs=[pl.BlockSpec((row_tile, n), lambda i: (i, 0)),
                   per_batch(ATTN_WIDTH),
                   per_batch(HEADS * pv_rows),
                   pl.BlockSpec((row_tile, HEADS * LANES), lambda i: (i, 0)),
                   pl.BlockSpec((SSM_GROUPS, fold_rows, SSM_CHUNK * SSM_GROUP), lambda i: (0, i, 0))],
        out_shape=[jax.ShapeDtypeStruct((m, n), BF16),
                   jax.ShapeDtypeStruct((batch, ATTN_WIDTH, seq), BF16),
                   jax.ShapeDtypeStruct((batch, HEADS * pv_rows, seq), BF16),
                   jax.ShapeDtypeStruct((m, HEADS * LANES), BF16),
                   jax.ShapeDtypeStruct((SSM_GROUPS, m // SSM_CHUNK, SSM_CHUNK * SSM_GROUP), BF16)],
        scratch_shapes=[pltpu.VMEM((nu // LANES, row_tile, LANES), F32)],
        compiler_params=pltpu.CompilerParams(
            dimension_semantics=("parallel",), vmem_limit_bytes=_vmem_limit(vmem)),
        name="proj",
    )(x2d, w_bf16, wt_bf16, wk_bf16, wu_bf16)


def _t5_bucket_thresholds():
    max_exact = REL_BUCKETS // 2
    dist = np.arange(0, 2 * MOBA_BLOCK, dtype=np.int32)
    d = np.maximum(dist, 1).astype(np.float32)
    large = max_exact + (np.log(d / np.float32(max_exact)) / np.float32(math.log(REL_MAX_DIST / max_exact))
                         * np.float32(REL_BUCKETS - max_exact)).astype(np.int32)
    large = np.minimum(large, REL_BUCKETS - 1)
    bucket = np.where(dist < max_exact, dist, large)
    assert np.all(np.diff(bucket) >= 0) and bucket[-1] == REL_BUCKETS - 1
    return [int(np.argmax(bucket >= k)) for k in range(1, REL_BUCKETS)]


_BUCKET_THRESHOLDS = _t5_bucket_thresholds()


def _bias_strip_kernel(relb_ref, o_ref, *, lead):
    head = pl.program_id(0)
    far = relb_ref[REL_BUCKETS - 1, head]
    shape = o_ref.shape
    dist = (lax.broadcasted_iota(jnp.int32, shape, 1) - lax.broadcasted_iota(jnp.int32, shape, 0) - lead)
    val = jnp.full(shape, relb_ref[0, head] - far, F32)
    for kk, thr in enumerate(_BUCKET_THRESHOLDS, start=1):
        val = jnp.where(dist >= thr, relb_ref[kk, head] - far, val)
    o_ref[...] = jnp.where(dist >= 0, val, MASK_VALUE)


def _bias_strip(rel_bias, *, lead, width):
    return pl.pallas_call(
        functools.partial(_bias_strip_kernel, lead=lead),
        grid=(HEADS,),
        in_specs=[pl.BlockSpec(memory_space=pltpu.SMEM)],
        out_specs=pl.BlockSpec((None, MOBA_BLOCK, width), lambda h: (h, 0, 0)),
        out_shape=jax.ShapeDtypeStruct((HEADS, MOBA_BLOCK, width), F32),
        compiler_params=pltpu.CompilerParams(dimension_semantics=("parallel",)),
        name="bias_strip",
    )(rel_bias)


def _attn_kernel(qt_ref, kaug_ref, vaug_ref, strip_ref, o_ref,
                 kmean, kmean_hi, kmean_lo, qaug_t, acc_t, mcol, logits_a, logits_b, smax,
                 *, tile_blocks):
    blk = MOBA_BLOCK
    qtile = tile_blocks * blk
    n_tiles = kaug_ref.shape[0] // qtile
    n_slots = kmean.shape[1]
    pv_rows = vaug_ref.shape[0] // 2
    lane = lax.broadcasted_iota(jnp.int32, (1, LANES), 1)
    dim = lax.broadcasted_iota(jnp.int32, (LANES, qtile), 0)

    def in_head(index, h):
        return (index >= HEAD_DIM * h) & (index < HEAD_DIM * (h + 1))

    def other_off(h):
        return HEAD_DIM * (1 - h)

    def keys(h, rows):
        return kaug_ref[rows, h * LANES:(h + 1) * LANES]

    def values_t(h, rows):
        return vaug_ref[h * pv_rows:(h + 1) * pv_rows, rows]

    def _prepare():
        kmean[...] = jnp.zeros(kmean.shape, F32)
        for h in range(2):
            head = in_head(lane, h)

            def block_mean(j, carry, h=h, head=head):
                kb = keys(h, pl.ds(pl.multiple_of(j * blk, blk), blk))
                mean = jnp.sum(kb.astype(F32), axis=0, keepdims=True) * (1.0 / blk)
                kmean[h, pl.ds(j, 1), :] = jnp.where(head, mean, 0.0)
                return carry

            lax.fori_loop(0, tile_blocks * n_tiles, block_mean, 0)
            km = kmean[h]
            hi = km.astype(BF16)
            kmean_hi[h] = hi
            kmean_lo[h] = (km - hi.astype(F32)).astype(BF16)

    _prepare()
    block_id = lax.broadcasted_iota(jnp.int32, (n_slots, qtile), 0)
    qcol = lax.broadcasted_iota(jnp.int32, (n_slots, qtile), 1)
    block_in_tile = sum((qcol >= r * blk).astype(jnp.int32) for r in range(1, tile_blocks))
    slots = (logits_a, logits_b)

    def block_rows(j, t):
        b_idx = jnp.maximum(tile_blocks * (j + 1) - 1 - t, 0)
        return pl.ds(pl.multiple_of(b_idx * blk, blk), blk)

    def compute_logits(j, t, slot, biased, heads=(0, 1)):
        for h in heads:
            s = _dot(keys(h, block_rows(j, t)), qaug_t[j % 2, h])
            if biased:
                s = s + strip_ref[h, :, t * blk:t * blk + qtile]
            slots[slot][h] = s.astype(BF16)
            smax[slot, h] = jnp.max(s, axis=0, keepdims=True).astype(BF16).astype(F32)

    def softmax_update(j, t, slot, heads=(0, 1)):
        for h in heads:
            s = slots[slot][h]
            m_old = mcol[j % 2, h]
            m_new = jnp.maximum(m_old, smax[slot, h])
            p = jnp.exp(s - m_new.astype(BF16))
            acc_t[j % 2, h] = (acc_t[j % 2, h] * jnp.exp(m_old - m_new)
                               + _dot(values_t(h, block_rows(j, t)), p))
            mcol[j % 2, h] = m_new

    def select_blocks(j):
        qt = qt_ref[:, pl.ds(pl.multiple_of(j * qtile, qtile), qtile)]
        past = block_id < tile_blocks * j + block_in_tile
        for h in range(2):
            off = other_off(h)
            qs = jnp.where(in_head(dim, h), qt * QUERY_SCALE, jnp.zeros_like(qt))
            gate = _dot(kmean_hi[h], qs) + _dot(kmean_lo[h], qs)
            gate = jnp.where(past, gate, -jnp.inf)
            chosen = jnp.zeros(gate.shape, jnp.bool_)
            for _ in range(MOBA_TOPK):
                top = jnp.max(gate, axis=0, keepdims=True)
                first = jnp.min(jnp.where(gate == top, block_id, n_slots), axis=0, keepdims=True)
                hit = block_id == first
                chosen = chosen | (hit & (top > -jnp.inf))
                gate = jnp.where(hit, -jnp.inf, gate)
            qaug_t[j % 2, h] = qs
            qaug_t[j % 2, h, off:off + n_slots, :] = jnp.where(
                past & jnp.logical_not(chosen), MASK_VALUE, 0.0).astype(BF16)

    n_lead = tile_blocks // 2

    def reset_accumulators(j):
        mcol[j % 2] = jnp.full(mcol.shape[1:], -jnp.inf, F32)
        acc_t[j % 2] = jnp.zeros(acc_t.shape[1:], F32)

    def step_pair(j, pair, lead=False):
        t = 2 * pair
        for step, slot in ((t, 0), (t + 1, 1)):
            biased = lead and step + 2 <= tile_blocks
            for h in range(2):
                softmax_update(j, step, slot, (h,))
                compute_logits(j, step + 2, slot, biased, (h,))

    def query_tile(j, carry):
        n_pairs = tile_blocks * (j + 1) // 2

        n_mid = jnp.maximum(n_pairs - 1 - n_lead, 0)

        def pairs_from(first, count):
            for k in range(count):
                step_pair(j, first + k)

        def four_pairs(k, carry):
            pairs_from(n_lead + 4 * k, 4)
            return carry

        lax.fori_loop(0, n_mid // 4, four_pairs, 0)

        @pl.when(n_mid % 4 >= 2)
        def _two_more_pairs():
            pairs_from(n_lead + n_mid // 4 * 4, 2)

        @pl.when(n_mid % 2 == 1)
        def _one_more_pair():
            pairs_from(n_lead + n_mid - 1, 1)

        nxt = jnp.where(j + 1 < n_tiles, j + 1, jnp.maximum(j - 1, 0))
        t_last = 2 * (n_pairs - 1)
        select_blocks(nxt)
        reset_accumulators(nxt)
        for slot in range(2):
            for h in range(2):
                softmax_update(j, t_last + slot, slot, (h,))
                compute_logits(nxt, slot, slot, True, (h,))
        a0 = acc_t[j % 2, 0]
        a1 = acc_t[j % 2, 1]
        o_t = jnp.concatenate([a0[:HEAD_DIM] / a0[HEAD_DIM:HEAD_DIM + 1],
                               a1[:HEAD_DIM] / a1[HEAD_DIM:HEAD_DIM + 1]], axis=0)
        o_ref[pl.ds(pl.multiple_of(j * qtile, qtile), qtile), :] = o_t.T.astype(o_ref.dtype)
        for pair in range(n_lead):
            step_pair(nxt, pair, True)
        return carry

    select_blocks(0)
    reset_accumulators(0)
    compute_logits(0, 0, 0, True)
    compute_logits(0, 1, 1, True)
    for pair in range(n_lead - 1):
        step_pair(0, pair, True)
    lax.fori_loop(0, n_tiles, query_tile, 0)


def _attention(q_t, kaug, vaug_t, rel_bias, *, tile_blocks=4):
    b, s, _ = kaug.shape
    blk = MOBA_BLOCK
    qtile = tile_blocks * blk
    n_slots = HEAD_DIM // 2
    assert tile_blocks % 2 == 0 and s % qtile == 0 and s // qtile >= 2 and s // blk <= n_slots
    pv_rows = HEAD_DIM + BF16_SUBLANES
    strip_width = 2 * qtile
    strip = _bias_strip(rel_bias, lead=(tile_blocks - 1) * blk, width=strip_width)
    scratch = [
        pltpu.VMEM((2, n_slots, LANES), F32),
        pltpu.VMEM((2, n_slots, LANES), BF16),
        pltpu.VMEM((2, n_slots, LANES), BF16),
        pltpu.VMEM((2, 2, LANES, qtile), BF16),
        pltpu.VMEM((2, 2, pv_rows, qtile), F32),
        pltpu.VMEM((2, 2, 1, qtile), F32),
        pltpu.VMEM((2, blk, qtile), BF16),
        pltpu.VMEM((2, blk, qtile), BF16),
        pltpu.VMEM((2, 2, 1, qtile), F32),
    ]
    vmem = (2 * (4 * s * LANES + 2 * pv_rows * s) * 2
            + 2 * 2 * blk * strip_width * 4
            + 4 * blk * qtile * 2 + 10 * blk * qtile * 4)
    return pl.pallas_call(
        functools.partial(_attn_kernel, tile_blocks=tile_blocks),
        grid=(b, HEAD_PAIRS),
        in_specs=[
            pl.BlockSpec((None, LANES, s), lambda bi, hp: (bi, hp, 0)),
            pl.BlockSpec((None, s, 2 * LANES), lambda bi, hp: (bi, 0, hp)),
            pl.BlockSpec((None, 2 * pv_rows, s), lambda bi, hp: (bi, hp, 0)),
            pl.BlockSpec((2, blk, strip_width), lambda bi, hp: (hp, 0, 0)),
        ],
        out_specs=pl.BlockSpec((None, s, LANES), lambda bi, hp: (bi, 0, hp)),
        out_shape=jax.ShapeDtypeStruct((b, s, ATTN_WIDTH), BF16),
        scratch_shapes=scratch,
        compiler_params=pltpu.CompilerParams(
            dimension_semantics=("parallel", "parallel"),
            vmem_limit_bytes=_vmem_limit(vmem)),
        name="moba_attn",
    )(q_t, kaug, vaug_t, strip)


def _ssm_weights(a_re, a_im, log_dt, b_re, b_im, c_re, c_im, d_skip):
    g, p_states, c = b_re.shape
    big_l = SSM_CHUNK
    dt = jnp.exp(log_dt.astype(F32))[:, None]
    ar = a_re.astype(F32)
    ai = a_im.astype(F32)
    mag = jnp.exp(dt * ar)
    ang = dt * ai
    abar_re = mag * jnp.cos(ang)
    abar_im = mag * jnp.sin(ang)
    den = ar * ar + ai * ai
    nr = abar_re - 1.0
    ni = abar_im
    fr = (nr * ar + ni * ai) / den
    fi = (ni * ar - nr * ai) / den
    br = b_re.astype(F32)
    bi = b_im.astype(F32)
    bbar_re = fr[..., None] * br - fi[..., None] * bi
    bbar_im = fr[..., None] * bi + fi[..., None] * br
    n = jnp.arange(big_l + 1, dtype=F32)[:, None, None]
    pmag = jnp.exp(n * (dt * ar)[None])
    pw_re = pmag * jnp.cos(n * ang[None])
    pw_im = pmag * jnp.sin(n * ang[None])
    cr = c_re.astype(F32)
    ci = c_im.astype(F32)
    ab_re = pw_re[..., None] * bbar_re[None] - pw_im[..., None] * bbar_im[None]
    ab_im = pw_re[..., None] * bbar_im[None] + pw_im[..., None] * bbar_re[None]
    hi = lax.Precision.HIGHEST
    kern = (jnp.einsum('gdp,ngpc->ngcd', cr, ab_re[:big_l], precision=hi)
            - jnp.einsum('gdp,ngpc->ngcd', ci, ab_im[:big_l], precision=hi))
    lag = np.arange(big_l)[None, :] - np.arange(big_l)[:, None]
    place = jnp.asarray(lag[:, :, None] == np.arange(big_l), F32)
    toep = jnp.einsum('stn,ngcd->gsctd', place, kern, precision=hi)
    toep = toep.reshape(g, big_l * c, big_l * c)
    skip = jnp.tile(d_skip.astype(F32).reshape(g, 1, c), (1, big_l, 1)).reshape(g, big_l * c)
    toep = toep + skip[:, :, None] * jnp.eye(big_l * c, dtype=F32)[None]
    e_re = ab_re[:big_l][::-1].transpose(1, 0, 3, 2).reshape(g, big_l * c, p_states)
    e_im = ab_im[:big_l][::-1].transpose(1, 0, 3, 2).reshape(g, big_l * c, p_states)
    w_in = jnp.concatenate([e_re, e_im, e_im, e_re], axis=-1)
    ca_re = cr[None] * pw_re[1:, :, None, :] - ci[None] * pw_im[1:, :, None, :]
    ca_im = cr[None] * pw_im[1:, :, None, :] + ci[None] * pw_re[1:, :, None, :]
    o_re = ca_re.transpose(1, 3, 0, 2).reshape(g, p_states, big_l * c)
    o_im = -ca_im.transpose(1, 3, 0, 2).reshape(g, p_states, big_l * c)
    w_out = jnp.concatenate([o_re, o_im], axis=1)
    are, aim = pw_re[big_l], pw_im[big_l]
    carry = jnp.stack([jnp.concatenate([are, are], -1),
                       jnp.concatenate([-aim, aim], -1),
                       jnp.concatenate([aim, -aim], -1)], axis=1)
    return toep.astype(BF16), w_in.astype(BF16), w_out.astype(BF16), carry


def _ssm_kernel(u_ref, toep_ref, win_ref, wout_ref, carry_ref, y_ref, e_sc, prev_sc, *, batch):
    n_groups = u_ref.shape[0]
    n_chunks = u_ref.shape[1] // batch
    half = LANES
    coef = []
    for g in range(n_groups):
        e = _dot(u_ref[g], win_ref[g])
        e_sc[g, 0] = e[:, :half]
        e_sc[g, 1] = e[:, half:]
        coef.append([jnp.broadcast_to(carry_ref[g, r:r + 1, :], (batch, half)) for r in range(3)])

    def step(kk, state):
        rows = pl.ds(kk, batch, stride=n_chunks)
        out = []
        for g in range(n_groups):
            st, st_swapped = state[2 * g], state[2 * g + 1]
            a1, a2, a3 = coef[g]
            prev_sc[g, rows, :] = st
            out.append(a1 * st + a2 * st_swapped + e_sc[g, 0, rows, :])
            out.append(a1 * st_swapped + a3 * st + e_sc[g, 1, rows, :])
        return tuple(out)

    zero = jnp.zeros((batch, half), F32)
    lax.fori_loop(0, n_chunks, step, (zero,) * (2 * n_groups), unroll=8)
    for g in range(n_groups):
        y = _dot(u_ref[g], toep_ref[g]) + _dot(prev_sc[g].astype(BF16), wout_ref[g])
        y_ref[g] = y.astype(y_ref.dtype)


def _ssm(u_t, toep, w_in, w_out, carry, *, batch, groups_per_step=2):
    g, rows, width = u_t.shape
    gps = groups_per_step
    vmem = gps * (2 * 2 * rows * width * 2 + rows * width * 4 + rows * LANES * 4 + 2 * rows * width * 4)

    def per_step(*tail):
        return pl.BlockSpec((gps,) + tail, lambda gi: (gi,) + (0,) * len(tail))

    return pl.pallas_call(
        functools.partial(_ssm_kernel, batch=batch),
        grid=(g // gps,),
        in_specs=[per_step(rows, width), per_step(width, width), per_step(width, width),
                  per_step(LANES, width), per_step(3, LANES)],
        out_specs=per_step(rows, width),
        out_shape=jax.ShapeDtypeStruct((g, rows, width), BF16),
        scratch_shapes=[pltpu.VMEM((gps, 2, rows, LANES), F32), pltpu.VMEM((gps, rows, LANES), F32)],
        compiler_params=pltpu.CompilerParams(
            dimension_semantics=("parallel",), vmem_limit_bytes=_vmem_limit(vmem)),
        name="s5_ssm",
    )(u_t, toep, w_in, w_out, carry)


def _final_kernel(x_ref, p_ref, oa_ref, yf_ref, za_ref, zs_ref, ga_ref, gs_ref,
                  wap_ref, wglu_ref, wsp_ref, wout_ref, wpg_ref, wpp_ref, lng_ref, lnb_ref, o_ref, ys_sc,
                  *, sub_rows):
    for sub in range(x_ref.shape[0] // sub_rows):
        _final_rows(sub, sub_rows, x_ref, p_ref, oa_ref, yf_ref, za_ref, zs_ref, ga_ref, gs_ref,
                    wap_ref, wglu_ref, wsp_ref, wout_ref, wpg_ref, wpp_ref, lng_ref, lnb_ref, o_ref, ys_sc)


def _final_rows(sub, sub_rows, x_ref, p_ref, oa_ref, yf_ref, za_ref, zs_ref, ga_ref, gs_ref,
                wap_ref, wglu_ref, wsp_ref, wout_ref, wpg_ref, wpp_ref, lng_ref, lnb_ref, o_ref, ys_sc):
    rows = slice(sub * sub_rows, (sub + 1) * sub_rows)
    n_chunks = sub_rows // SSM_CHUNK
    chunks = slice(sub * n_chunks, (sub + 1) * n_chunks)
    half = wpg_ref.shape[1] // 2
    x = x_ref[rows, :]
    xb = x.astype(BF16)
    za = za_ref[rows, :]
    a_in = oa_ref[rows, :] * (za * _sigmoid(za))
    y_a = _dot(a_in, wap_ref[...])
    gate_lo = _dot(xb, wpg_ref[:, :half])
    groups_per_tile = LANES // SSM_GROUP
    folded = [yf_ref[g, chunks, :].astype(F32) for g in range(SSM_GROUPS)]
    for t in range(SSM_CHUNK):
        for gb in range(ys_sc.shape[1]):
            tile = jnp.concatenate(
                [folded[gb * groups_per_tile + gl][:, t * SSM_GROUP:(t + 1) * SSM_GROUP]
                 for gl in range(groups_per_tile)], axis=1)
            ys_sc[sub, gb, pl.ds(t, n_chunks, stride=SSM_CHUNK), :] = tile
    ys = jnp.concatenate([ys_sc[sub, gb] for gb in range(ys_sc.shape[1])], axis=1)
    gelu = 0.5 * ys * (1.0 + lax.erf(ys * (2.0 ** -0.5)))
    glu = _dot(gelu.astype(BF16), wglu_ref[...])
    gate_hi = _dot(xb, wpg_ref[:, half:])
    zs = zs_ref[rows, :]
    s_in = glu[:, :SSM_WIDTH] * _sigmoid(glu[:, SSM_WIDTH:]) * (zs * _sigmoid(zs)).astype(F32)
    y_s = _dot(s_in.astype(BF16), wsp_ref[...])
    emb = _dot(p_ref[rows, :].astype(BF16), wpp_ref[...])
    merge = _sigmoid(ga_ref[rows, :]) * y_a.astype(BF16) + _sigmoid(gs_ref[rows, :]) * y_s.astype(BF16)
    mix = _dot(merge, wout_ref[...])
    ple = _sigmoid(jnp.concatenate([gate_lo, gate_hi], axis=1)) * emb
    hsum = DEEPNORM_ALPHA * x + mix + ple
    mu = jnp.mean(hsum, axis=-1, keepdims=True)
    cen = hsum - mu
    var = jnp.mean(cen * cen, axis=-1, keepdims=True)
    o_ref[rows, :] = cen * lax.rsqrt(var + LN_EPS) * lng_ref[...] + lnb_ref[...]


def _final(x2d, p2d, proj, o_a, y_fold, w_ap, w_glu, w_sp, w_out, w_pg, w_pp, ln_g, ln_b, *,
           row_tile=512, sub_rows=256):
    m = x2d.shape[0]
    half, full = SSM_WIDTH, D_MODEL

    def rows(width, col):
        return pl.BlockSpec((row_tile, width), lambda i: (i, col))

    def whole(arr):
        return pl.BlockSpec(arr.shape, lambda i: (0, 0))

    weights = (w_ap, w_glu, w_sp, w_out, w_pg, w_pp, ln_g, ln_b)
    vmem = (2 * sum(int(np.prod(w.shape)) * w.dtype.itemsize for w in weights)
            + 2 * row_tile * (2 * full * 4 + PLE_DIM * 4 + (4 * half + 2 * full) * 2)
            + 12 * row_tile * full * 4)
    return pl.pallas_call(
        functools.partial(_final_kernel, sub_rows=sub_rows),
        grid=(m // row_tile,),
        in_specs=[rows(full, 0), rows(PLE_DIM, 0), rows(half, 0),
                  pl.BlockSpec((SSM_GROUPS, row_tile // SSM_CHUNK, SSM_CHUNK * SSM_GROUP), lambda i: (0, i, 0)),
                  rows(half, COL_ZA), rows(half, COL_ZS), rows(full, COL_GA), rows(full, COL_GS)]
                 + [whole(w) for w in weights],
        out_specs=rows(full, 0),
        out_shape=jax.ShapeDtypeStruct((m, full), F32),
        scratch_shapes=[pltpu.VMEM((row_tile // sub_rows, half // LANES, sub_rows, LANES), F32)],
        compiler_params=pltpu.CompilerParams(
            dimension_semantics=("parallel",), vmem_limit_bytes=_vmem_limit(vmem)),
        name="final",
    )(x2d, p2d, o_a, y_fold, proj, proj, proj, proj, *weights)


def kernel(x, p, w_in, w_attn_proj, w_ssm_proj, w_out, ssm_a_re, ssm_a_im, ssm_log_dt, ssm_b_re, ssm_b_im, ssm_c_re, ssm_c_im, ssm_d, w_glu, w_ple_gate, w_ple_proj, ln_g, ln_b, rel_bias):
    b, s, d = x.shape
    m = b * s
    for i in range(w_in.shape[0]):
        x2d = x.reshape(m, d)
        wq, wk, wv, wza, wu, wzs, wga, wgs = jnp.split(w_in[i].astype(BF16), _IN_SPLITS, axis=1)
        w_main = jnp.concatenate([wga, wgs, wza, wzs], axis=1)
        proj, q_t, vaug_t, kaug, u_fold = _proj(
            x2d, w_main, jnp.concatenate([wq, wv], axis=1).T, wk, wu, batch=b)
        o_a = _attention(q_t, kaug.reshape(b, s, HEADS * LANES), vaug_t, rel_bias.astype(F32))
        toep, s_in, s_out, carry = _ssm_weights(
            ssm_a_re[i], ssm_a_im[i], ssm_log_dt[i], ssm_b_re[i], ssm_b_im[i],
            ssm_c_re[i], ssm_c_im[i], ssm_d[i].reshape(SSM_GROUPS, SSM_GROUP))
        y_fold = _ssm(u_fold, toep, s_in, s_out, carry, batch=b)
        x2d = _final(x2d, p[i].reshape(m, PLE_DIM), proj, o_a.reshape(m, ATTN_WIDTH), y_fold,
                     w_attn_proj[i].astype(BF16), w_glu[i].astype(BF16), w_ssm_proj[i].astype(BF16),
                     w_out[i].astype(BF16), w_ple_gate[i].astype(BF16), w_ple_proj[i].astype(BF16),
                     ln_g[i].astype(F32).reshape(1, d), ln_b[i].astype(F32).reshape(1, d))
        x = x2d.reshape(b, s, d)
    return x
```

```python
import functools
import math

import numpy as np
import jax
import jax.numpy as jnp
from jax import lax
from jax.experimental import pallas as pl
from jax.experimental.pallas import tpu as pltpu

F32 = jnp.float32
BF16 = jnp.bfloat16

LANES = 128
BF16_SUBLANES = 16
V7X_VMEM_BYTES = 64 * 1024 * 1024

D_MODEL = 1024
PLE_DIM = 256
HEADS = 8
HEAD_DIM = 64
QUERY_SCALE = HEAD_DIM ** -0.5
ATTN_WIDTH = HEADS * HEAD_DIM
HEAD_PAIRS = ATTN_WIDTH // LANES
MOBA_BLOCK = 256
MOBA_TOPK = 3
REL_BUCKETS = 32
REL_MAX_DIST = 128
SSM_WIDTH = 512
SSM_GROUP = 16
SSM_GROUPS = SSM_WIDTH // SSM_GROUP
SSM_STATE = 64
SSM_CHUNK = 16
IN_WIDTH = 4 * ATTN_WIDTH + 2 * SSM_WIDTH + 2 * D_MODEL
DEPTH = 1
DEEPNORM_ALPHA = (2.0 * DEPTH) ** 0.25
LN_EPS = 1e-5
MASK_VALUE = -1e30

PROJ_WIDTH = IN_WIDTH - 3 * ATTN_WIDTH - SSM_WIDTH
_IN_SPLITS = tuple(int(v) for v in np.cumsum(
    (ATTN_WIDTH,) * 4 + (SSM_WIDTH,) * 2 + (D_MODEL,) * 2)[:-1])
COL_GA, COL_GS = 0, 1
COL_ZA, COL_ZS = 4, 5


def _dot(a, b):
    return jnp.dot(a, b, preferred_element_type=F32)


def _dot_nt(a, b):
    return lax.dot_general(a, b, (((1,), (1,)), ((), ())), preferred_element_type=F32)


def _sigmoid(v):
    return 1.0 / (1.0 + jnp.exp(-v))


VMEM_INTERNAL_ALLOWANCE = 8 << 20
VMEM_LIMIT_FLOOR = 32 << 20
VMEM_LIMIT_CEILING = V7X_VMEM_BYTES - (4 << 20)


def _vmem_limit(nbytes):
    return int(min(VMEM_LIMIT_CEILING, max(nbytes + VMEM_INTERNAL_ALLOWANCE, VMEM_LIMIT_FLOOR)))


def _proj_kernel(x_ref, w_ref, wt_ref, wk_ref, wu_ref, o_ref, qt_ref, vaug_ref, kaug_ref, uf_ref, u_sc,
                 *, col_tile, tiles_per_batch):
    xb = x_ref[...].astype(BF16)
    u = _dot(xb, wu_ref[...])
    for gb in range(u_sc.shape[0]):
        u_sc[gb] = u[:, gb * LANES:(gb + 1) * LANES]
    n_cols = o_ref.shape[1]
    for start in range(0, n_cols, col_tile):
        cols = slice(start, min(start + col_tile, n_cols))
        o_ref[:, cols] = _dot(xb, w_ref[:, cols]).astype(BF16)
    t = _dot_nt(wt_ref[...], xb).astype(BF16)
    qt_ref[...] = t[:ATTN_WIDTH]
    pv_rows = vaug_ref.shape[0] // HEADS
    for h in range(HEADS):
        vaug_ref[h * pv_rows:h * pv_rows + HEAD_DIM, :] = t[ATTN_WIDTH + h * HEAD_DIM:ATTN_WIDTH + (h + 1) * HEAD_DIM]
        vaug_ref[h * pv_rows + HEAD_DIM:(h + 1) * pv_rows, :] = jnp.ones((pv_rows - HEAD_DIM, t.shape[1]), BF16)
    k = _dot(xb, wk_ref[...])
    lane = lax.broadcasted_iota(jnp.int32, (k.shape[0], LANES), 1)
    row = lax.broadcasted_iota(jnp.int32, (k.shape[0], LANES), 0)
    blocks_per_tile = k.shape[0] // MOBA_BLOCK
    block = (pl.program_id(0) % tiles_per_batch) * blocks_per_tile + sum(
        (row >= r * MOBA_BLOCK).astype(jnp.int32) for r in range(1, blocks_per_tile))
    for hp in range(HEAD_PAIRS):
        k_pair = k[:, hp * LANES:(hp + 1) * LANES]
        for h in range(2):
            head = (lane >= HEAD_DIM * h) & (lane < HEAD_DIM * (h + 1))
            onehot = jnp.where(lane - HEAD_DIM * (1 - h) == block, 1.0, 0.0)
            tile = 2 * hp + h
            kaug_ref[:, tile * LANES:(tile + 1) * LANES] = jnp.where(head, k_pair, onehot).astype(BF16)
    n_chunks = u_sc.shape[1] // SSM_CHUNK
    groups_per_tile = LANES // SSM_GROUP
    steps = [[u_sc[gb, pl.ds(t_, n_chunks, stride=SSM_CHUNK), :]
              for gb in range(u_sc.shape[0])] for t_ in range(SSM_CHUNK)]
    for g in range(SSM_GROUPS):
        gb, lo = g // groups_per_tile, (g % groups_per_tile) * SSM_GROUP
        folded = jnp.concatenate([steps[t_][gb][:, lo:lo + SSM_GROUP] for t_ in range(SSM_CHUNK)], axis=1)
        uf_ref[g] = folded.astype(BF16)


def _proj(x2d, w_bf16, wt_bf16, wk_bf16, wu_bf16, *, batch, row_tile=512, col_tile=1024):
    m, k = x2d.shape
    n = w_bf16.shape[1]
    nt = wt_bf16.shape[0]
    nk = wk_bf16.shape[1]
    nu = wu_bf16.shape[1]
    seq = m // batch
    tiles_per_batch = seq // row_tile
    fold_rows = row_tile // SSM_CHUNK
    pv_rows = HEAD_DIM + BF16_SUBLANES
    n_out = n + nt + nu + HEADS * (pv_rows + LANES)
    vmem = (2 * row_tile * k * 4 + 2 * k * (n + nt + nk + nu) * 2 + 2 * row_tile * n_out * 2
            + row_tile * k * 2 + 2 * row_tile * col_tile * 4 + row_tile * (nu + nt + nk) * 4)

    def per_batch(rows):
        return pl.BlockSpec((None, rows, row_tile), lambda i: (i // tiles_per_batch, 0, i % tiles_per_batch))

    return pl.pallas_call(
        functools.partial(_proj_kernel, col_tile=col_tile, tiles_per_batch=tiles_per_batch),
        grid=(m // row_tile,),
        in_specs=[pl.BlockSpec((row_tile, k), lambda i: (i, 0)),
                  pl.BlockSpec((k, n), lambda i: (0, 0)),
                  pl.BlockSpec((nt, k), lambda i: (0, 0)),
                  pl.BlockSpec((k, nk), lambda i: (0, 0)),
                  pl.BlockSpec((k, nu), lambda i: (0, 0))],
        out_specs=[pl.BlockSpec((row_tile, n), lambda i: (i, 0)),
                   per_batch(ATTN_WIDTH),
                   per_batch(HEADS * pv_rows),
                   pl.BlockSpec((row_tile, HEADS * LANES), lambda i: (i, 0)),
                   pl.BlockSpec((SSM_GROUPS, fold_rows, SSM_CHUNK * SSM_GROUP), lambda i: (0, i, 0))],
        out_shape=[jax.ShapeDtypeStruct((m, n), BF16),
                   jax.ShapeDtypeStruct((batch, ATTN_WIDTH, seq), BF16),
                   jax.ShapeDtypeStruct((batch, HEADS * pv_rows, seq), BF16),
                   jax.ShapeDtypeStruct((m, HEADS * LANES), BF16),
                   jax.ShapeDtypeStruct((SSM_GROUPS, m // SSM_CHUNK, SSM_CHUNK * SSM_GROUP), BF16)],
        scratch_shapes=[pltpu.VMEM((nu // LANES, row_tile, LANES), F32)],
        compiler_params=pltpu.CompilerParams(
            dimension_semantics=("parallel",), vmem_limit_bytes=_vmem_limit(vmem)),
        name="proj",
    )(x2d, w_bf16, wt_bf16, wk_bf16, wu_bf16)


def _t5_bucket_thresholds():
    max_exact = REL_BUCKETS // 2
    dist = np.arange(0, 2 * MOBA_BLOCK, dtype=np.int32)
    d = np.maximum(dist, 1).astype(np.float32)
    large = max_exact + (np.log(d / np.float32(max_exact)) / np.float32(math.log(REL_MAX_DIST / max_exact))
                         * np.float32(REL_BUCKETS - max_exact)).astype(np.int32)
    large = np.minimum(large, REL_BUCKETS - 1)
    bucket = np.where(dist < max_exact, dist, large)
    assert np.all(np.diff(bucket) >= 0) and bucket[-1] == REL_BUCKETS - 1
    return [int(np.argmax(bucket >= k)) for k in range(1, REL_BUCKETS)]


_BUCKET_THRESHOLDS = _t5_bucket_thresholds()


def _bias_strip_kernel(relb_ref, o_ref, *, lead):
    head = pl.program_id(0)
    far = relb_ref[REL_BUCKETS - 1, head]
    shape = o_ref.shape
    dist = (lax.broadcasted_iota(jnp.int32, shape, 1) - lax.broadcasted_iota(jnp.int32, shape, 0) - lead)
    val = jnp.full(shape, relb_ref[0, head] - far, F32)
    for kk, thr in enumerate(_BUCKET_THRESHOLDS, start=1):
        val = jnp.where(dist >= thr, relb_ref[kk, head] - far, val)
    o_ref[...] = jnp.where(dist >= 0, val, MASK_VALUE)


def _bias_strip(rel_bias, *, lead, width):
    return pl.pallas_call(
        functools.partial(_bias_strip_kernel, lead=lead),
        grid=(HEADS,),
        in_specs=[pl.BlockSpec(memory_space=pltpu.SMEM)],
        out_specs=pl.BlockSpec((None, MOBA_BLOCK, width), lambda h: (h, 0, 0)),
        out_shape=jax.ShapeDtypeStruct((HEADS, MOBA_BLOCK, width), F32),
        compiler_params=pltpu.CompilerParams(dimension_semantics=("parallel",)),
        name="bias_strip",
    )(rel_bias)


def _attn_kernel(qt_ref, kaug_ref, vaug_ref, strip_ref, o_ref,
                 kmean, kmean_hi, kmean_lo, qaug_t, acc_t, mcol, logits_a, logits_b, smax,
                 *, tile_blocks):
    blk = MOBA_BLOCK
    qtile = tile_blocks * blk
    n_tiles = kaug_ref.shape[0] // qtile
    n_slots = kmean.shape[1]
    pv_rows = vaug_ref.shape[0] // 2
    lane = lax.broadcasted_iota(jnp.int32, (1, LANES), 1)
    dim = lax.broadcasted_iota(jnp.int32, (LANES, qtile), 0)

    def in_head(index, h):
        return (index >= HEAD_DIM * h) & (index < HEAD_DIM * (h + 1))

    def other_off(h):
        return HEAD_DIM * (1 - h)

    def keys(h, rows):
        return kaug_ref[rows, h * LANES:(h + 1) * LANES]

    def values_t(h, rows):
        return vaug_ref[h * pv_rows:(h + 1) * pv_rows, rows]

    def _prepare():
        kmean[...] = jnp.zeros(kmean.shape, F32)
        for h in range(2):
            head = in_head(lane, h)

            def block_mean(j, carry, h=h, head=head):
                kb = keys(h, pl.ds(pl.multiple_of(j * blk, blk), blk))
                mean = jnp.sum(kb.astype(F32), axis=0, keepdims=True) * (1.0 / blk)
                kmean[h, pl.ds(j, 1), :] = jnp.where(head, mean, 0.0)
                return carry

            lax.fori_loop(0, tile_blocks * n_tiles, block_mean, 0)
            km = kmean[h]
            hi = km.astype(BF16)
            kmean_hi[h] = hi
            kmean_lo[h] = (km - hi.astype(F32)).astype(BF16)

    _prepare()
    block_id = lax.broadcasted_iota(jnp.int32, (n_slots, qtile), 0)
    qcol = lax.broadcasted_iota(jnp.int32, (n_slots, qtile), 1)
    block_in_tile = sum((qcol >= r * blk).astype(jnp.int32) for r in range(1, tile_blocks))
    slots = (logits_a, logits_b)

    def block_rows(j, t):
        b_idx = jnp.maximum(tile_blocks * (j + 1) - 1 - t, 0)
        return pl.ds(pl.multiple_of(b_idx * blk, blk), blk)

    def compute_logits(j, t, slot, biased, heads=(0, 1)):
        for h in heads:
            s = _dot(keys(h, block_rows(j, t)), qaug_t[j % 2, h])
            if biased:
                s = s + strip_ref[h, :, t * blk:t * blk + qtile]
            slots[slot][h] = s.astype(BF16)
            smax[slot, h] = jnp.max(s, axis=0, keepdims=True).astype(BF16).astype(F32)

    def softmax_update(j, t, slot, heads=(0, 1)):
        for h in heads:
            s = slots[slot][h]
            m_old = mcol[j % 2, h]
            m_new = jnp.maximum(m_old, smax[slot, h])
            p = jnp.exp(s - m_new.astype(BF16))
            acc_t[j % 2, h] = (acc_t[j % 2, h] * jnp.exp(m_old - m_new)
                               + _dot(values_t(h, block_rows(j, t)), p))
            mcol[j % 2, h] = m_new

    def select_blocks(j):
        qt = qt_ref[:, pl.ds(pl.multiple_of(j * qtile, qtile), qtile)]
        past = block_id < tile_blocks * j + block_in_tile
        for h in range(2):
            off = other_off(h)
            qs = jnp.where(in_head(dim, h), qt * QUERY_SCALE, jnp.zeros_like(qt))
            gate = _dot(kmean_hi[h], qs) + _dot(kmean_lo[h], qs)
            gate = jnp.where(past, gate, -jnp.inf)
            chosen = jnp.zeros(gate.shape, jnp.bool_)
            for _ in range(MOBA_TOPK):
                top = jnp.max(gate, axis=0, keepdims=True)
                first = jnp.min(jnp.where(gate == top, block_id, n_slots), axis=0, keepdims=True)
                hit = block_id == first
                chosen = chosen | (hit & (top > -jnp.inf))
                gate = jnp.where(hit, -jnp.inf, gate)
            qaug_t[j % 2, h] = qs
            qaug_t[j % 2, h, off:off + n_slots, :] = jnp.where(
                past & jnp.logical_not(chosen), MASK_VALUE, 0.0).astype(BF16)

    n_lead = tile_blocks // 2

    def reset_accumulators(j):
        mcol[j % 2] = jnp.full(mcol.shape[1:], -jnp.inf, F32)
        acc_t[j % 2] = jnp.zeros(acc_t.shape[1:], F32)

    def step_pair(j, pair, lead=False):
        t = 2 * pair
        for step, slot in ((t, 0), (t + 1, 1)):
            biased = lead and step + 2 <= tile_blocks
            for h in range(2):
                softmax_update(j, step, slot, (h,))
                compute_logits(j, step + 2, slot, biased, (h,))

    def query_tile(j, carry):
        n_pairs = tile_blocks * (j + 1) // 2

        n_mid = jnp.maximum(n_pairs - 1 - n_lead, 0)

        def pairs_from(first, count):
            for k in range(count):
                step_pair(j, first + k)

        def four_pairs(k, carry):
            pairs_from(n_lead + 4 * k, 4)
            return carry

        lax.fori_loop(0, n_mid // 4, four_pairs, 0)

        @pl.when(n_mid % 4 >= 2)
        def _two_more_pairs():
            pairs_from(n_lead + n_mid // 4 * 4, 2)

        @pl.when(n_mid % 2 == 1)
        def _one_more_pair():
            pairs_from(n_lead + n_mid - 1, 1)

        nxt = jnp.where(j + 1 < n_tiles, j + 1, jnp.maximum(j - 1, 0))
        t_last = 2 * (n_pairs - 1)
        select_blocks(nxt)
        reset_accumulators(nxt)
        for slot in range(2):
            for h in range(2):
                softmax_update(j, t_last + slot, slot, (h,))
                compute_logits(nxt, slot, slot, True, (h,))
        a0 = acc_t[j % 2, 0]
        a1 = acc_t[j % 2, 1]
        o_t = jnp.concatenate([a0[:HEAD_DIM] / a0[HEAD_DIM:HEAD_DIM + 1],
                               a1[:HEAD_DIM] / a1[HEAD_DIM:HEAD_DIM + 1]], axis=0)
        o_ref[pl.ds(pl.multiple_of(j * qtile, qtile), qtile), :] = o_t.T.astype(o_ref.dtype)
        for pair in range(n_lead):
            step_pair(nxt, pair, True)
        return carry

    select_blocks(0)
    reset_accumulators(0)
    compute_logits(0, 0, 0, True)
    compute_logits(0, 1, 1, True)
    for pair in range(n_lead - 1):
        step_pair(0, pair, True)
    lax.fori_loop(0, n_tiles, query_tile, 0)


def _attention(q_t, kaug, vaug_t, rel_bias, *, tile_blocks=2):
    b, s, _ = kaug.shape
    blk = MOBA_BLOCK
    qtile = tile_blocks * blk
    n_slots = HEAD_DIM // 2
    assert tile_blocks % 2 == 0 and s % qtile == 0 and s // qtile >= 2 and s // blk <= n_slots
    pv_rows = HEAD_DIM + BF16_SUBLANES
    strip_width = 2 * qtile
    strip = _bias_strip(rel_bias, lead=(tile_blocks - 1) * blk, width=strip_width)
    scratch = [
        pltpu.VMEM((2, n_slots, LANES), F32),
        pltpu.VMEM((2, n_slots, LANES), BF16),
        pltpu.VMEM((2, n_slots, LANES), BF16),
        pltpu.VMEM((2, 2, LANES, qtile), BF16),
        pltpu.VMEM((2, 2, pv_rows, qtile), F32),
        pltpu.VMEM((2, 2, 1, qtile), F32),
        pltpu.VMEM((2, blk, qtile), BF16),
        pltpu.VMEM((2, blk, qtile), BF16),
        pltpu.VMEM((2, 2, 1, qtile), F32),
    ]
    vmem = (2 * (4 * s * LANES + 2 * pv_rows * s) * 2
            + 2 * 2 * blk * strip_width * 4
            + 4 * blk * qtile * 2 + 10 * blk * qtile * 4)
    return pl.pallas_call(
        functools.partial(_attn_kernel, tile_blocks=tile_blocks),
        grid=(b, HEAD_PAIRS),
        in_specs=[
            pl.BlockSpec((None, LANES, s), lambda bi, hp: (bi, hp, 0)),
            pl.BlockSpec((None, s, 2 * LANES), lambda bi, hp: (bi, 0, hp)),
            pl.BlockSpec((None, 2 * pv_rows, s), lambda bi, hp: (bi, hp, 0)),
            pl.BlockSpec((2, blk, strip_width), lambda bi, hp: (hp, 0, 0)),
        ],
        out_specs=pl.BlockSpec((None, s, LANES), lambda bi, hp: (bi, 0, hp)),
        out_shape=jax.ShapeDtypeStruct((b, s, ATTN_WIDTH), BF16),
        scratch_shapes=scratch,
        compiler_params=pltpu.CompilerParams(
            dimension_semantics=("parallel", "parallel"),
            vmem_limit_bytes=_vmem_limit(vmem)),
        name="moba_attn",
    )(q_t, kaug, vaug_t, strip)


def _ssm_weights(a_re, a_im, log_dt, b_re, b_im, c_re, c_im, d_skip):
    g, p_states, c = b_re.shape
    big_l = SSM_CHUNK
    dt = jnp.exp(log_dt.astype(F32))[:, None]
    ar = a_re.astype(F32)
    ai = a_im.astype(F32)
    mag = jnp.exp(dt * ar)
    ang = dt * ai
    abar_re = mag * jnp.cos(ang)
    abar_im = mag * jnp.sin(ang)
    den = ar * ar + ai * ai
    nr = abar_re - 1.0
    ni = abar_im
    fr = (nr * ar + ni * ai) / den
    fi = (ni * ar - nr * ai) / den
    br = b_re.astype(F32)
    bi = b_im.astype(F32)
    bbar_re = fr[..., None] * br - fi[..., None] * bi
    bbar_im = fr[..., None] * bi + fi[..., None] * br
    n = jnp.arange(big_l + 1, dtype=F32)[:, None, None]
    pmag = jnp.exp(n * (dt * ar)[None])
    pw_re = pmag * jnp.cos(n * ang[None])
    pw_im = pmag * jnp.sin(n * ang[None])
    cr = c_re.astype(F32)
    ci = c_im.astype(F32)
    ab_re = pw_re[..., None] * bbar_re[None] - pw_im[..., None] * bbar_im[None]
    ab_im = pw_re[..., None] * bbar_im[None] + pw_im[..., None] * bbar_re[None]
    hi = lax.Precision.HIGHEST
    kern = (jnp.einsum('gdp,ngpc->ngcd', cr, ab_re[:big_l], precision=hi)
            - jnp.einsum('gdp,ngpc->ngcd', ci, ab_im[:big_l], precision=hi))
    lag = np.arange(big_l)[None, :] - np.arange(big_l)[:, None]
    place = jnp.asarray(lag[:, :, None] == np.arange(big_l), F32)
    toep = jnp.einsum('stn,ngcd->gsctd', place, kern, precision=hi)
    toep = toep.reshape(g, big_l * c, big_l * c)
    skip = jnp.tile(d_skip.astype(F32).reshape(g, 1, c), (1, big_l, 1)).reshape(g, big_l * c)
    toep = toep + skip[:, :, None] * jnp.eye(big_l * c, dtype=F32)[None]
    e_re = ab_re[:big_l][::-1].transpose(1, 0, 3, 2).reshape(g, big_l * c, p_states)
    e_im = ab_im[:big_l][::-1].transpose(1, 0, 3, 2).reshape(g, big_l * c, p_states)
    w_in = jnp.concatenate([e_re, e_im, e_im, e_re], axis=-1)
    ca_re = cr[None] * pw_re[1:, :, None, :] - ci[None] * pw_im[1:, :, None, :]
    ca_im = cr[None] * pw_im[1:, :, None, :] + ci[None] * pw_re[1:, :, None, :]
    o_re = ca_re.transpose(1, 3, 0, 2).reshape(g, p_states, big_l * c)
    o_im = -ca_im.transpose(1, 3, 0, 2).reshape(g, p_states, big_l * c)
    w_out = jnp.concatenate([o_re, o_im], axis=1)
    are, aim = pw_re[big_l], pw_im[big_l]
    carry = jnp.stack([jnp.concatenate([are, are], -1),
                       jnp.concatenate([-aim, aim], -1),
                       jnp.concatenate([aim, -aim], -1)], axis=1)
    return toep.astype(BF16), w_in.astype(BF16), w_out.astype(BF16), carry


def _ssm_kernel(u_ref, toep_ref, win_ref, wout_ref, carry_ref, y_ref, e_sc, prev_sc, *, batch):
    n_groups = u_ref.shape[0]
    n_chunks = u_ref.shape[1] // batch
    half = LANES
    coef = []
    for g in range(n_groups):
        e = _dot(u_ref[g], win_ref[g])
        e_sc[g, 0] = e[:, :half]
        e_sc[g, 1] = e[:, half:]
        coef.append([jnp.broadcast_to(carry_ref[g, r:r + 1, :], (batch, half)) for r in range(3)])

    def step(kk, state):
        rows = pl.ds(kk, batch, stride=n_chunks)
        out = []
        for g in range(n_groups):
            st, st_swapped = state[2 * g], state[2 * g + 1]
            a1, a2, a3 = coef[g]
            prev_sc[g, rows, :] = st
            out.append(a1 * st + a2 * st_swapped + e_sc[g, 0, rows, :])
            out.append(a1 * st_swapped + a3 * st + e_sc[g, 1, rows, :])
        return tuple(out)

    zero = jnp.zeros((batch, half), F32)
    lax.fori_loop(0, n_chunks, step, (zero,) * (2 * n_groups), unroll=8)
    for g in range(n_groups):
        y = _dot(u_ref[g], toep_ref[g]) + _dot(prev_sc[g].astype(BF16), wout_ref[g])
        y_ref[g] = y.astype(y_ref.dtype)


def _ssm(u_t, toep, w_in, w_out, carry, *, batch, groups_per_step=2):
    g, rows, width = u_t.shape
    gps = groups_per_step
    vmem = gps * (2 * 2 * rows * width * 2 + rows * width * 4 + rows * LANES * 4 + 2 * rows * width * 4)

    def per_step(*tail):
        return pl.BlockSpec((gps,) + tail, lambda gi: (gi,) + (0,) * len(tail))

    return pl.pallas_call(
        functools.partial(_ssm_kernel, batch=batch),
        grid=(g // gps,),
        in_specs=[per_step(rows, width), per_step(width, width), per_step(width, width),
                  per_step(LANES, width), per_step(3, LANES)],
        out_specs=per_step(rows, width),
        out_shape=jax.ShapeDtypeStruct((g, rows, width), BF16),
        scratch_shapes=[pltpu.VMEM((gps, 2, rows, LANES), F32), pltpu.VMEM((gps, rows, LANES), F32)],
        compiler_params=pltpu.CompilerParams(
            dimension_semantics=("parallel",), vmem_limit_bytes=_vmem_limit(vmem)),
        name="s5_ssm",
    )(u_t, toep, w_in, w_out, carry)


def _final_kernel(x_ref, p_ref, oa_ref, yf_ref, za_ref, zs_ref, ga_ref, gs_ref,
                  wap_ref, wglu_ref, wsp_ref, wout_ref, wpg_ref, wpp_ref, lng_ref, lnb_ref, o_ref, ys_sc,
                  *, sub_rows):
    for sub in range(x_ref.shape[0] // sub_rows):
        _final_rows(sub, sub_rows, x_ref, p_ref, oa_ref, yf_ref, za_ref, zs_ref, ga_ref, gs_ref,
                    wap_ref, wglu_ref, wsp_ref, wout_ref, wpg_ref, wpp_ref, lng_ref, lnb_ref, o_ref, ys_sc)


def _final_rows(sub, sub_rows, x_ref, p_ref, oa_ref, yf_ref, za_ref, zs_ref, ga_ref, gs_ref,
                wap_ref, wglu_ref, wsp_ref, wout_ref, wpg_ref, wpp_ref, lng_ref, lnb_ref, o_ref, ys_sc):
    rows = slice(sub * sub_rows, (sub + 1) * sub_rows)
    n_chunks = sub_rows // SSM_CHUNK
    chunks = slice(sub * n_chunks, (sub + 1) * n_chunks)
    half = wpg_ref.shape[1] // 2
    x = x_ref[rows, :]
    xb = x.astype(BF16)
    za = za_ref[rows, :]
    a_in = oa_ref[rows, :] * (za * _sigmoid(za))
    y_a = _dot(a_in, wap_ref[...])
    gate_lo = _dot(xb, wpg_ref[:, :half])
    groups_per_tile = LANES // SSM_GROUP
    for t in range(SSM_CHUNK):
        for gb in range(ys_sc.shape[1]):
            tile = jnp.concatenate(
                [yf_ref[gb * groups_per_tile + gl, chunks, t * SSM_GROUP:(t + 1) * SSM_GROUP]
                 for gl in range(groups_per_tile)], axis=1)
            ys_sc[sub, gb, pl.ds(t, n_chunks, stride=SSM_CHUNK), :] = tile.astype(F32)
    ys = jnp.concatenate([ys_sc[sub, gb] for gb in range(ys_sc.shape[1])], axis=1)
    gelu = 0.5 * ys * (1.0 + lax.erf(ys * (2.0 ** -0.5)))
    glu = _dot(gelu.astype(BF16), wglu_ref[...])
    gate_hi = _dot(xb, wpg_ref[:, half:])
    zs = zs_ref[rows, :]
    s_in = glu[:, :SSM_WIDTH] * _sigmoid(glu[:, SSM_WIDTH:]) * (zs * _sigmoid(zs)).astype(F32)
    y_s = _dot(s_in.astype(BF16), wsp_ref[...])
    emb = _dot(p_ref[rows, :].astype(BF16), wpp_ref[...])
    merge = _sigmoid(ga_ref[rows, :]) * y_a.astype(BF16) + _sigmoid(gs_ref[rows, :]) * y_s.astype(BF16)
    mix = _dot(merge, wout_ref[...])
    ple = _sigmoid(jnp.concatenate([gate_lo, gate_hi], axis=1)) * emb
    hsum = DEEPNORM_ALPHA * x + mix + ple
    mu = jnp.mean(hsum, axis=-1, keepdims=True)
    cen = hsum - mu
    var = jnp.mean(cen * cen, axis=-1, keepdims=True)
    o_ref[rows, :] = cen * lax.rsqrt(var + LN_EPS) * lng_ref[...] + lnb_ref[...]


def _final(x2d, p2d, proj, o_a, y_fold, w_ap, w_glu, w_sp, w_out, w_pg, w_pp, ln_g, ln_b, *,
           row_tile=512, sub_rows=256):
    m = x2d.shape[0]
    half, full = SSM_WIDTH, D_MODEL

    def rows(width, col):
        return pl.BlockSpec((row_tile, width), lambda i: (i, col))

    def whole(arr):
        return pl.BlockSpec(arr.shape, lambda i: (0, 0))

    weights = (w_ap, w_glu, w_sp, w_out, w_pg, w_pp, ln_g, ln_b)
    vmem = (2 * sum(int(np.prod(w.shape)) * w.dtype.itemsize for w in weights)
            + 2 * row_tile * (2 * full * 4 + PLE_DIM * 4 + (4 * half + 2 * full) * 2)
            + 12 * row_tile * full * 4)
    return pl.pallas_call(
        functools.partial(_final_kernel, sub_rows=sub_rows),
        grid=(m // row_tile,),
        in_specs=[rows(full, 0), rows(PLE_DIM, 0), rows(half, 0),
                  pl.BlockSpec((SSM_GROUPS, row_tile // SSM_CHUNK, SSM_CHUNK * SSM_GROUP), lambda i: (0, i, 0)),
                  rows(half, COL_ZA), rows(half, COL_ZS), rows(full, COL_GA), rows(full, COL_GS)]
                 + [whole(w) for w in weights],
        out_specs=rows(full, 0),
        out_shape=jax.ShapeDtypeStruct((m, full), F32),
        scratch_shapes=[pltpu.VMEM((row_tile // sub_rows, half // LANES, sub_rows, LANES), F32)],
        compiler_params=pltpu.CompilerParams(
            dimension_semantics=("parallel",), vmem_limit_bytes=_vmem_limit(vmem)),
        name="final",
    )(x2d, p2d, o_a, y_fold, proj, proj, proj, proj, *weights)


def kernel(x, p, w_in, w_attn_proj, w_ssm_proj, w_out, ssm_a_re, ssm_a_im, ssm_log_dt, ssm_b_re, ssm_b_im, ssm_c_re, ssm_c_im, ssm_d, w_glu, w_ple_gate, w_ple_proj, ln_g, ln_b, rel_bias):
    b, s, d = x.shape
    m = b * s
    for i in range(w_in.shape[0]):
        x2d = x.reshape(m, d)
        wq, wk, wv, wza, wu, wzs, wga, wgs = jnp.split(w_in[i].astype(BF16), _IN_SPLITS, axis=1)
        w_main = jnp.concatenate([wga, wgs, wza, wzs], axis=1)
        proj, q_t, vaug_t, kaug, u_fold = _proj(
            x2d, w_main, jnp.concatenate([wq, wv], axis=1).T, wk, wu, batch=b)
        o_a = _attention(q_t, kaug.reshape(b, s, HEADS * LANES), vaug_t, rel_bias.astype(F32))
        toep, s_in, s_out, carry = _ssm_weights(
            ssm_a_re[i], ssm_a_im[i], ssm_log_dt[i], ssm_b_re[i], ssm_b_im[i],
            ssm_c_re[i], ssm_c_im[i], ssm_d[i].reshape(SSM_GROUPS, SSM_GROUP))
        y_fold = _ssm(u_fold, toep, s_in, s_out, carry, batch=b)
        x2d = _final(x2d, p[i].reshape(m, PLE_DIM), proj, o_a.reshape(m, ATTN_WIDTH), y_fold,
                     w_attn_proj[i].astype(BF16), w_glu[i].astype(BF16), w_ssm_proj[i].astype(BF16),
                     w_out[i].astype(BF16), w_ple_gate[i].astype(BF16), w_ple_proj[i].astype(BF16),
                     ln_g[i].astype(F32).reshape(1, d), ln_b[i].astype(F32).reshape(1, d))
        x = x2d.reshape(b, s, d)
    return x
```

```python
import functools
import math

import numpy as np
import jax
import jax.numpy as jnp
from jax import lax
from jax.experimental import pallas as pl
from jax.experimental.pallas import tpu as pltpu

F32 = jnp.float32
BF16 = jnp.bfloat16

LANES = 128
BF16_SUBLANES = 16
V7X_VMEM_BYTES = 64 * 1024 * 1024

D_MODEL = 1024
PLE_DIM = 256
HEADS = 8
HEAD_DIM = 64
QUERY_SCALE = HEAD_DIM ** -0.5
ATTN_WIDTH = HEADS * HEAD_DIM
HEAD_PAIRS = ATTN_WIDTH // LANES
MOBA_BLOCK = 256
MOBA_TOPK = 3
REL_BUCKETS = 32
REL_MAX_DIST = 128
SSM_WIDTH = 512
SSM_GROUP = 16
SSM_GROUPS = SSM_WIDTH // SSM_GROUP
SSM_STATE = 64
SSM_CHUNK = 16
IN_WIDTH = 4 * ATTN_WIDTH + 2 * SSM_WIDTH + 2 * D_MODEL
DEPTH = 1
DEEPNORM_ALPHA = (2.0 * DEPTH) ** 0.25
LN_EPS = 1e-5
MASK_VALUE = -1e30

PROJ_WIDTH = IN_WIDTH - 3 * ATTN_WIDTH - SSM_WIDTH
_IN_SPLITS = tuple(int(v) for v in np.cumsum(
    (ATTN_WIDTH,) * 4 + (SSM_WIDTH,) * 2 + (D_MODEL,) * 2)[:-1])
COL_GA, COL_GS = 0, 1
COL_ZA, COL_ZS = 4, 5


def _dot(a, b):
    return jnp.dot(a, b, preferred_element_type=F32)


def _dot_nt(a, b):
    return lax.dot_general(a, b, (((1,), (1,)), ((), ())), preferred_element_type=F32)


def _sigmoid(v):
    return 1.0 / (1.0 + jnp.exp(-v))


VMEM_INTERNAL_ALLOWANCE = 8 << 20
VMEM_LIMIT_FLOOR = 32 << 20
VMEM_LIMIT_CEILING = V7X_VMEM_BYTES - (4 << 20)


def _vmem_limit(nbytes):
    return int(min(VMEM_LIMIT_CEILING, max(nbytes + VMEM_INTERNAL_ALLOWANCE, VMEM_LIMIT_FLOOR)))


def _proj_kernel(x_ref, w_ref, wt_ref, wk_ref, wu_ref, o_ref, qt_ref, vaug_ref, kaug_ref, uf_ref, u_sc,
                 *, col_tile, tiles_per_batch):
    xb = x_ref[...].astype(BF16)
    u = _dot(xb, wu_ref[...])
    for gb in range(u_sc.shape[0]):
        u_sc[gb] = u[:, gb * LANES:(gb + 1) * LANES]
    n_cols = o_ref.shape[1]
    for start in range(0, n_cols, col_tile):
        cols = slice(start, min(start + col_tile, n_cols))
        o_ref[:, cols] = _dot(xb, w_ref[:, cols]).astype(BF16)
    t = _dot_nt(wt_ref[...], xb).astype(BF16)
    qt_ref[...] = t[:ATTN_WIDTH]
    pv_rows = vaug_ref.shape[0] // HEADS
    for h in range(HEADS):
        vaug_ref[h * pv_rows:h * pv_rows + HEAD_DIM, :] = t[ATTN_WIDTH + h * HEAD_DIM:ATTN_WIDTH + (h + 1) * HEAD_DIM]
        vaug_ref[h * pv_rows + HEAD_DIM:(h + 1) * pv_rows, :] = jnp.ones((pv_rows - HEAD_DIM, t.shape[1]), BF16)
    k = _dot(xb, wk_ref[...])
    lane = lax.broadcasted_iota(jnp.int32, (k.shape[0], LANES), 1)
    row = lax.broadcasted_iota(jnp.int32, (k.shape[0], LANES), 0)
    blocks_per_tile = k.shape[0] // MOBA_BLOCK
    block = (pl.program_id(0) % tiles_per_batch) * blocks_per_tile + sum(
        (row >= r * MOBA_BLOCK).astype(jnp.int32) for r in range(1, blocks_per_tile))
    for hp in range(HEAD_PAIRS):
        k_pair = k[:, hp * LANES:(hp + 1) * LANES]
        for h in range(2):
            head = (lane >= HEAD_DIM * h) & (lane < HEAD_DIM * (h + 1))
            onehot = jnp.where(lane - HEAD_DIM * (1 - h) == block, 1.0, 0.0)
            tile = 2 * hp + h
            kaug_ref[:, tile * LANES:(tile + 1) * LANES] = jnp.where(head, k_pair, onehot).astype(BF16)
    n_chunks = u_sc.shape[1] // SSM_CHUNK
    groups_per_tile = LANES // SSM_GROUP
    steps = [[u_sc[gb, pl.ds(t_, n_chunks, stride=SSM_CHUNK), :]
              for gb in range(u_sc.shape[0])] for t_ in range(SSM_CHUNK)]
    for g in range(SSM_GROUPS):
        gb, lo = g // groups_per_tile, (g % groups_per_tile) * SSM_GROUP
        folded = jnp.concatenate([steps[t_][gb][:, lo:lo + SSM_GROUP] for t_ in range(SSM_CHUNK)], axis=1)
        uf_ref[g] = folded.astype(BF16)


def _proj(x2d, w_bf16, wt_bf16, wk_bf16, wu_bf16, *, batch, row_tile=512, col_tile=1024):
    m, k = x2d.shape
    n = w_bf16.shape[1]
    nt = wt_bf16.shape[0]
    nk = wk_bf16.shape[1]
    nu = wu_bf16.shape[1]
    seq = m // batch
    tiles_per_batch = seq // row_tile
    fold_rows = row_tile // SSM_CHUNK
    pv_rows = HEAD_DIM + BF16_SUBLANES
    n_out = n + nt + nu + HEADS * (pv_rows + LANES)
    vmem = (2 * row_tile * k * 4 + 2 * k * (n + nt + nk + nu) * 2 + 2 * row_tile * n_out * 2
            + row_tile * k * 2 + 2 * row_tile * col_tile * 4 + row_tile * (nu + nt + nk) * 4)

    def per_batch(rows):
        return pl.BlockSpec((None, rows, row_tile), lambda i: (i // tiles_per_batch, 0, i % tiles_per_batch))

    return pl.pallas_call(
        functools.partial(_proj_kernel, col_tile=col_tile, tiles_per_batch=tiles_per_batch),
        grid=(m // row_tile,),
        in_specs=[pl.BlockSpec((row_tile, k), lambda i: (i, 0)),
                  pl.BlockSpec((k, n), lambda i: (0, 0)),
                  pl.BlockSpec((nt, k), lambda i: (0, 0)),
                  pl.BlockSpec((k, nk), lambda i: (0, 0)),
                  pl.BlockSpec((k, nu), lambda i: (0, 0))],
        out_specs=[pl.BlockSpec((row_tile, n), lambda i: (i, 0)),
                   per_batch(ATTN_WIDTH),
                   per_batch(HEADS * pv_rows),
                   pl.BlockSpec((row_tile, HEADS * LANES), lambda i: (i, 0)),
                   pl.BlockSpec((SSM_GROUPS, fold_rows, SSM_CHUNK * SSM_GROUP), lambda i: (0, i, 0))],
        out_shape=[jax.ShapeDtypeStruct((m, n), BF16),
                   jax.ShapeDtypeStruct((batch, ATTN_WIDTH, seq), BF16),
                   jax.ShapeDtypeStruct((batch, HEADS * pv_rows, seq), BF16),
                   jax.ShapeDtypeStruct((m, HEADS * LANES), BF16),
                   jax.ShapeDtypeStruct((SSM_GROUPS, m // SSM_CHUNK, SSM_CHUNK * SSM_GROUP), BF16)],
        scratch_shapes=[pltpu.VMEM((nu // LANES, row_tile, LANES), F32)],
        compiler_params=pltpu.CompilerParams(
            dimension_semantics=("parallel",), vmem_limit_bytes=_vmem_limit(vmem)),
        name="proj",
    )(x2d, w_bf16, wt_bf16, wk_bf16, wu_bf16)


def _t5_bucket_thresholds():
    max_exact = REL_BUCKETS // 2
    dist = np.arange(0, 2 * MOBA_BLOCK, dtype=np.int32)
    d = np.maximum(dist, 1).astype(np.float32)
    large = max_exact + (np.log(d / np.float32(max_exact)) / np.float32(math.log(REL_MAX_DIST / max_exact))
                         * np.float32(REL_BUCKETS - max_exact)).astype(np.int32)
    large = np.minimum(large, REL_BUCKETS - 1)
    bucket = np.where(dist < max_exact, dist, large)
    assert np.all(np.diff(bucket) >= 0) and bucket[-1] == REL_BUCKETS - 1
    return [int(np.argmax(bucket >= k)) for k in range(1, REL_BUCKETS)]


_BUCKET_THRESHOLDS = _t5_bucket_thresholds()


def _bias_strip_kernel(relb_ref, o_ref, *, lead):
    head = pl.program_id(0)
    far = relb_ref[REL_BUCKETS - 1, head]
    shape = o_ref.shape
    dist = (lax.broadcasted_iota(jnp.int32, shape, 1) - lax.broadcasted_iota(jnp.int32, shape, 0) - lead)
    val = jnp.full(shape, relb_ref[0, head] - far, F32)
    for kk, thr in enumerate(_BUCKET_THRESHOLDS, start=1):
        val = jnp.where(dist >= thr, relb_ref[kk, head] - far, val)
    o_ref[...] = jnp.where(dist >= 0, val, MASK_VALUE)


def _bias_strip(rel_bias, *, lead, width):
    return pl.pallas_call(
        functools.partial(_bias_strip_kernel, lead=lead),
        grid=(HEADS,),
        in_specs=[pl.BlockSpec(memory_space=pltpu.SMEM)],
        out_specs=pl.BlockSpec((None, MOBA_BLOCK, width), lambda h: (h, 0, 0)),
        out_shape=jax.ShapeDtypeStruct((HEADS, MOBA_BLOCK, width), F32),
        compiler_params=pltpu.CompilerParams(dimension_semantics=("parallel",)),
        name="bias_strip",
    )(rel_bias)


def _attn_kernel(qt_ref, kaug_ref, vaug_ref, strip_ref, o_ref,
                 kmean, kmean_hi, kmean_lo, qaug_t, acc_t, mcol, logits_a, logits_b, smax,
                 *, tile_blocks):
    blk = MOBA_BLOCK
    qtile = tile_blocks * blk
    n_tiles = kaug_ref.shape[0] // qtile
    n_slots = kmean.shape[1]
    pv_rows = vaug_ref.shape[0] // 2
    lane = lax.broadcasted_iota(jnp.int32, (1, LANES), 1)
    dim = lax.broadcasted_iota(jnp.int32, (LANES, qtile), 0)

    def in_head(index, h):
        return (index >= HEAD_DIM * h) & (index < HEAD_DIM * (h + 1))

    def other_off(h):
        return HEAD_DIM * (1 - h)

    def keys(h, rows):
        return kaug_ref[rows, h * LANES:(h + 1) * LANES]

    def values_t(h, rows):
        return vaug_ref[h * pv_rows:(h + 1) * pv_rows, rows]

    def _prepare():
        kmean[...] = jnp.zeros(kmean.shape, F32)
        for h in range(2):
            head = in_head(lane, h)

            def block_mean(j, carry, h=h, head=head):
                kb = keys(h, pl.ds(pl.multiple_of(j * blk, blk), blk))
                mean = jnp.sum(kb.astype(F32), axis=0, keepdims=True) * (1.0 / blk)
                kmean[h, pl.ds(j, 1), :] = jnp.where(head, mean, 0.0)
                return carry

            lax.fori_loop(0, tile_blocks * n_tiles, block_mean, 0)
            km = kmean[h]
            hi = km.astype(BF16)
            kmean_hi[h] = hi
            kmean_lo[h] = (km - hi.astype(F32)).astype(BF16)

    _prepare()
    block_id = lax.broadcasted_iota(jnp.int32, (n_slots, qtile), 0)
    qcol = lax.broadcasted_iota(jnp.int32, (n_slots, qtile), 1)
    block_in_tile = sum((qcol >= r * blk).astype(jnp.int32) for r in range(1, tile_blocks))
    slots = (logits_a, logits_b)

    def block_rows(j, t):
        b_idx = jnp.maximum(tile_blocks * (j + 1) - 1 - t, 0)
        return pl.ds(pl.multiple_of(b_idx * blk, blk), blk)

    def compute_logits(j, t, slot, biased, heads=(0, 1)):
        for h in heads:
            s = _dot(keys(h, block_rows(j, t)), qaug_t[j % 2, h])
            if biased:
                s = s + strip_ref[h, :, t * blk:t * blk + qtile]
            slots[slot][h] = s.astype(BF16)
            smax[slot, h] = jnp.max(s, axis=0, keepdims=True).astype(BF16).astype(F32)

    def softmax_update(j, t, slot, heads=(0, 1)):
        for h in heads:
            s = slots[slot][h]
            m_old = mcol[j % 2, h]
            m_new = jnp.maximum(m_old, smax[slot, h])
            p = jnp.exp(s - m_new.astype(BF16))
            acc_t[j % 2, h] = (acc_t[j % 2, h] * jnp.exp(m_old - m_new)
                               + _dot(values_t(h, block_rows(j, t)), p))
            mcol[j % 2, h] = m_new

    def select_blocks(j):
        qt = qt_ref[:, pl.ds(pl.multiple_of(j * qtile, qtile), qtile)]
        past = block_id < tile_blocks * j + block_in_tile
        for h in range(2):
            off = other_off(h)
            qs = jnp.where(in_head(dim, h), qt * QUERY_SCALE, jnp.zeros_like(qt))
            gate = _dot(kmean_hi[h], qs) + _dot(kmean_lo[h], qs)
            gate = jnp.where(past, gate, -jnp.inf)
            chosen = jnp.zeros(gate.shape, jnp.bool_)
            for _ in range(MOBA_TOPK):
                top = jnp.max(gate, axis=0, keepdims=True)
                first = jnp.min(jnp.where(gate == top, block_id, n_slots), axis=0, keepdims=True)
                hit = block_id == first
                chosen = chosen | (hit & (top > -jnp.inf))
                gate = jnp.where(hit, -jnp.inf, gate)
            qaug_t[j % 2, h] = qs
            qaug_t[j % 2, h, off:off + n_slots, :] = jnp.where(
                past & jnp.logical_not(chosen), MASK_VALUE, 0.0).astype(BF16)

    n_lead = tile_blocks // 2

    def reset_accumulators(j):
        mcol[j % 2] = jnp.full(mcol.shape[1:], -jnp.inf, F32)
        acc_t[j % 2] = jnp.zeros(acc_t.shape[1:], F32)

    def step_pair(j, pair, lead=False):
        t = 2 * pair
        for step, slot in ((t, 0), (t + 1, 1)):
            biased = lead and step + 2 <= tile_blocks
            for h in range(2):
                softmax_update(j, step, slot, (h,))
                compute_logits(j, step + 2, slot, biased, (h,))

    def query_tile(j, carry):
        n_pairs = tile_blocks * (j + 1) // 2

        n_mid = jnp.maximum(n_pairs - 1 - n_lead, 0)

        def pairs_from(first, count):
            for k in range(count):
                step_pair(j, first + k)

        def four_pairs(k, carry):
            pairs_from(n_lead + 4 * k, 4)
            return carry

        lax.fori_loop(0, n_mid // 4, four_pairs, 0)

        @pl.when(n_mid % 4 >= 2)
        def _two_more_pairs():
            pairs_from(n_lead + n_mid // 4 * 4, 2)

        @pl.when(n_mid % 2 == 1)
        def _one_more_pair():
            pairs_from(n_lead + n_mid - 1, 1)

        nxt = jnp.where(j + 1 < n_tiles, j + 1, jnp.maximum(j - 1, 0))
        t_last = 2 * (n_pairs - 1)
        select_blocks(nxt)
        reset_accumulators(nxt)
        for slot in range(2):
            for h in range(2):
                softmax_update(j, t_last + slot, slot, (h,))
                compute_logits(nxt, slot, slot, True, (h,))
        a0 = acc_t[j % 2, 0]
        a1 = acc_t[j % 2, 1]
        o_t = jnp.concatenate([a0[:HEAD_DIM] / a0[HEAD_DIM:HEAD_DIM + 1],
                               a1[:HEAD_DIM] / a1[HEAD_DIM:HEAD_DIM + 1]], axis=0)
        o_ref[pl.ds(pl.multiple_of(j * qtile, qtile), qtile), :] = o_t.T.astype(o_ref.dtype)
        for pair in range(n_lead):
            step_pair(nxt, pair, True)
        return carry

    select_blocks(0)
    reset_accumulators(0)
    compute_logits(0, 0, 0, True)
    compute_logits(0, 1, 1, True)
    for pair in range(n_lead - 1):
        step_pair(0, pair, True)
    lax.fori_loop(0, n_tiles, query_tile, 0)


def _attention(q_t, kaug, vaug_t, rel_bias, *, tile_blocks=2):
    b, s, _ = kaug.shape
    blk = MOBA_BLOCK
    qtile = tile_blocks * blk
    n_slots = HEAD_DIM // 2
    assert tile_blocks % 2 == 0 and s % qtile == 0 and s // qtile >= 2 and s // blk <= n_slots
    pv_rows = HEAD_DIM + BF16_SUBLANES
    strip_width = 2 * qtile
    strip = _bias_strip(rel_bias, lead=(tile_blocks - 1) * blk, width=strip_width)
    scratch = [
        pltpu.VMEM((2, n_slots, LANES), F32),
        pltpu.VMEM((2, n_slots, LANES), BF16),
        pltpu.VMEM((2, n_slots, LANES), BF16),
        pltpu.VMEM((2, 2, LANES, qtile), BF16),
        pltpu.VMEM((2, 2, pv_rows, qtile), F32),
        pltpu.VMEM((2, 2, 1, qtile), F32),
        pltpu.VMEM((2, blk, qtile), BF16),
        pltpu.VMEM((2, blk, qtile), BF16),
        pltpu.VMEM((2, 2, 1, qtile), F32),
    ]
    vmem = (2 * (4 * s * LANES + 2 * pv_rows * s) * 2
            + 2 * 2 * blk * strip_width * 4
            + 4 * blk * qtile * 2 + 10 * blk * qtile * 4)
    return pl.pallas_call(
        functools.partial(_attn_kernel, tile_blocks=tile_blocks),
        grid=(b, HEAD_PAIRS),
        in_specs=[
            pl.BlockSpec((None, LANES, s), lambda bi, hp: (bi, hp, 0)),
            pl.BlockSpec((None, s, 2 * LANES), lambda bi, hp: (bi, 0, hp)),
            pl.BlockSpec((None, 2 * pv_rows, s), lambda bi, hp: (bi, hp, 0)),
            pl.BlockSpec((2, blk, strip_width), lambda bi, hp: (hp, 0, 0)),
        ],
        out_specs=pl.BlockSpec((None, s, LANES), lambda bi, hp: (bi, 0, hp)),
        out_shape=jax.ShapeDtypeStruct((b, s, ATTN_WIDTH), BF16),
        scratch_shapes=scratch,
        compiler_params=pltpu.CompilerParams(
            dimension_semantics=("parallel", "parallel"),
            vmem_limit_bytes=_vmem_limit(vmem)),
        name="moba_attn",
    )(q_t, kaug, vaug_t, strip)


def _ssm_weights(a_re, a_im, log_dt, b_re, b_im, c_re, c_im, d_skip):
    g, p_states, c = b_re.shape
    big_l = SSM_CHUNK
    dt = jnp.exp(log_dt.astype(F32))[:, None]
    ar = a_re.astype(F32)
    ai = a_im.astype(F32)
    mag = jnp.exp(dt * ar)
    ang = dt * ai
    abar_re = mag * jnp.cos(ang)
    abar_im = mag * jnp.sin(ang)
    den = ar * ar + ai * ai
    nr = abar_re - 1.0
    ni = abar_im
    fr = (nr * ar + ni * ai) / den
    fi = (ni * ar - nr * ai) / den
    br = b_re.astype(F32)
    bi = b_im.astype(F32)
    bbar_re = fr[..., None] * br - fi[..., None] * bi
    bbar_im = fr[..., None] * bi + fi[..., None] * br
    n = jnp.arange(big_l + 1, dtype=F32)[:, None, None]
    pmag = jnp.exp(n * (dt * ar)[None])
    pw_re = pmag * jnp.cos(n * ang[None])
    pw_im = pmag * jnp.sin(n * ang[None])
    cr = c_re.astype(F32)
    ci = c_im.astype(F32)
    ab_re = pw_re[..., None] * bbar_re[None] - pw_im[..., None] * bbar_im[None]
    ab_im = pw_re[..., None] * bbar_im[None] + pw_im[..., None] * bbar_re[None]
    hi = lax.Precision.HIGHEST
    kern = (jnp.einsum('gdp,ngpc->ngcd', cr, ab_re[:big_l], precision=hi)
            - jnp.einsum('gdp,ngpc->ngcd', ci, ab_im[:big_l], precision=hi))
    lag = np.arange(big_l)[None, :] - np.arange(big_l)[:, None]
    place = jnp.asarray(lag[:, :, None] == np.arange(big_l), F32)
    toep = jnp.einsum('stn,ngcd->gsctd', place, kern, precision=hi)
    toep = toep.reshape(g, big_l * c, big_l * c)
    skip = jnp.tile(d_skip.astype(F32).reshape(g, 1, c), (1, big_l, 1)).reshape(g, big_l * c)
    toep = toep + skip[:, :, None] * jnp.eye(big_l * c, dtype=F32)[None]
    e_re = ab_re[:big_l][::-1].transpose(1, 0, 3, 2).reshape(g, big_l * c, p_states)
    e_im = ab_im[:big_l][::-1].transpose(1, 0, 3, 2).reshape(g, big_l * c, p_states)
    w_in = jnp.concatenate([e_re, e_im, e_im, e_re], axis=-1)
    ca_re = cr[None] * pw_re[1:, :, None, :] - ci[None] * pw_im[1:, :, None, :]
    ca_im = cr[None] * pw_im[1:, :, None, :] + ci[None] * pw_re[1:, :, None, :]
    o_re = ca_re.transpose(1, 3, 0, 2).reshape(g, p_states, big_l * c)
    o_im = -ca_im.transpose(1, 3, 0, 2).reshape(g, p_states, big_l * c)
    w_out = jnp.concatenate([o_re, o_im], axis=1)
    are, aim = pw_re[big_l], pw_im[big_l]
    carry = jnp.stack([jnp.concatenate([are, are], -1),
                       jnp.concatenate([-aim, aim], -1),
                       jnp.concatenate([aim, -aim], -1)], axis=1)
    return toep.astype(BF16), w_in.astype(BF16), w_out.astype(BF16), carry


def _ssm_kernel(u_ref, toep_ref, win_ref, wout_ref, carry_ref, y_ref, e_sc, prev_sc, *, batch):
    n_groups = u_ref.shape[0]
    n_chunks = u_ref.shape[1] // batch
    half = LANES
    coef = []
    for g in range(n_groups):
        e = _dot(u_ref[g], win_ref[g])
        e_sc[g, 0] = e[:, :half]
        e_sc[g, 1] = e[:, half:]
        coef.append([jnp.broadcast_to(carry_ref[g, r:r + 1, :], (batch, half)) for r in range(3)])

    def step(kk, state):
        rows = pl.ds(kk, batch, stride=n_chunks)
        out = []
        for g in range(n_groups):
            st, st_swapped = state[2 * g], state[2 * g + 1]
            a1, a2, a3 = coef[g]
            prev_sc[g, rows, :] = st
            out.append(a1 * st + a2 * st_swapped + e_sc[g, 0, rows, :])
            out.append(a1 * st_swapped + a3 * st + e_sc[g, 1, rows, :])
        return tuple(out)

    zero = jnp.zeros((batch, half), F32)
    lax.fori_loop(0, n_chunks, step, (zero,) * (2 * n_groups), unroll=8)
    for g in range(n_groups):
        y = _dot(u_ref[g], toep_ref[g]) + _dot(prev_sc[g].astype(BF16), wout_ref[g])
        y_ref[g] = y.astype(y_ref.dtype)


def _ssm(u_t, toep, w_in, w_out, carry, *, batch, groups_per_step=2):
    g, rows, width = u_t.shape
    gps = groups_per_step
    vmem = gps * (2 * 2 * rows * width * 2 + rows * width * 4 + rows * LANES * 4 + 2 * rows * width * 4)

    def per_step(*tail):
        return pl.BlockSpec((gps,) + tail, lambda gi: (gi,) + (0,) * len(tail))

    return pl.pallas_call(
        functools.partial(_ssm_kernel, batch=batch),
        grid=(g // gps,),
        in_specs=[per_step(rows, width), per_step(width, width), per_step(width, width),
                  per_step(LANES, width), per_step(3, LANES)],
        out_specs=per_step(rows, width),
        out_shape=jax.ShapeDtypeStruct((g, rows, width), BF16),
        scratch_shapes=[pltpu.VMEM((gps, 2, rows, LANES), F32), pltpu.VMEM((gps, rows, LANES), F32)],
        compiler_params=pltpu.CompilerParams(
            dimension_semantics=("parallel",), vmem_limit_bytes=_vmem_limit(vmem)),
        name="s5_ssm",
    )(u_t, toep, w_in, w_out, carry)


def _final_kernel(x_ref, p_ref, oa_ref, yf_ref, za_ref, zs_ref, ga_ref, gs_ref,
                  wap_ref, wglu_ref, wsp_ref, wout_ref, wpg_ref, wpp_ref, lng_ref, lnb_ref, o_ref, ys_sc,
                  *, sub_rows):
    for sub in range(x_ref.shape[0] // sub_rows):
        _final_rows(sub, sub_rows, x_ref, p_ref, oa_ref, yf_ref, za_ref, zs_ref, ga_ref, gs_ref,
                    wap_ref, wglu_ref, wsp_ref, wout_ref, wpg_ref, wpp_ref, lng_ref, lnb_ref, o_ref, ys_sc)


def _final_rows(sub, sub_rows, x_ref, p_ref, oa_ref, yf_ref, za_ref, zs_ref, ga_ref, gs_ref,
                wap_ref, wglu_ref, wsp_ref, wout_ref, wpg_ref, wpp_ref, lng_ref, lnb_ref, o_ref, ys_sc):
    rows = slice(sub * sub_rows, (sub + 1) * sub_rows)
    n_chunks = sub_rows // SSM_CHUNK
    chunks = slice(sub * n_chunks, (sub + 1) * n_chunks)
    half = wpg_ref.shape[1] // 2
    x = x_ref[rows, :]
    xb = x.astype(BF16)
    za = za_ref[rows, :]
    a_in = oa_ref[rows, :] * (za * _sigmoid(za))
    y_a = _dot(a_in, wap_ref[...])
    gate_lo = _dot(xb, wpg_ref[:, :half])
    groups_per_tile = LANES // SSM_GROUP
    folded = [yf_ref[g, chunks, :].astype(F32) for g in range(SSM_GROUPS)]
    for t in range(SSM_CHUNK):
        for gb in range(ys_sc.shape[1]):
            tile = jnp.concatenate(
                [folded[gb * groups_per_tile + gl][:, t * SSM_GROUP:(t + 1) * SSM_GROUP]
                 for gl in range(groups_per_tile)], axis=1)
            ys_sc[sub, gb, pl.ds(t, n_chunks, stride=SSM_CHUNK), :] = tile
    ys = jnp.concatenate([ys_sc[sub, gb] for gb in range(ys_sc.shape[1])], axis=1)
    gelu = 0.5 * ys * (1.0 + lax.erf(ys * (2.0 ** -0.5)))
    glu = _dot(gelu.astype(BF16), wglu_ref[...])
    gate_hi = _dot(xb, wpg_ref[:, half:])
    zs = zs_ref[rows, :]
    s_in = glu[:, :SSM_WIDTH] * _sigmoid(glu[:, SSM_WIDTH:]) * (zs * _sigmoid(zs)).astype(F32)
    y_s = _dot(s_in.astype(BF16), wsp_ref[...])
    emb = _dot(p_ref[rows, :].astype(BF16), wpp_ref[...])
    merge = _sigmoid(ga_ref[rows, :]) * y_a.astype(BF16) + _sigmoid(gs_ref[rows, :]) * y_s.astype(BF16)
    mix = _dot(merge, wout_ref[...])
    ple = _sigmoid(jnp.concatenate([gate_lo, gate_hi], axis=1)) * emb
    hsum = DEEPNORM_ALPHA * x + mix + ple
    mu = jnp.mean(hsum, axis=-1, keepdims=True)
    cen = hsum - mu
    var = jnp.mean(cen * cen, axis=-1, keepdims=True)
    o_ref[rows, :] = cen * lax.rsqrt(var + LN_EPS) * lng_ref[...] + lnb_ref[...]


def _final(x2d, p2d, proj, o_a, y_fold, w_ap, w_glu, w_sp, w_out, w_pg, w_pp, ln_g, ln_b, *,
           row_tile=512, sub_rows=256):
    m = x2d.shape[0]
    half, full = SSM_WIDTH, D_MODEL

    def rows(width, col):
        return pl.BlockSpec((row_tile, width), lambda i: (i, col))

    def whole(arr):
        return pl.BlockSpec(arr.shape, lambda i: (0, 0), pipeline_mode=pl.Buffered(1))

    weights = (w_ap, w_glu, w_sp, w_out, w_pg, w_pp, ln_g, ln_b)
    vmem = (2 * sum(int(np.prod(w.shape)) * w.dtype.itemsize for w in weights)
            + 2 * row_tile * (2 * full * 4 + PLE_DIM * 4 + (4 * half + 2 * full) * 2)
            + 12 * row_tile * full * 4)
    return pl.pallas_call(
        functools.partial(_final_kernel, sub_rows=sub_rows),
        grid=(m // row_tile,),
        in_specs=[rows(full, 0), rows(PLE_DIM, 0), rows(half, 0),
                  pl.BlockSpec((SSM_GROUPS, row_tile // SSM_CHUNK, SSM_CHUNK * SSM_GROUP), lambda i: (0, i, 0)),
                  rows(half, COL_ZA), rows(half, COL_ZS), rows(full, COL_GA), rows(full, COL_GS)]
                 + [whole(w) for w in weights],
        out_specs=rows(full, 0),
        out_shape=jax.ShapeDtypeStruct((m, full), F32),
        scratch_shapes=[pltpu.VMEM((row_tile // sub_rows, half // LANES, sub_rows, LANES), F32)],
        compiler_params=pltpu.CompilerParams(
            dimension_semantics=("parallel",), vmem_limit_bytes=_vmem_limit(vmem)),
        name="final",
    )(x2d, p2d, o_a, y_fold, proj, proj, proj, proj, *weights)


def kernel(x, p, w_in, w_attn_proj, w_ssm_proj, w_out, ssm_a_re, ssm_a_im, ssm_log_dt, ssm_b_re, ssm_b_im, ssm_c_re, ssm_c_im, ssm_d, w_glu, w_ple_gate, w_ple_proj, ln_g, ln_b, rel_bias):
    b, s, d = x.shape
    m = b * s
    for i in range(w_in.shape[0]):
        x2d = x.reshape(m, d)
        wq, wk, wv, wza, wu, wzs, wga, wgs = jnp.split(w_in[i].astype(BF16), _IN_SPLITS, axis=1)
        w_main = jnp.concatenate([wga, wgs, wza, wzs], axis=1)
        proj, q_t, vaug_t, kaug, u_fold = _proj(
            x2d, w_main, jnp.concatenate([wq, wv], axis=1).T, wk, wu, batch=b)
        o_a = _attention(q_t, kaug.reshape(b, s, HEADS * LANES), vaug_t, rel_bias.astype(F32))
        toep, s_in, s_out, carry = _ssm_weights(
            ssm_a_re[i], ssm_a_im[i], ssm_log_dt[i], ssm_b_re[i], ssm_b_im[i],
            ssm_c_re[i], ssm_c_im[i], ssm_d[i].reshape(SSM_GROUPS, SSM_GROUP))
        y_fold = _ssm(u_fold, toep, s_in, s_out, carry, batch=b)
        x2d = _final(x2d, p[i].reshape(m, PLE_DIM), proj, o_a.reshape(m, ATTN_WIDTH), y_fold,
                     w_attn_proj[i].astype(BF16), w_glu[i].astype(BF16), w_ssm_proj[i].astype(BF16),
                     w_out[i].astype(BF16), w_ple_gate[i].astype(BF16), w_ple_proj[i].astype(BF16),
                     ln_g[i].astype(F32).reshape(1, d), ln_b[i].astype(F32).reshape(1, d))
        x = x2d.reshape(b, s, d)
    return x
```
